```python
import jax, jax.numpy as jnp
from jax import lax
import numpy as np

D_MODEL = 1024
BATCH = 2
SEQ = 8192
DEPTH = 4

CHUNK = 64
N_LEFT_CHUNKS = 8
BAND = (N_LEFT_CHUNKS + 1) * CHUNK
HEAD_DIM = 64
N_HEADS_A = 8
N_HEADS_B = 8
A_W = N_HEADS_A * HEAD_DIM
B_W = N_HEADS_B * HEAD_DIM
MIX_W = A_W + B_W
REL_CLIP = 256
Q_BLOCK = 128
LRU_WIDTH = D_MODEL
LRU_BLOCKS = 4
LRU_BLOCK_W = LRU_WIDTH // LRU_BLOCKS
CONV_WIDTH = 4
LRU_C = 8.0
D_FF = -(-(8 * D_MODEL) // (3 * 256)) * 256
RMS_EPS = 1e-6
N_ATTN_LAYERS = (DEPTH + 1) // 2
N_REC_LAYERS = DEPTH // 2

kernel_name = "hybrid_chunked_sb_rglru_trunk"


def rmsnorm(x, g):
    xf = x.astype(jnp.float32)
    y = xf * lax.rsqrt(jnp.mean(xf * xf, axis=-1, keepdims=True) + RMS_EPS)
    return (y * g.astype(jnp.float32)).astype(x.dtype)


def chunked_relpos_attention(q, k, v, rel_bias):
    b, s, h, dh = q.shape
    nc = s // CHUNK
    qc = q.reshape(b, nc, CHUNK, h, dh)

    def gather_band(t):
        tc = t.reshape(b, nc, CHUNK, h, dh)
        tp = jnp.pad(tc, ((0, 0), (N_LEFT_CHUNKS, 0), (0, 0), (0, 0), (0, 0)))
        return jnp.concatenate([tp[:, j:j + nc] for j in range(N_LEFT_CHUNKS + 1)], axis=2)

    kb, vb = gather_band(k), gather_band(v)
    scores = jnp.einsum('bcqhd,bckhd->bhcqk', qc, kb).astype(jnp.float32) * (dh ** -0.5)
    qpos = N_LEFT_CHUNKS * CHUNK + jnp.arange(CHUNK)
    kpos = jnp.arange(BAND)
    rel = jnp.clip(qpos[:, None] - kpos[None, :], -REL_CLIP, REL_CLIP) + REL_CLIP
    bias = rel_bias[:, rel].astype(jnp.float32)
    key_chunk = jnp.arange(nc)[:, None] - N_LEFT_CHUNKS + (kpos // CHUNK)[None, :]
    valid = key_chunk >= 0
    scores = scores + bias[None, :, None, :, :]
    scores = jnp.where(valid[None, None, :, None, :], scores, -jnp.inf)
    p = jax.nn.softmax(scores, axis=-1).astype(v.dtype)
    o = jnp.einsum('bhcqk,bckhd->bcqhd', p, vb)
    return o.reshape(b, s, h * dh)


def stick_breaking_attention(q, k, v):
    b, s, h, dh = q.shape
    nb = s // Q_BLOCK
    scale = dh ** -0.5
    kpos = jnp.arange(s)
    q_blocks = q.reshape(b, nb, Q_BLOCK, h, dh).transpose(1, 0, 2, 3, 4)

    def one_block(args):
        q_blk, blk = args
        z = jnp.einsum('bqhd,bkhd->bhqk', q_blk, k).astype(jnp.float32) * scale
        qpos = blk * Q_BLOCK + jnp.arange(Q_BLOCK)
        causal = kpos[None, :] < qpos[:, None]
        log_beta = jax.nn.log_sigmoid(z)
        log_1m_beta = jnp.where(causal, jax.nn.log_sigmoid(-z), 0.0)
        after = lax.cumsum(log_1m_beta, axis=3, reverse=True) - log_1m_beta
        w = jnp.where(causal, jnp.exp(log_beta + after), 0.0).astype(v.dtype)
        return jnp.einsum('bhqk,bkhd->bqhd', w, v)

    o = lax.map(one_block, (q_blocks, jnp.arange(nb)))
    return o.transpose(1, 0, 2, 3, 4).reshape(b, s, h * dh)


def attention_mixer(h, w_in, rel_bias, w_out):
    b, s, _ = h.shape
    proj = h @ w_in
    part_a, part_b = proj[..., :3 * A_W], proj[..., 3 * A_W:]
    qa, ka, va = [t.reshape(b, s, N_HEADS_A, HEAD_DIM) for t in jnp.split(part_a, 3, axis=-1)]
    qs, ks, vs = [t.reshape(b, s, N_HEADS_B, HEAD_DIM) for t in jnp.split(part_b, 3, axis=-1)]
    out_a = chunked_relpos_attention(qa, ka, va, rel_bias)
    out_b = stick_breaking_attention(qs, ks, vs)
    return jnp.concatenate([out_a, out_b], axis=-1) @ w_out


def recurrent_mixer(h, w_in, conv_w, conv_b, w_a, b_a, w_i, b_i, lam, w_out):
    b, s, _ = h.shape
    proj = h @ w_in
    gate, xr = jnp.split(proj, 2, axis=-1)
    gate = jax.nn.gelu(gate, approximate=True)
    xc = lax.conv_general_dilated(
        xr, conv_w, window_strides=(1,), padding=[(CONV_WIDTH - 1, 0)],
        dimension_numbers=('NWC', 'WIO', 'NWC'), feature_group_count=LRU_WIDTH) + conv_b
    xg = xc.reshape(b, s, LRU_BLOCKS, LRU_BLOCK_W)
    r = jax.nn.sigmoid(jnp.einsum('bsni,nij->bsnj', xg, w_a) + b_a).reshape(b, s, LRU_WIDTH)
    i = jax.nn.sigmoid(jnp.einsum('bsni,nij->bsnj', xg, w_i) + b_i).reshape(b, s, LRU_WIDTH)
    log_a = -LRU_C * r.astype(jnp.float32) * jax.nn.softplus(-lam.astype(jnp.float32))
    a = jnp.exp(log_a)
    mult = jnp.sqrt(-jnp.expm1(2.0 * log_a))
    u = mult * (i * xc).astype(jnp.float32)

    def combine(left, right):
        a1, b1 = left
        a2, b2 = right
        return a1 * a2, a2 * b1 + b2

    _, hs = lax.associative_scan(combine, (a, u), axis=1)
    return (hs.astype(h.dtype) * gate) @ w_out


def swiglu(h, w_gate, w_up, w_down):
    return (jax.nn.silu(h @ w_gate) * (h @ w_up)) @ w_down


def setup_inputs(seed: int = 0) -> dict:
    key = jax.random.key(seed)
    ks = jax.random.split(key, 24)
    f32 = jnp.float32

    def nrm(k, shape, fan_in):
        return jax.random.normal(k, shape, f32) * (fan_in ** -0.5)

    def gain(k, shape):
        return 1.0 + 0.05 * jax.random.normal(k, shape, f32)

    u = jax.random.uniform(ks[12], (N_REC_LAYERS, LRU_WIDTH), f32, 0.9, 0.999)
    base = u ** (1.0 / LRU_C)
    lam = jnp.log(base) - jnp.log1p(-base)
    return {
        'x': jax.random.normal(ks[0], (BATCH, SEQ, D_MODEL), f32),
        'attn_w_in': nrm(ks[1], (N_ATTN_LAYERS, D_MODEL, 3 * MIX_W), D_MODEL),
        'attn_rel_bias': 0.2 * jax.random.normal(ks[2], (N_ATTN_LAYERS, N_HEADS_A, 2 * REL_CLIP + 1), f32),
        'attn_w_out': nrm(ks[3], (N_ATTN_LAYERS, MIX_W, D_MODEL), MIX_W),
        'rg_w_in': nrm(ks[4], (N_REC_LAYERS, D_MODEL, 2 * LRU_WIDTH), D_MODEL),
        'rg_conv_w': nrm(ks[5], (N_REC_LAYERS, CONV_WIDTH, 1, LRU_WIDTH), CONV_WIDTH),
        'rg_conv_b': 0.01 * jax.random.normal(ks[6], (N_REC_LAYERS, LRU_WIDTH), f32),
        'rg_w_a': nrm(ks[7], (N_REC_LAYERS, LRU_BLOCKS, LRU_BLOCK_W, LRU_BLOCK_W), LRU_BLOCK_W),
        'rg_b_a': 0.01 * jax.random.normal(ks[8], (N_REC_LAYERS, LRU_BLOCKS, LRU_BLOCK_W), f32),
        'rg_w_i': nrm(ks[9], (N_REC_LAYERS, LRU_BLOCKS, LRU_BLOCK_W, LRU_BLOCK_W), LRU_BLOCK_W),
        'rg_b_i': 0.01 * jax.random.normal(ks[10], (N_REC_LAYERS, LRU_BLOCKS, LRU_BLOCK_W), f32),
        'rg_lambda': lam,
        'rg_w_out': nrm(ks[11], (N_REC_LAYERS, LRU_WIDTH, D_MODEL), LRU_WIDTH),
        'norm_mix_pre': gain(ks[13], (DEPTH, D_MODEL)),
        'norm_mix_post': gain(ks[14], (DEPTH, D_MODEL)),
        'norm_ffn_pre': gain(ks[15], (DEPTH, D_MODEL)),
        'norm_ffn_post': gain(ks[16], (DEPTH, D_MODEL)),
        'ffn_w_gate': nrm(ks[17], (DEPTH, D_MODEL, D_FF), D_MODEL),
        'ffn_w_up': nrm(ks[18], (DEPTH, D_MODEL, D_FF), D_MODEL),
        'ffn_w_down': nrm(ks[19], (DEPTH, D_FF, D_MODEL), D_FF),
    }


def reference(x, attn_w_in, attn_rel_bias, attn_w_out, rg_w_in, rg_conv_w, rg_conv_b,
              rg_w_a, rg_b_a, rg_w_i, rg_b_i, rg_lambda, rg_w_out,
              norm_mix_pre, norm_mix_post, norm_ffn_pre, norm_ffn_post,
              ffn_w_gate, ffn_w_up, ffn_w_down):
    for layer in range(DEPTH):
        j = layer // 2
        h = rmsnorm(x, norm_mix_pre[layer])
        if layer % 2 == 0:
            m = attention_mixer(h, attn_w_in[j], attn_rel_bias[j], attn_w_out[j])
        else:
            m = recurrent_mixer(h, rg_w_in[j], rg_conv_w[j], rg_conv_b[j], rg_w_a[j], rg_b_a[j],
                                rg_w_i[j], rg_b_i[j], rg_lambda[j], rg_w_out[j])
        x = x + rmsnorm(m, norm_mix_post[layer])
        h = rmsnorm(x, norm_ffn_pre[layer])
        f = swiglu(h, ffn_w_gate[layer], ffn_w_up[layer], ffn_w_down[layer])
        x = x + rmsnorm(f, norm_ffn_post[layer])
    return x
```

```python
import functools

import jax
import jax.numpy as jnp
from jax import lax
from jax.experimental import pallas as pl
from jax.experimental.pallas import tpu as pltpu

F32 = jnp.float32
BF16 = jnp.bfloat16

D_MODEL = 1024
HEAD_DIM = 64
CHUNK = 64
N_LEFT_CHUNKS = 8
REL_CLIP = 256
A_W = 512
B_W = 512
LRU_BLOCKS = 4
LRU_BLOCK_W = 256
LRU_C = 8.0
RMS_EPS = 1e-6

LANES = 128
VMEM_LIMIT = 56 * 1024 * 1024

ROW_TILE = 512
FF_TILE = 256
SB_TILE = 256
A_TILE = 2 * CHUNK
A_WIN = (N_LEFT_CHUNKS + 2) * CHUNK
A_PAD = N_LEFT_CHUNKS * CHUNK
REC_TILE = 512
NEG_BIG = -1e30

_NT = (((1,), (1,)), ((), ()))


def _params(sem):
    return pltpu.CompilerParams(dimension_semantics=sem, vmem_limit_bytes=VMEM_LIMIT)


def _rms(x, g):
    ms = jnp.mean(x * x, axis=-1, keepdims=True)
    return x * lax.rsqrt(ms + RMS_EPS) * g


def _sigmoid(x):
    return 1.0 / (1.0 + jnp.exp(-x))


def _norm_linear_kernel(x_ref, g_ref, w_ref, o_ref):
    h = _rms(x_ref[...], g_ref[...]).astype(BF16)
    o_ref[...] = jnp.dot(h, w_ref[...], preferred_element_type=F32).astype(o_ref.dtype)


def norm_linear(x, g, w, out_dtype):
    m, d = x.shape
    n = w.shape[1]
    return pl.pallas_call(
        _norm_linear_kernel,
        out_shape=jax.ShapeDtypeStruct((m, n), out_dtype),
        grid=(m // ROW_TILE,),
        in_specs=[
            pl.BlockSpec((ROW_TILE, d), lambda i: (i, 0)),
            pl.BlockSpec((1, d), lambda i: (0, 0)),
            pl.BlockSpec((d, n), lambda i: (0, 0)),
        ],
        out_specs=pl.BlockSpec((ROW_TILE, n), lambda i: (i, 0)),
        compiler_params=_params(("arbitrary",)),
        name="norm_linear",
    )(x, g, w)


def _norm_linear_gelu_kernel(x_ref, g_ref, w_ref, gate_ref, xr_ref):
    h = _rms(x_ref[...], g_ref[...]).astype(BF16)
    n = gate_ref.shape[1]
    gate = jnp.dot(h, w_ref[:, :n], preferred_element_type=F32)
    c = 0.7978845608028654
    gate_ref[...] = 0.5 * gate * (1.0 + jnp.tanh(c * (gate + 0.044715 * (gate * gate * gate))))
    xr_ref[...] = jnp.dot(h, w_ref[:, n:], preferred_element_type=F32)


def norm_linear_gelu(x, g, w):
    m, d = x.shape
    n = w.shape[1] // 2
    return pl.pallas_call(
        _norm_linear_gelu_kernel,
        out_shape=(jax.ShapeDtypeStruct((m, n), F32), jax.ShapeDtypeStruct((m, n), F32)),
        grid=(m // ROW_TILE,),
        in_specs=[
            pl.BlockSpec((ROW_TILE, d), lambda i: (i, 0)),
            pl.BlockSpec((1, d), lambda i: (0, 0)),
            pl.BlockSpec((d, 2 * n), lambda i: (0, 0)),
        ],
        out_specs=(pl.BlockSpec((ROW_TILE, n), lambda i: (i, 0)),
                   pl.BlockSpec((ROW_TILE, n), lambda i: (i, 0))),
        compiler_params=_params(("arbitrary",)),
        name="norm_linear_gelu",
    )(x, g, w)


def _out_proj_kernel(n_in, *refs):
    x_ref, g_ref = refs[0], refs[1]
    ins = refs[2:2 + n_in]
    ws = refs[2 + n_in:2 + 2 * n_in]
    o_ref = refs[2 + 2 * n_in]
    m = jnp.dot(ins[0][...], ws[0][...], preferred_element_type=F32)
    for a, w in zip(ins[1:], ws[1:]):
        m = m + jnp.dot(a[...], w[...], preferred_element_type=F32)
    o_ref[...] = x_ref[...] + _rms(m, g_ref[...])


def out_proj_residual(x, g, ins, ws):
    m, d = x.shape
    n_in = len(ins)
    in_specs = [
        pl.BlockSpec((ROW_TILE, d), lambda i: (i, 0)),
        pl.BlockSpec((1, d), lambda i: (0, 0)),
    ]
    in_specs += [pl.BlockSpec((ROW_TILE, a.shape[1]), lambda i: (i, 0)) for a in ins]
    in_specs += [pl.BlockSpec(w.shape, lambda i: (0, 0)) for w in ws]
    return pl.pallas_call(
        functools.partial(_out_proj_kernel, n_in),
        out_shape=jax.ShapeDtypeStruct((m, d), F32),
        grid=(m // ROW_TILE,),
        in_specs=in_specs,
        out_specs=pl.BlockSpec((ROW_TILE, d), lambda i: (i, 0)),
        compiler_params=_params(("arbitrary",)),
        name="out_proj_residual",
    )(x, g, *ins, *ws)


def _ffn_kernel(x_ref, gpre_ref, gpost_ref, wg_ref, wu_ref, wd_ref, o_ref, h_ref, acc_ref):
    j = pl.program_id(1)

    @pl.when(j == 0)
    def _():
        h_ref[...] = _rms(x_ref[...], gpre_ref[...]).astype(BF16)
        acc_ref[...] = jnp.zeros_like(acc_ref)

    h = h_ref[...]
    gate = jnp.dot(h, wg_ref[...], preferred_element_type=F32)
    up = jnp.dot(h, wu_ref[...], preferred_element_type=F32)
    act = (gate * _sigmoid(gate) * up).astype(BF16)
    acc_ref[...] += jnp.dot(act, wd_ref[...], preferred_element_type=F32)

    @pl.when(j == pl.num_programs(1) - 1)
    def _():
        o_ref[...] = x_ref[...] + _rms(acc_ref[...], gpost_ref[...])


def ffn_residual(x, gpre, gpost, wg, wu, wd):
    m, d = x.shape
    f = wg.shape[1]
    return pl.pallas_call(
        _ffn_kernel,
        out_shape=jax.ShapeDtypeStruct((m, d), F32),
        grid=(m // ROW_TILE, f // FF_TILE),
        in_specs=[
            pl.BlockSpec((ROW_TILE, d), lambda i, j: (i, 0)),
            pl.BlockSpec((1, d), lambda i, j: (0, 0)),
            pl.BlockSpec((1, d), lambda i, j: (0, 0)),
            pl.BlockSpec((d, FF_TILE), lambda i, j: (0, j)),
            pl.BlockSpec((d, FF_TILE), lambda i, j: (0, j)),
            pl.BlockSpec((FF_TILE, d), lambda i, j: (j, 0)),
        ],
        out_specs=pl.BlockSpec((ROW_TILE, d), lambda i, j: (i, 0)),
        scratch_shapes=[pltpu.VMEM((ROW_TILE, d), BF16), pltpu.VMEM((ROW_TILE, d), F32)],
        compiler_params=_params(("arbitrary", "arbitrary")),
        name="ffn_residual",
    )(x, gpre, gpost, wg, wu, wd)


def _band_attn_kernel(q_ref, k_ref, v_ref, bias_ref, o_ref, kp_ref, vp_ref):
    t = pl.program_id(2)

    @pl.when(t == 0)
    def _():
        kp_ref[0:A_PAD, :] = jnp.zeros((A_PAD, LANES), BF16)
        vp_ref[0:A_PAD, :] = jnp.zeros((A_PAD, LANES), BF16)
        kp_ref[A_PAD:, :] = k_ref[0]
        vp_ref[A_PAD:, :] = v_ref[0]

    start = pl.multiple_of(t * A_TILE, A_TILE)
    kw = kp_ref[pl.ds(start, A_WIN), :]
    vw = vp_ref[pl.ds(start, A_WIN), :]
    lane = lax.broadcasted_iota(jnp.int32, (1, LANES), 1)
    first = lane < HEAD_DIM
    col = lax.broadcasted_iota(jnp.int32, (1, A_WIN), 1)
    in_seq = col >= (A_PAD - t * A_TILE)
    q2 = q_ref[0] * (HEAD_DIM ** -0.5)
    zero = jnp.zeros_like(q2)
    vzero = jnp.zeros_like(vw)
    out = None
    for h in range(2):
        sel = first if h == 0 else jnp.logical_not(first)
        qh = jnp.where(sel, q2, zero)
        vh = jnp.where(sel, vw, vzero)
        s = lax.dot_general(qh, kw, _NT, preferred_element_type=F32) + bias_ref[h]
        s = jnp.where(in_seq, s, NEG_BIG)
        mx = jnp.max(s, axis=-1, keepdims=True)
        p = jnp.exp(s - mx)
        l = jnp.sum(p, axis=-1, keepdims=True)
        o = jnp.dot(p.astype(BF16), vh, preferred_element_type=F32) * (1.0 / l)
        out = o if out is None else out + o
    o_ref[0] = out.astype(o_ref.dtype)


def band_attention(proj, bias, col0):
    b, s, _ = proj.shape
    nq = col0 // LANES
    nk = (col0 + A_W) // LANES
    nv = (col0 + 2 * A_W) // LANES
    return pl.pallas_call(
        _band_attn_kernel,
        out_shape=jax.ShapeDtypeStruct((b, s, A_W), BF16),
        grid=(b, A_W // LANES, s // A_TILE),
        in_specs=[
            pl.BlockSpec((1, A_TILE, LANES), lambda bi, p, t: (bi, t, nq + p)),
            pl.BlockSpec((1, s, LANES), lambda bi, p, t: (bi, 0, nk + p)),
            pl.BlockSpec((1, s, LANES), lambda bi, p, t: (bi, 0, nv + p)),
            pl.BlockSpec((2, A_TILE, A_WIN), lambda bi, p, t: (p, 0, 0)),
        ],
        out_specs=pl.BlockSpec((1, A_TILE, LANES), lambda bi, p, t: (bi, t, p)),
        scratch_shapes=[pltpu.VMEM((s + A_PAD, LANES), BF16), pltpu.VMEM((s + A_PAD, LANES), BF16)],
        compiler_params=_params(("arbitrary", "arbitrary", "arbitrary")),
        name="band_attention",
    )(proj, proj, proj, bias)


def band_bias(rel_bias):
    r = jnp.arange(A_TILE)[:, None]
    c = jnp.arange(A_WIN)[None, :]
    rel = jnp.clip(A_PAD + r - c, -REL_CLIP, REL_CLIP) + REL_CLIP
    dchunk = c // CHUNK - r // CHUNK
    ok = (dchunk >= 0) & (dchunk <= N_LEFT_CHUNKS)
    return jnp.where(ok[None], rel_bias[:, rel], NEG_BIG).astype(F32)


def _stick_kernel(q_ref, k_ref, v_ref, o_ref, acc_ref):
    qi = pl.program_id(2)
    t = SB_TILE
    lane = lax.broadcasted_iota(jnp.int32, (1, LANES), 1)
    first = lane < HEAD_DIM
    q2 = q_ref[0] * (HEAD_DIM ** -0.5)
    zero = jnp.zeros_like(q2)
    qh = (jnp.where(first, q2, zero), jnp.where(first, zero, q2))
    row = lax.broadcasted_iota(jnp.int32, (t, t), 0)
    col = lax.broadcasted_iota(jnp.int32, (t, t), 1)
    causal = col < row
    suffix = jnp.where(row >= col, 1.0, 0.0).astype(BF16)
    acc_ref[...] = jnp.zeros_like(acc_ref)

    def block(j, carry, diag):
        start = pl.multiple_of(j * t, t)
        kb = k_ref[0, pl.ds(start, t), :]
        vb = v_ref[0, pl.ds(start, t), :]
        vzero = jnp.zeros_like(vb)
        vh = (jnp.where(first, vb, vzero), jnp.where(first, vzero, vb))
        contrib = None
        new_carry = []
        for h in range(2):
            z = lax.dot_general(qh[h], kb, _NT, preferred_element_type=F32)
            sp = jnp.maximum(z, 0.0) + jnp.log(1.0 + jnp.exp(-jnp.abs(z)))
            if diag:
                sp = jnp.where(causal, sp, 0.0)
            hi = sp.astype(BF16)
            lo = (sp - hi.astype(F32)).astype(BF16)
            c = (jnp.dot(hi, suffix, preferred_element_type=F32)
                 + jnp.dot(lo, suffix, preferred_element_type=F32))
            w = jnp.exp(z - c - carry[h])
            if diag:
                w = jnp.where(causal, w, 0.0)
            pv = jnp.dot(w.astype(BF16), vh[h], preferred_element_type=F32)
            contrib = pv if contrib is None else contrib + pv
            new_carry.append(carry[h] + c[:, 0:1])
        acc_ref[...] += contrib
        return tuple(new_carry)

    carry = (jnp.zeros((t, 1), F32), jnp.zeros((t, 1), F32))
    carry = block(qi, carry, True)
    lax.fori_loop(0, qi, lambda i, cr: block(qi - 1 - i, cr, False), carry)
    o_ref[0] = acc_ref[...].astype(o_ref.dtype)


def stick_breaking_attention(proj, col0):
    b, s, _ = proj.shape
    nq = col0 // LANES
    nk = (col0 + B_W) // LANES
    nv = (col0 + 2 * B_W) // LANES
    return pl.pallas_call(
        _stick_kernel,
        out_shape=jax.ShapeDtypeStruct((b, s, B_W), BF16),
        grid=(b, B_W // LANES, s // SB_TILE),
        in_specs=[
            pl.BlockSpec((1, SB_TILE, LANES), lambda bi, p, t: (bi, t, nq + p)),
            pl.BlockSpec((1, s, LANES), lambda bi, p, t: (bi, 0, nk + p)),
            pl.BlockSpec((1, s, LANES), lambda bi, p, t: (bi, 0, nv + p)),
        ],
        out_specs=pl.BlockSpec((1, SB_TILE, LANES), lambda bi, p, t: (bi, t, p)),
        scratch_shapes=[pltpu.VMEM((SB_TILE, LANES), F32)],
        compiler_params=_params(("arbitrary", "arbitrary", "arbitrary")),
        name="stick_breaking",
    )(proj, proj, proj)


def _rglru_kernel(xr_ref, gate_ref, cw_ref, cb_ref, wa_ref, ba_ref, wi_ref, bi_ref, lam_ref,
                  y_ref, xpad_ref, a_ref, u_ref, h_ref):
    t = pl.program_id(1)
    n = REC_TILE
    width = xr_ref.shape[2]

    @pl.when(t == 0)
    def _():
        xpad_ref[0:8, :] = jnp.zeros((8, width), F32)
        h_ref[...] = jnp.zeros_like(h_ref)

    xpad_ref[8:8 + n, :] = xr_ref[0]
    cw = cw_ref[...]
    xc = (cb_ref[...]
          + cw[3:4] * xpad_ref[8:8 + n, :]
          + cw[2:3] * xpad_ref[7:7 + n, :]
          + cw[1:2] * xpad_ref[6:6 + n, :]
          + cw[0:1] * xpad_ref[5:5 + n, :])
    xpad_ref[0:8, :] = xpad_ref[n:n + 8, :]
    xcb = xc.astype(BF16)
    lam = lam_ref[...]
    sp_lam = jnp.maximum(-lam, 0.0) + jnp.log1p(jnp.exp(-jnp.abs(lam)))
    for blk in range(LRU_BLOCKS):
        sl = slice(blk * LRU_BLOCK_W, (blk + 1) * LRU_BLOCK_W)
        xb = xcb[:, sl]
        r = _sigmoid(jnp.dot(xb, wa_ref[blk], preferred_element_type=F32) + ba_ref[:, sl])
        i = _sigmoid(jnp.dot(xb, wi_ref[blk], preferred_element_type=F32) + bi_ref[:, sl])
        log_a = (-LRU_C * r) * sp_lam[:, sl]
        a = jnp.exp(log_a)
        mult = jnp.sqrt(-jnp.tanh(log_a) * (a * a + 1.0))
        a_ref[:, sl] = a
        u_ref[:, sl] = mult * (i * xc[:, sl])

    rowid = lax.broadcasted_iota(jnp.int32, (8, width), 0)

    def group(g, hprev):
        r0 = pl.multiple_of(g * 8, 8)
        a = a_ref[pl.ds(r0, 8), :]
        b = u_ref[pl.ds(r0, 8), :]
        for d in (1, 2, 4):
            keep = rowid >= d
            a_sh = jnp.where(keep, pltpu.roll(a, d, 0), 1.0)
            b_sh = jnp.where(keep, pltpu.roll(b, d, 0), 0.0)
            b = a * b_sh + b
            a = a * a_sh
        out = b + a * hprev
        u_ref[pl.ds(r0, 8), :] = out
        return out[7:8, :]

    h_ref[...] = lax.fori_loop(0, n // 8, group, h_ref[...], unroll=4)
    y_ref[0] = (u_ref[...] * gate_ref[0]).astype(y_ref.dtype)


def rglru_core(xr, gate, conv_w, conv_b, w_a, b_a, w_i, b_i, lam):
    b, s, width = xr.shape
    row = lambda bi, t: (0, 0)
    return pl.pallas_call(
        _rglru_kernel,
        out_shape=jax.ShapeDtypeStruct((b, s, width), BF16),
        grid=(b, s // REC_TILE),
        in_specs=[
            pl.BlockSpec((1, REC_TILE, width), lambda bi, t: (bi, t, 0)),
            pl.BlockSpec((1, REC_TILE, width), lambda bi, t: (bi, t, 0)),
            pl.BlockSpec(conv_w.shape, row),
            pl.BlockSpec((1, width), row),
            pl.BlockSpec(w_a.shape, lambda bi, t: (0, 0, 0)),
            pl.BlockSpec((1, width), row),
            pl.BlockSpec(w_i.shape, lambda bi, t: (0, 0, 0)),
            pl.BlockSpec((1, width), row),
            pl.BlockSpec((1, width), row),
        ],
        out_specs=pl.BlockSpec((1, REC_TILE, width), lambda bi, t: (bi, t, 0)),
        scratch_shapes=[
            pltpu.VMEM((REC_TILE + 8, width), F32),
            pltpu.VMEM((REC_TILE, width), F32),
            pltpu.VMEM((REC_TILE, width), F32),
            pltpu.VMEM((1, width), F32),
        ],
        compiler_params=_params(("arbitrary", "arbitrary")),
        name="rglru_core",
    )(xr, gate, conv_w, conv_b, w_a, b_a, w_i, b_i, lam)


def kernel(x, attn_w_in, attn_rel_bias, attn_w_out, rg_w_in, rg_conv_w, rg_conv_b, rg_w_a, rg_b_a,
           rg_w_i, rg_b_i, rg_lambda, rg_w_out, norm_mix_pre, norm_mix_post, norm_ffn_pre,
           norm_ffn_post, ffn_w_gate, ffn_w_up, ffn_w_down):
    b, s, d = x.shape
    depth = norm_mix_pre.shape[0]
    xf = x.reshape(b * s, d)
    row = lambda v: v.reshape(1, -1)
    for layer in range(depth):
        j = layer // 2
        g_pre = row(norm_mix_pre[layer])
        g_post = row(norm_mix_post[layer])
        if layer % 2 == 0:
            proj = norm_linear(xf, g_pre, attn_w_in[j].astype(BF16), BF16).reshape(b, s, -1)
            out_a = band_attention(proj, band_bias(attn_rel_bias[j]), 0)
            out_b = stick_breaking_attention(proj, 3 * A_W)
            w_out = attn_w_out[j].astype(BF16)
            xf = out_proj_residual(
                xf, g_post,
                [out_a.reshape(b * s, A_W), out_b.reshape(b * s, B_W)],
                [w_out[:A_W], w_out[A_W:]])
        else:
            gate, xr = norm_linear_gelu(xf, g_pre, rg_w_in[j].astype(BF16))
            width = xr.shape[1]
            y = rglru_core(
                xr.reshape(b, s, width), gate.reshape(b, s, width),
                rg_conv_w[j].reshape(-1, width), row(rg_conv_b[j]),
                rg_w_a[j].astype(BF16), row(rg_b_a[j]),
                rg_w_i[j].astype(BF16), row(rg_b_i[j]), row(rg_lambda[j]))
            xf = out_proj_residual(xf, g_post, [y.reshape(b * s, width)],
                                   [rg_w_out[j].astype(BF16)])
        xf = ffn_residual(
            xf, row(norm_ffn_pre[layer]), row(norm_ffn_post[layer]),
            ffn_w_gate[layer].astype(BF16), ffn_w_up[layer].astype(BF16),
            ffn_w_down[layer].astype(BF16))
    return xf.reshape(b, s, d)
```

```python
import functools

import jax
import jax.numpy as jnp
from jax import lax
from jax.experimental import pallas as pl
from jax.experimental.pallas import tpu as pltpu

F32 = jnp.float32
BF16 = jnp.bfloat16

D_MODEL = 1024
HEAD_DIM = 64
CHUNK = 64
N_LEFT_CHUNKS = 8
REL_CLIP = 256
A_W = 512
B_W = 512
LRU_BLOCKS = 4
LRU_BLOCK_W = 256
LRU_C = 8.0
RMS_EPS = 1e-6

LANES = 128
VMEM_LIMIT = 56 * 1024 * 1024

ROW_TILE = 512
FF_TILE = 256
SB_TILE = 256
SB_SUBTILES = 4
A_TILE = 2 * CHUNK
A_WIN = (N_LEFT_CHUNKS + 2) * CHUNK
A_PAD = N_LEFT_CHUNKS * CHUNK
REC_TILE = 512
NEG_BIG = -1e30
LOG2E = 1.4426950408889634

_NT = (((1,), (1,)), ((), ()))


def _params(sem):
    return pltpu.CompilerParams(dimension_semantics=sem, vmem_limit_bytes=VMEM_LIMIT)


def _rms(x, g):
    ms = jnp.mean(x * x, axis=-1, keepdims=True)
    return x * lax.rsqrt(ms + RMS_EPS) * g


def _sigmoid(x):
    return 1.0 / (1.0 + jnp.exp(-x))


def _norm_linear_kernel(x_ref, g_ref, w_ref, o_ref):
    h = _rms(x_ref[...], g_ref[...]).astype(BF16)
    o_ref[...] = jnp.dot(h, w_ref[...], preferred_element_type=F32).astype(o_ref.dtype)


def norm_linear(x, g, w, out_dtype):
    m, d = x.shape
    n = w.shape[1]
    return pl.pallas_call(
        _norm_linear_kernel,
        out_shape=jax.ShapeDtypeStruct((m, n), out_dtype),
        grid=(m // ROW_TILE,),
        in_specs=[
            pl.BlockSpec((ROW_TILE, d), lambda i: (i, 0)),
            pl.BlockSpec((1, d), lambda i: (0, 0)),
            pl.BlockSpec((d, n), lambda i: (0, 0)),
        ],
        out_specs=pl.BlockSpec((ROW_TILE, n), lambda i: (i, 0)),
        compiler_params=_params(("arbitrary",)),
        name="norm_linear",
    )(x, g, w)


def _norm_linear_gelu_kernel(x_ref, g_ref, w_ref, gate_ref, xr_ref):
    h = _rms(x_ref[...], g_ref[...]).astype(BF16)
    n = gate_ref.shape[1]
    gate = jnp.dot(h, w_ref[:, :n], preferred_element_type=F32)
    c = 0.7978845608028654
    gate_ref[...] = 0.5 * gate * (1.0 + jnp.tanh(c * (gate + 0.044715 * (gate * gate * gate))))
    xr_ref[...] = jnp.dot(h, w_ref[:, n:], preferred_element_type=F32)


def norm_linear_gelu(x, g, w):
    m, d = x.shape
    n = w.shape[1] // 2
    return pl.pallas_call(
        _norm_linear_gelu_kernel,
        out_shape=(jax.ShapeDtypeStruct((m, n), F32), jax.ShapeDtypeStruct((m, n), F32)),
        grid=(m // ROW_TILE,),
        in_specs=[
            pl.BlockSpec((ROW_TILE, d), lambda i: (i, 0)),
            pl.BlockSpec((1, d), lambda i: (0, 0)),
            pl.BlockSpec((d, 2 * n), lambda i: (0, 0)),
        ],
        out_specs=(pl.BlockSpec((ROW_TILE, n), lambda i: (i, 0)),
                   pl.BlockSpec((ROW_TILE, n), lambda i: (i, 0))),
        compiler_params=_params(("arbitrary",)),
        name="norm_linear_gelu",
    )(x, g, w)


def _out_proj_kernel(n_in, *refs):
    x_ref, g_ref = refs[0], refs[1]
    ins = refs[2:2 + n_in]
    ws = refs[2 + n_in:2 + 2 * n_in]
    o_ref = refs[2 + 2 * n_in]
    m = jnp.dot(ins[0][...], ws[0][...], preferred_element_type=F32)
    for a, w in zip(ins[1:], ws[1:]):
        m = m + jnp.dot(a[...], w[...], preferred_element_type=F32)
    o_ref[...] = x_ref[...] + _rms(m, g_ref[...])


def out_proj_residual(x, g, ins, ws):
    m, d = x.shape
    n_in = len(ins)
    in_specs = [
        pl.BlockSpec((ROW_TILE, d), lambda i: (i, 0)),
        pl.BlockSpec((1, d), lambda i: (0, 0)),
    ]
    in_specs += [pl.BlockSpec((ROW_TILE, a.shape[1]), lambda i: (i, 0)) for a in ins]
    in_specs += [pl.BlockSpec(w.shape, lambda i: (0, 0)) for w in ws]
    return pl.pallas_call(
        functools.partial(_out_proj_kernel, n_in),
        out_shape=jax.ShapeDtypeStruct((m, d), F32),
        grid=(m // ROW_TILE,),
        in_specs=in_specs,
        out_specs=pl.BlockSpec((ROW_TILE, d), lambda i: (i, 0)),
        compiler_params=_params(("arbitrary",)),
        name="out_proj_residual",
    )(x, g, *ins, *ws)


def _ffn_kernel(x_ref, gpre_ref, gpost_ref, wg_ref, wu_ref, wd_ref, o_ref, h_ref, acc_ref):
    j = pl.program_id(1)

    @pl.when(j == 0)
    def _():
        h_ref[...] = _rms(x_ref[...], gpre_ref[...]).astype(BF16)
        acc_ref[...] = jnp.zeros_like(acc_ref)

    h = h_ref[...]
    gate = jnp.dot(h, wg_ref[...], preferred_element_type=F32)
    up = jnp.dot(h, wu_ref[...], preferred_element_type=F32)
    act = (gate * _sigmoid(gate) * up).astype(BF16)
    acc_ref[...] += jnp.dot(act, wd_ref[...], preferred_element_type=F32)

    @pl.when(j == pl.num_programs(1) - 1)
    def _():
        o_ref[...] = x_ref[...] + _rms(acc_ref[...], gpost_ref[...])


def ffn_residual(x, gpre, gpost, wg, wu, wd):
    m, d = x.shape
    f = wg.shape[1]
    return pl.pallas_call(
        _ffn_kernel,
        out_shape=jax.ShapeDtypeStruct((m, d), F32),
        grid=(m // ROW_TILE, f // FF_TILE),
        in_specs=[
            pl.BlockSpec((ROW_TILE, d), lambda i, j: (i, 0)),
            pl.BlockSpec((1, d), lambda i, j: (0, 0)),
            pl.BlockSpec((1, d), lambda i, j: (0, 0)),
            pl.BlockSpec((d, FF_TILE), lambda i, j: (0, j)),
            pl.BlockSpec((d, FF_TILE), lambda i, j: (0, j)),
            pl.BlockSpec((FF_TILE, d), lambda i, j: (j, 0)),
        ],
        out_specs=pl.BlockSpec((ROW_TILE, d), lambda i, j: (i, 0)),
        scratch_shapes=[pltpu.VMEM((ROW_TILE, d), BF16), pltpu.VMEM((ROW_TILE, d), F32)],
        compiler_params=_params(("arbitrary", "arbitrary")),
        name="ffn_residual",
    )(x, gpre, gpost, wg, wu, wd)


def _band_attn_kernel(q_ref, k_ref, v_ref, bias_ref, o_ref, kp_ref, vp_ref):
    t = pl.program_id(2)

    @pl.when(t == 0)
    def _():
        kp_ref[0:A_PAD, :] = jnp.zeros((A_PAD, LANES), BF16)
        vp_ref[0:A_PAD, :] = jnp.zeros((A_PAD, LANES), BF16)
        kp_ref[A_PAD:, :] = k_ref[0]
        vp_ref[A_PAD:, :] = v_ref[0]

    start = pl.multiple_of(t * A_TILE, A_TILE)
    kw = kp_ref[pl.ds(start, A_WIN), :]
    vw = vp_ref[pl.ds(start, A_WIN), :]
    lane = lax.broadcasted_iota(jnp.int32, (1, LANES), 1)
    first = lane < HEAD_DIM
    col = lax.broadcasted_iota(jnp.int32, (1, A_WIN), 1)
    in_seq = col >= (A_PAD - t * A_TILE)
    q2 = q_ref[0] * (HEAD_DIM ** -0.5)
    zero = jnp.zeros_like(q2)
    vzero = jnp.zeros_like(vw)
    out = None
    for h in range(2):
        sel = first if h == 0 else jnp.logical_not(first)
        qh = jnp.where(sel, q2, zero)
        vh = jnp.where(sel, vw, vzero)
        s = lax.dot_general(qh, kw, _NT, preferred_element_type=F32) + bias_ref[h]
        s = jnp.where(in_seq, s, NEG_BIG)
        mx = jnp.max(s, axis=-1, keepdims=True)
        p = jnp.exp(s - mx)
        l = jnp.sum(p, axis=-1, keepdims=True)
        o = jnp.dot(p.astype(BF16), vh, preferred_element_type=F32) * (1.0 / l)
        out = o if out is None else out + o
    o_ref[0] = out.astype(o_ref.dtype)


def band_attention(proj, bias, col0):
    b, s, _ = proj.shape
    nq = col0 // LANES
    nk = (col0 + A_W) // LANES
    nv = (col0 + 2 * A_W) // LANES
    return pl.pallas_call(
        _band_attn_kernel,
        out_shape=jax.ShapeDtypeStruct((b, s, A_W), BF16),
        grid=(b, A_W // LANES, s // A_TILE),
        in_specs=[
            pl.BlockSpec((1, A_TILE, LANES), lambda bi, p, t: (bi, t, nq + p)),
            pl.BlockSpec((1, s, LANES), lambda bi, p, t: (bi, 0, nk + p)),
            pl.BlockSpec((1, s, LANES), lambda bi, p, t: (bi, 0, nv + p)),
            pl.BlockSpec((2, A_TILE, A_WIN), lambda bi, p, t: (p, 0, 0)),
        ],
        out_specs=pl.BlockSpec((1, A_TILE, LANES), lambda bi, p, t: (bi, t, p)),
        scratch_shapes=[pltpu.VMEM((s + A_PAD, LANES), BF16), pltpu.VMEM((s + A_PAD, LANES), BF16)],
        compiler_params=_params(("arbitrary", "arbitrary", "arbitrary")),
        name="band_attention",
    )(proj, proj, proj, bias)


def band_bias(rel_bias):
    h = rel_bias.shape[0]
    n_f = A_WIN + A_TILE - 1
    n_const = A_PAD + A_TILE - 1 - REL_CLIP
    far = jnp.broadcast_to(rel_bias[:, 2 * REL_CLIP:], (h, n_const))
    near = rel_bias[:, 2 * REL_CLIP + 1 - (n_f - n_const):][:, ::-1]
    f = jnp.concatenate([far, near], axis=1)
    period = n_f + 1
    g = jnp.concatenate([f[:, A_TILE - 1:], jnp.zeros((h, 1), F32), f[:, :A_TILE - 1]], axis=1)
    flat = jnp.tile(g, (1, A_TILE))[:, :A_TILE * (period - 1)]
    toep = flat.reshape(h, A_TILE, period - 1)[:, :, :A_WIN]
    r = jnp.arange(A_TILE)[:, None]
    c = jnp.arange(A_WIN)[None, :]
    dchunk = c // CHUNK - r // CHUNK
    ok = (dchunk >= 0) & (dchunk <= N_LEFT_CHUNKS)
    return jnp.where(ok[None], toep, NEG_BIG).astype(F32)


def _stick_kernel(q_ref, k_ref, v_ref, o_ref, acc_ref, carry_ref):
    qi = pl.program_id(2)
    t = SB_TILE
    ns = SB_SUBTILES
    lane = lax.broadcasted_iota(jnp.int32, (1, LANES), 1)
    first = lane < HEAD_DIM
    q2 = q_ref[0] * (HEAD_DIM ** -0.5)
    row = lax.broadcasted_iota(jnp.int32, (t, t), 0)
    col = lax.broadcasted_iota(jnp.int32, (t, t), 1)
    causal = col < row
    true = jnp.ones((t, t), jnp.bool_)
    suffix = jnp.where(row >= col, 1.0, 0.0).astype(BF16)
    suffix2 = jnp.concatenate([suffix, suffix], axis=0)
    zero = jnp.zeros((t, LANES), BF16)
    acc_ref[...] = jnp.zeros_like(acc_ref)
    carry_ref[...] = jnp.zeros_like(carry_ref)

    def split_heads(x):
        return jnp.concatenate([jnp.where(first, x, zero), jnp.where(first, zero, x)], axis=0)

    def process(j, d, diag):
        r0 = d * t
        n = (ns - d) * t
        start = pl.multiple_of(j * t, t)
        kcat = split_heads(k_ref[0, pl.ds(start, t), :])
        vcat = split_heads(v_ref[0, pl.ds(start, t), :])
        z = lax.dot_general(q2[r0:], kcat, _NT, preferred_element_type=F32)
        sp = jnp.maximum(z, 0.0) + jnp.log(1.0 + jnp.exp2(jnp.abs(z) * (-LOG2E)))
        if diag:
            mask = jnp.concatenate([causal] + [true] * (ns - d - 1), axis=0)
            mask = jnp.concatenate([mask, mask], axis=1)
            sp = jnp.where(mask, sp, 0.0)
        spb = sp.astype(BF16)
        c = [jnp.dot(spb[:, h * t:(h + 1) * t], suffix, preferred_element_type=F32) for h in range(2)]
        carry = [carry_ref[h, r0:, :] for h in range(2)]
        reps = t // LANES
        shift = jnp.concatenate([c[0]] + [c[1]], axis=1) + jnp.concatenate(
            [carry[0]] * reps + [carry[1]] * reps, axis=1)
        w = jnp.exp(z - shift)
        if diag:
            w = jnp.where(mask, w, 0.0)
        for h in range(2):
            carry_ref[h, r0:, :] = carry[h] + jnp.broadcast_to(c[h][:, 0:1], (n, LANES))
        acc_ref[r0:, :] += jnp.dot(w.astype(BF16), vcat, preferred_element_type=F32)

    for d in range(ns - 1, -1, -1):
        process(ns * qi + d, d, True)

    def body(i, _):
        process(ns * qi - 1 - i, 0, False)
        return 0

    lax.fori_loop(0, ns * qi, body, 0)
    o_ref[0] = acc_ref[...].astype(o_ref.dtype)


def stick_breaking_attention(proj, col0):
    b, s, _ = proj.shape
    nq = col0 // LANES
    nk = (col0 + B_W) // LANES
    nv = (col0 + 2 * B_W) // LANES
    tq = SB_SUBTILES * SB_TILE
    return pl.pallas_call(
        _stick_kernel,
        out_shape=jax.ShapeDtypeStruct((b, s, B_W), BF16),
        grid=(b, B_W // LANES, s // tq),
        in_specs=[
            pl.BlockSpec((1, tq, LANES), lambda bi, p, t: (bi, t, nq + p)),
            pl.BlockSpec((1, s, LANES), lambda bi, p, t: (bi, 0, nk + p)),
            pl.BlockSpec((1, s, LANES), lambda bi, p, t: (bi, 0, nv + p)),
        ],
        out_specs=pl.BlockSpec((1, tq, LANES), lambda bi, p, t: (bi, t, p)),
        scratch_shapes=[pltpu.VMEM((tq, LANES), F32), pltpu.VMEM((2, tq, LANES), F32)],
        compiler_params=_params(("arbitrary", "arbitrary", "arbitrary")),
        name="stick_breaking",
    )(proj, proj, proj)


def _rglru_kernel(xr_ref, gate_ref, cw_ref, cb_ref, wa_ref, ba_ref, wi_ref, bi_ref, lam_ref,
                  y_ref, xpad_ref, a_ref, u_ref, h_ref):
    t = pl.program_id(1)
    n = REC_TILE
    width = xr_ref.shape[2]

    @pl.when(t == 0)
    def _():
        xpad_ref[0:8, :] = jnp.zeros((8, width), F32)
        h_ref[...] = jnp.zeros_like(h_ref)

    xpad_ref[8:8 + n, :] = xr_ref[0]
    cw = cw_ref[...]
    xc = (cb_ref[...]
          + cw[3:4] * xpad_ref[8:8 + n, :]
          + cw[2:3] * xpad_ref[7:7 + n, :]
          + cw[1:2] * xpad_ref[6:6 + n, :]
          + cw[0:1] * xpad_ref[5:5 + n, :])
    xpad_ref[0:8, :] = xpad_ref[n:n + 8, :]
    xcb = xc.astype(BF16)
    lam = lam_ref[...]
    sp_lam = jnp.maximum(-lam, 0.0) + jnp.log1p(jnp.exp(-jnp.abs(lam)))
    for blk in range(LRU_BLOCKS):
        sl = slice(blk * LRU_BLOCK_W, (blk + 1) * LRU_BLOCK_W)
        xb = xcb[:, sl]
        r = _sigmoid(jnp.dot(xb, wa_ref[blk], preferred_element_type=F32) + ba_ref[:, sl])
        i = _sigmoid(jnp.dot(xb, wi_ref[blk], preferred_element_type=F32) + bi_ref[:, sl])
        log_a = (-LRU_C * r) * sp_lam[:, sl]
        a = jnp.exp(log_a)
        mult = jnp.sqrt(-jnp.tanh(log_a) * (a * a + 1.0))
        a_ref[:, sl] = a
        u_ref[:, sl] = mult * (i * xc[:, sl])

    rowid = lax.broadcasted_iota(jnp.int32, (8, width), 0)

    def group(g, hprev):
        r0 = pl.multiple_of(g * 8, 8)
        a = a_ref[pl.ds(r0, 8), :]
        b = u_ref[pl.ds(r0, 8), :]
        for d in (1, 2, 4):
            keep = rowid >= d
            a_sh = jnp.where(keep, pltpu.roll(a, d, 0), 1.0)
            b_sh = jnp.where(keep, pltpu.roll(b, d, 0), 0.0)
            b = a * b_sh + b
            a = a * a_sh
        out = b + a * hprev
        u_ref[pl.ds(r0, 8), :] = out
        return out[7:8, :]

    h_ref[...] = lax.fori_loop(0, n // 8, group, h_ref[...], unroll=4)
    y_ref[0] = (u_ref[...] * gate_ref[0]).astype(y_ref.dtype)


def rglru_core(xr, gate, conv_w, conv_b, w_a, b_a, w_i, b_i, lam):
    b, s, width = xr.shape
    row = lambda bi, t: (0, 0)
    return pl.pallas_call(
        _rglru_kernel,
        out_shape=jax.ShapeDtypeStruct((b, s, width), BF16),
        grid=(b, s // REC_TILE),
        in_specs=[
            pl.BlockSpec((1, REC_TILE, width), lambda bi, t: (bi, t, 0)),
            pl.BlockSpec((1, REC_TILE, width), lambda bi, t: (bi, t, 0)),
            pl.BlockSpec(conv_w.shape, row),
            pl.BlockSpec((1, width), row),
            pl.BlockSpec(w_a.shape, lambda bi, t: (0, 0, 0)),
            pl.BlockSpec((1, width), row),
            pl.BlockSpec(w_i.shape, lambda bi, t: (0, 0, 0)),
            pl.BlockSpec((1, width), row),
            pl.BlockSpec((1, width), row),
        ],
        out_specs=pl.BlockSpec((1, REC_TILE, width), lambda bi, t: (bi, t, 0)),
        scratch_shapes=[
            pltpu.VMEM((REC_TILE + 8, width), F32),
            pltpu.VMEM((REC_TILE, width), F32),
            pltpu.VMEM((REC_TILE, width), F32),
            pltpu.VMEM((1, width), F32),
        ],
        compiler_params=_params(("arbitrary", "arbitrary")),
        name="rglru_core",
    )(xr, gate, conv_w, conv_b, w_a, b_a, w_i, b_i, lam)


def kernel(x, attn_w_in, attn_rel_bias, attn_w_out, rg_w_in, rg_conv_w, rg_conv_b, rg_w_a, rg_b_a,
           rg_w_i, rg_b_i, rg_lambda, rg_w_out, norm_mix_pre, norm_mix_post, norm_ffn_pre,
           norm_ffn_post, ffn_w_gate, ffn_w_up, ffn_w_down):
    b, s, d = x.shape
    depth = norm_mix_pre.shape[0]
    xf = x.reshape(b * s, d)
    row = lambda v: v.reshape(1, -1)
    for layer in range(depth):
        j = layer // 2
        g_pre = row(norm_mix_pre[layer])
        g_post = row(norm_mix_post[layer])
        if layer % 2 == 0:
            proj = norm_linear(xf, g_pre, attn_w_in[j].astype(BF16), BF16).reshape(b, s, -1)
            out_a = band_attention(proj, band_bias(attn_rel_bias[j]), 0)
            out_b = stick_breaking_attention(proj, 3 * A_W)
            w_out = attn_w_out[j].astype(BF16)
            xf = out_proj_residual(
                xf, g_post,
                [out_a.reshape(b * s, A_W), out_b.reshape(b * s, B_W)],
                [w_out[:A_W], w_out[A_W:]])
        else:
            gate, xr = norm_linear_gelu(xf, g_pre, rg_w_in[j].astype(BF16))
            width = xr.shape[1]
            y = rglru_core(
                xr.reshape(b, s, width), gate.reshape(b, s, width),
                rg_conv_w[j].reshape(-1, width), row(rg_conv_b[j]),
                rg_w_a[j].astype(BF16), row(rg_b_a[j]),
                rg_w_i[j].astype(BF16), row(rg_b_i[j]), row(rg_lambda[j]))
            xf = out_proj_residual(xf, g_post, [y.reshape(b * s, width)],
                                   [rg_w_out[j].astype(BF16)])
        xf = ffn_residual(
            xf, row(norm_ffn_pre[layer]), row(norm_ffn_post[layer]),
            ffn_w_gate[layer].astype(BF16), ffn_w_up[layer].astype(BF16),
            ffn_w_down[layer].astype(BF16))
    return xf.reshape(b, s, d)
```

```python
import functools

import jax
import jax.numpy as jnp
from jax import lax
from jax.experimental import pallas as pl
from jax.experimental.pallas import tpu as pltpu

F32 = jnp.float32
BF16 = jnp.bfloat16

D_MODEL = 1024
HEAD_DIM = 64
CHUNK = 64
N_LEFT_CHUNKS = 8
REL_CLIP = 256
A_W = 512
B_W = 512
LRU_BLOCKS = 4
LRU_BLOCK_W = 256
LRU_C = 8.0
RMS_EPS = 1e-6

LANES = 128
VMEM_LIMIT = 56 * 1024 * 1024

ROW_TILE = 512
SB_TILE = 256
SB_SUBTILES = 4
A_TILE = 2 * CHUNK
A_SUBTILES = 4
A_WIN = (N_LEFT_CHUNKS + 2) * CHUNK
A_PAD = N_LEFT_CHUNKS * CHUNK
REC_TILE = 512
NEG_BIG = -1e30
LOG2E = 1.4426950408889634

_NT = (((1,), (1,)), ((), ()))


def _params(sem):
    return pltpu.CompilerParams(dimension_semantics=sem, vmem_limit_bytes=VMEM_LIMIT)


def _rms(x, g):
    ms = jnp.mean(x * x, axis=-1, keepdims=True)
    return x * lax.rsqrt(ms + RMS_EPS) * g


def _sigmoid(x):
    return 1.0 / (1.0 + jnp.exp(-x))


def _norm_linear_kernel(x_ref, g_ref, w_ref, o_ref):
    h = _rms(x_ref[...], g_ref[...]).astype(BF16)
    o_ref[...] = jnp.dot(h, w_ref[...], preferred_element_type=F32).astype(o_ref.dtype)


def norm_linear(x, g, w, out_dtype):
    m, d = x.shape
    n = w.shape[1]
    return pl.pallas_call(
        _norm_linear_kernel,
        out_shape=jax.ShapeDtypeStruct((m, n), out_dtype),
        grid=(m // ROW_TILE,),
        in_specs=[
            pl.BlockSpec((ROW_TILE, d), lambda i: (i, 0)),
            pl.BlockSpec((1, d), lambda i: (0, 0)),
            pl.BlockSpec((d, n), lambda i: (0, 0)),
        ],
        out_specs=pl.BlockSpec((ROW_TILE, n), lambda i: (i, 0)),
        compiler_params=_params(("arbitrary",)),
        name="norm_linear",
    )(x, g, w)


def _norm_linear_gelu_kernel(x_ref, g_ref, w_ref, gate_ref, xr_ref):
    h = _rms(x_ref[...], g_ref[...]).astype(BF16)
    n = gate_ref.shape[1]
    gate = jnp.dot(h, w_ref[:, :n], preferred_element_type=F32)
    c = 0.7978845608028654
    gate_ref[...] = 0.5 * gate * (1.0 + jnp.tanh(c * (gate + 0.044715 * (gate * gate * gate))))
    xr_ref[...] = jnp.dot(h, w_ref[:, n:], preferred_element_type=F32)


def norm_linear_gelu(x, g, w):
    m, d = x.shape
    n = w.shape[1] // 2
    return pl.pallas_call(
        _norm_linear_gelu_kernel,
        out_shape=(jax.ShapeDtypeStruct((m, n), F32), jax.ShapeDtypeStruct((m, n), F32)),
        grid=(m // ROW_TILE,),
        in_specs=[
            pl.BlockSpec((ROW_TILE, d), lambda i: (i, 0)),
            pl.BlockSpec((1, d), lambda i: (0, 0)),
            pl.BlockSpec((d, 2 * n), lambda i: (0, 0)),
        ],
        out_specs=(pl.BlockSpec((ROW_TILE, n), lambda i: (i, 0)),
                   pl.BlockSpec((ROW_TILE, n), lambda i: (i, 0))),
        compiler_params=_params(("arbitrary",)),
        name="norm_linear_gelu",
    )(x, g, w)


def _out_ffn_kernel(n_in, *refs):
    x_ref, gmix_ref, gpre_ref, gpost_ref, wg_ref, wu_ref, wd_ref = refs[:7]
    ins = refs[7:7 + n_in]
    ws = refs[7 + n_in:7 + 2 * n_in]
    o_ref = refs[7 + 2 * n_in]
    m = jnp.dot(ins[0][...], ws[0][...], preferred_element_type=F32)
    for a, w in zip(ins[1:], ws[1:]):
        m = m + jnp.dot(a[...], w[...], preferred_element_type=F32)
    x = x_ref[...] + _rms(m, gmix_ref[...])
    h = _rms(x, gpre_ref[...]).astype(BF16)
    gate = jnp.dot(h, wg_ref[...], preferred_element_type=F32)
    up = jnp.dot(h, wu_ref[...], preferred_element_type=F32)
    act = (gate * _sigmoid(gate) * up).astype(BF16)
    f = jnp.dot(act, wd_ref[...], preferred_element_type=F32)
    o_ref[...] = x + _rms(f, gpost_ref[...])


def _resident(shape):
    return pl.BlockSpec(shape, lambda i: (0,) * len(shape), pipeline_mode=pl.Buffered(1))


def mixer_out_ffn(x, gmix, ins, ws, gpre, gpost, wg, wu, wd):
    m, d = x.shape
    n_in = len(ins)
    in_specs = [pl.BlockSpec((ROW_TILE, d), lambda i: (i, 0))]
    in_specs += [_resident(a.shape) for a in (gmix, gpre, gpost, wg, wu, wd)]
    in_specs += [pl.BlockSpec((ROW_TILE, a.shape[1]), lambda i: (i, 0)) for a in ins]
    in_specs += [_resident(w.shape) for w in ws]
    return pl.pallas_call(
        functools.partial(_out_ffn_kernel, n_in),
        out_shape=jax.ShapeDtypeStruct((m, d), F32),
        grid=(m // ROW_TILE,),
        in_specs=in_specs,
        out_specs=pl.BlockSpec((ROW_TILE, d), lambda i: (i, 0)),
        compiler_params=_params(("arbitrary",)),
        name="mixer_out_ffn",
    )(x, gmix, gpre, gpost, wg, wu, wd, *ins, *ws)


def _band_attn_kernel(q_ref, k_ref, v_ref, bias_ref, o_ref, kp_ref, vp_ref):
    t = pl.program_id(2)

    @pl.when(t == 0)
    def _():
        kp_ref[0:A_PAD, :] = jnp.zeros((A_PAD, LANES), BF16)
        vp_ref[0:A_PAD, :] = jnp.zeros((A_PAD, LANES), BF16)
        kp_ref[A_PAD:, :] = k_ref[0]
        vp_ref[A_PAD:, :] = v_ref[0]

    lane = lax.broadcasted_iota(jnp.int32, (1, LANES), 1)
    first = lane < HEAD_DIM
    col = lax.broadcasted_iota(jnp.int32, (1, A_WIN), 1)
    bias = jnp.concatenate([bias_ref[0], bias_ref[1]], axis=0)
    zero = jnp.zeros((A_TILE, LANES), BF16)
    for u in range(A_SUBTILES):
        tile = t * A_SUBTILES + u
        start = pl.multiple_of(tile * A_TILE, A_TILE)
        kw = kp_ref[pl.ds(start, A_WIN), :]
        vw = vp_ref[pl.ds(start, A_WIN), :]
        in_seq = col >= (A_PAD - tile * A_TILE)
        q2 = q_ref[0, u * A_TILE:(u + 1) * A_TILE, :] * (HEAD_DIM ** -0.5)
        qs = jnp.concatenate([jnp.where(first, q2, zero), jnp.where(first, zero, q2)], axis=0)
        s = lax.dot_general(qs, kw, _NT, preferred_element_type=F32) + bias
        s = jnp.where(in_seq, s, NEG_BIG)
        mx = jnp.max(s, axis=-1, keepdims=True)
        p = jnp.exp(s - mx)
        l = jnp.sum(p, axis=-1, keepdims=True)
        o = jnp.dot(p.astype(BF16), vw, preferred_element_type=F32) * (1.0 / l)
        out = jnp.where(first, o[:A_TILE], o[A_TILE:])
        o_ref[0, u * A_TILE:(u + 1) * A_TILE, :] = out.astype(o_ref.dtype)


def band_attention(proj, bias, col0):
    b, s, _ = proj.shape
    nq = col0 // LANES
    nk = (col0 + A_W) // LANES
    nv = (col0 + 2 * A_W) // LANES
    tq = A_SUBTILES * A_TILE
    return pl.pallas_call(
        _band_attn_kernel,
        out_shape=jax.ShapeDtypeStruct((b, s, A_W), BF16),
        grid=(b, A_W // LANES, s // tq),
        in_specs=[
            pl.BlockSpec((1, tq, LANES), lambda bi, p, t: (bi, t, nq + p)),
            pl.BlockSpec((1, s, LANES), lambda bi, p, t: (bi, 0, nk + p)),
            pl.BlockSpec((1, s, LANES), lambda bi, p, t: (bi, 0, nv + p)),
            pl.BlockSpec((2, A_TILE, A_WIN), lambda bi, p, t: (p, 0, 0)),
        ],
        out_specs=pl.BlockSpec((1, tq, LANES), lambda bi, p, t: (bi, t, p)),
        scratch_shapes=[pltpu.VMEM((s + A_PAD, LANES), BF16), pltpu.VMEM((s + A_PAD, LANES), BF16)],
        compiler_params=_params(("arbitrary", "arbitrary", "arbitrary")),
        name="band_attention",
    )(proj, proj, proj, bias)


def band_bias(rel_bias):
    h = rel_bias.shape[0]
    n_f = A_WIN + A_TILE - 1
    n_const = A_PAD + A_TILE - 1 - REL_CLIP
    far = jnp.broadcast_to(rel_bias[:, 2 * REL_CLIP:], (h, n_const))
    near = rel_bias[:, 2 * REL_CLIP + 1 - (n_f - n_const):][:, ::-1]
    f = jnp.concatenate([far, near], axis=1)
    period = n_f + 1
    g = jnp.concatenate([f[:, A_TILE - 1:], jnp.zeros((h, 1), F32), f[:, :A_TILE - 1]], axis=1)
    flat = jnp.tile(g, (1, A_TILE))[:, :A_TILE * (period - 1)]
    toep = flat.reshape(h, A_TILE, period - 1)[:, :, :A_WIN]
    r = jnp.arange(A_TILE)[:, None]
    c = jnp.arange(A_WIN)[None, :]
    dchunk = c // CHUNK - r // CHUNK
    ok = (dchunk >= 0) & (dchunk <= N_LEFT_CHUNKS)
    return jnp.where(ok[None], toep, NEG_BIG).astype(F32)


def _stick_kernel(q_ref, k_ref, v_ref, o_ref, acc_ref, carry_ref):
    qi = pl.program_id(2)
    t = SB_TILE
    ns = SB_SUBTILES
    lane = lax.broadcasted_iota(jnp.int32, (1, LANES), 1)
    first = lane < HEAD_DIM
    q2 = q_ref[0] * (HEAD_DIM ** -0.5)
    row = lax.broadcasted_iota(jnp.int32, (t, t), 0)
    col = lax.broadcasted_iota(jnp.int32, (t, t), 1)
    causal = col < row
    true = jnp.ones((t, t), jnp.bool_)
    suffix = jnp.where(row >= col, 1.0, 0.0).astype(BF16)
    suffix2 = jnp.concatenate([suffix, suffix], axis=0)
    zero = jnp.zeros((t, LANES), BF16)
    acc_ref[...] = jnp.zeros_like(acc_ref)
    carry_ref[...] = jnp.zeros_like(carry_ref)

    def split_heads(x):
        return jnp.concatenate([jnp.where(first, x, zero), jnp.where(first, zero, x)], axis=0)

    def process(j, d, diag):
        r0 = d * t
        n = (ns - d) * t
        start = pl.multiple_of(j * t, t)
        kcat = split_heads(k_ref[0, pl.ds(start, t), :])
        vcat = split_heads(v_ref[0, pl.ds(start, t), :])
        z = lax.dot_general(q2[r0:], kcat, _NT, preferred_element_type=F32)
        sp = jnp.maximum(z, 0.0) + jnp.log(1.0 + jnp.exp2(jnp.abs(z) * (-LOG2E)))
        if diag:
            mask = jnp.concatenate([causal] + [true] * (ns - d - 1), axis=0)
            mask = jnp.concatenate([mask, mask], axis=1)
            sp = jnp.where(mask, sp, 0.0)
        spb = sp.astype(BF16)
        c = [jnp.dot(spb[:, h * t:(h + 1) * t], suffix, preferred_element_type=F32) for h in range(2)]
        carry = [carry_ref[h, r0:, :] for h in range(2)]
        reps = t // LANES
        shift = jnp.concatenate([c[0]] + [c[1]], axis=1) + jnp.concatenate(
            [carry[0]] * reps + [carry[1]] * reps, axis=1)
        w = jnp.exp(z - shift)
        if diag:
            w = jnp.where(mask, w, 0.0)
        for h in range(2):
            carry_ref[h, r0:, :] = carry[h] + jnp.broadcast_to(c[h][:, 0:1], (n, LANES))
        acc_ref[r0:, :] += jnp.dot(w.astype(BF16), vcat, preferred_element_type=F32)

    for d in range(ns - 1, -1, -1):
        process(ns * qi + d, d, True)

    def body(i, _):
        process(ns * qi - 1 - i, 0, False)
        return 0

    lax.fori_loop(0, ns * qi, body, 0)
    o_ref[0] = acc_ref[...].astype(o_ref.dtype)


def stick_breaking_attention(proj, col0):
    b, s, _ = proj.shape
    nq = col0 // LANES
    nk = (col0 + B_W) // LANES
    nv = (col0 + 2 * B_W) // LANES
    tq = SB_SUBTILES * SB_TILE
    return pl.pallas_call(
        _stick_kernel,
        out_shape=jax.ShapeDtypeStruct((b, s, B_W), BF16),
        grid=(b, B_W // LANES, s // tq),
        in_specs=[
            pl.BlockSpec((1, tq, LANES), lambda bi, p, t: (bi, t, nq + p)),
            pl.BlockSpec((1, s, LANES), lambda bi, p, t: (bi, 0, nk + p)),
            pl.BlockSpec((1, s, LANES), lambda bi, p, t: (bi, 0, nv + p)),
        ],
        out_specs=pl.BlockSpec((1, tq, LANES), lambda bi, p, t: (bi, t, p)),
        scratch_shapes=[pltpu.VMEM((tq, LANES), F32), pltpu.VMEM((2, tq, LANES), F32)],
        compiler_params=_params(("arbitrary", "arbitrary", "arbitrary")),
        name="stick_breaking",
    )(proj, proj, proj)


def _rglru_kernel(xr_ref, gate_ref, cw_ref, cb_ref, wa_ref, ba_ref, wi_ref, bi_ref, lam_ref,
                  y_ref, xpad_ref, a_ref, u_ref, h_ref):
    t = pl.program_id(1)
    n = REC_TILE
    width = xr_ref.shape[2]

    @pl.when(t == 0)
    def _():
        xpad_ref[0:8, :] = jnp.zeros((8, width), F32)
        h_ref[...] = jnp.zeros_like(h_ref)

    xpad_ref[8:8 + n, :] = xr_ref[0]
    cw = cw_ref[...]
    xc = (cb_ref[...]
          + cw[3:4] * xpad_ref[8:8 + n, :]
          + cw[2:3] * xpad_ref[7:7 + n, :]
          + cw[1:2] * xpad_ref[6:6 + n, :]
          + cw[0:1] * xpad_ref[5:5 + n, :])
    xpad_ref[0:8, :] = xpad_ref[n:n + 8, :]
    xcb = xc.astype(BF16)
    lam = lam_ref[...]
    sp_lam = jnp.maximum(-lam, 0.0) + jnp.log1p(jnp.exp(-jnp.abs(lam)))
    for blk in range(LRU_BLOCKS):
        sl = slice(blk * LRU_BLOCK_W, (blk + 1) * LRU_BLOCK_W)
        xb = xcb[:, sl]
        r = _sigmoid(jnp.dot(xb, wa_ref[blk], preferred_element_type=F32) + ba_ref[:, sl])
        i = _sigmoid(jnp.dot(xb, wi_ref[blk], preferred_element_type=F32) + bi_ref[:, sl])
        log_a = (-LRU_C * r) * sp_lam[:, sl]
        a = jnp.exp(log_a)
        mult = jnp.sqrt(-jnp.tanh(log_a) * (a * a + 1.0))
        a_ref[:, sl] = a
        u_ref[:, sl] = mult * (i * xc[:, sl])

    rowid = lax.broadcasted_iota(jnp.int32, (8, width), 0)

    def group(g, hprev):
        r0 = pl.multiple_of(g * 8, 8)
        a = a_ref[pl.ds(r0, 8), :]
        b = u_ref[pl.ds(r0, 8), :]
        for d in (1, 2, 4):
            keep = rowid >= d
            a_sh = jnp.where(keep, pltpu.roll(a, d, 0), 1.0)
            b_sh = jnp.where(keep, pltpu.roll(b, d, 0), 0.0)
            b = a * b_sh + b
            a = a * a_sh
        out = b + a * hprev
        u_ref[pl.ds(r0, 8), :] = out
        return out[7:8, :]

    h_ref[...] = lax.fori_loop(0, n // 8, group, h_ref[...], unroll=4)
    y_ref[0] = (u_ref[...] * gate_ref[0]).astype(y_ref.dtype)


def rglru_core(xr, gate, conv_w, conv_b, w_a, b_a, w_i, b_i, lam):
    b, s, width = xr.shape
    row = lambda bi, t: (0, 0)
    return pl.pallas_call(
        _rglru_kernel,
        out_shape=jax.ShapeDtypeStruct((b, s, width), BF16),
        grid=(b, s // REC_TILE),
        in_specs=[
            pl.BlockSpec((1, REC_TILE, width), lambda bi, t: (bi, t, 0)),
            pl.BlockSpec((1, REC_TILE, width), lambda bi, t: (bi, t, 0)),
            pl.BlockSpec(conv_w.shape, row),
            pl.BlockSpec((1, width), row),
            pl.BlockSpec(w_a.shape, lambda bi, t: (0, 0, 0)),
            pl.BlockSpec((1, width), row),
            pl.BlockSpec(w_i.shape, lambda bi, t: (0, 0, 0)),
            pl.BlockSpec((1, width), row),
            pl.BlockSpec((1, width), row),
        ],
        out_specs=pl.BlockSpec((1, REC_TILE, width), lambda bi, t: (bi, t, 0)),
        scratch_shapes=[
            pltpu.VMEM((REC_TILE + 8, width), F32),
            pltpu.VMEM((REC_TILE, width), F32),
            pltpu.VMEM((REC_TILE, width), F32),
            pltpu.VMEM((1, width), F32),
        ],
        compiler_params=_params(("arbitrary", "arbitrary")),
        name="rglru_core",
    )(xr, gate, conv_w, conv_b, w_a, b_a, w_i, b_i, lam)


def kernel(x, attn_w_in, attn_rel_bias, attn_w_out, rg_w_in, rg_conv_w, rg_conv_b, rg_w_a, rg_b_a,
           rg_w_i, rg_b_i, rg_lambda, rg_w_out, norm_mix_pre, norm_mix_post, norm_ffn_pre,
           norm_ffn_post, ffn_w_gate, ffn_w_up, ffn_w_down):
    b, s, d = x.shape
    depth = norm_mix_pre.shape[0]
    xf = x.reshape(b * s, d)
    row = lambda v: v.reshape(1, -1)
    for layer in range(depth):
        j = layer // 2
        g_pre = row(norm_mix_pre[layer])
        g_post = row(norm_mix_post[layer])
        if layer % 2 == 0:
            proj = norm_linear(xf, g_pre, attn_w_in[j].astype(BF16), BF16).reshape(b, s, -1)
            out_a = band_attention(proj, band_bias(attn_rel_bias[j]), 0)
            out_b = stick_breaking_attention(proj, 3 * A_W)
            w_out = attn_w_out[j].astype(BF16)
            ins = [out_a.reshape(b * s, A_W), out_b.reshape(b * s, B_W)]
            ws = [w_out[:A_W], w_out[A_W:]]
        else:
            gate, xr = norm_linear_gelu(xf, g_pre, rg_w_in[j].astype(BF16))
            width = xr.shape[1]
            y = rglru_core(
                xr.reshape(b, s, width), gate.reshape(b, s, width),
                rg_conv_w[j].reshape(-1, width), row(rg_conv_b[j]),
                rg_w_a[j].astype(BF16), row(rg_b_a[j]),
                rg_w_i[j].astype(BF16), row(rg_b_i[j]), row(rg_lambda[j]))
            ins = [y.reshape(b * s, width)]
            ws = [rg_w_out[j].astype(BF16)]
        xf = mixer_out_ffn(
            xf, g_post, ins, ws, row(norm_ffn_pre[layer]), row(norm_ffn_post[layer]),
            ffn_w_gate[layer].astype(BF16), ffn_w_up[layer].astype(BF16),
            ffn_w_down[layer].astype(BF16))
    return xf.reshape(b, s, d)
```

```python
import functools

import jax
import jax.numpy as jnp
from jax import lax
from jax.experimental import pallas as pl
from jax.experimental.pallas import tpu as pltpu

F32 = jnp.float32
BF16 = jnp.bfloat16

D_MODEL = 1024
HEAD_DIM = 64
CHUNK = 64
N_LEFT_CHUNKS = 8
REL_CLIP = 256
A_W = 512
B_W = 512
LRU_BLOCKS = 4
LRU_BLOCK_W = 256
LRU_C = 8.0
RMS_EPS = 1e-6

LANES = 128
VMEM_LIMIT = 56 * 1024 * 1024

ROW_TILE = 512
SB_TILE = 256
SB_SUBTILES = 4
A_TILE = 2 * CHUNK
A_SUBTILES = 4
A_WIN = (N_LEFT_CHUNKS + 2) * CHUNK
A_PAD = N_LEFT_CHUNKS * CHUNK
REC_TILE = 512
NEG_BIG = -1e30
LOG2E = 1.4426950408889634

_NT = (((1,), (1,)), ((), ()))


def _params(sem):
    return pltpu.CompilerParams(dimension_semantics=sem, vmem_limit_bytes=VMEM_LIMIT)


def _rms(x, g):
    ms = jnp.mean(x * x, axis=-1, keepdims=True)
    return x * lax.rsqrt(ms + RMS_EPS) * g


def _sigmoid(x):
    return 1.0 / (1.0 + jnp.exp(-x))


def _norm_linear_kernel(x_ref, g_ref, w_ref, o_ref):
    h = _rms(x_ref[...], g_ref[...]).astype(BF16)
    o_ref[...] = jnp.dot(h, w_ref[...], preferred_element_type=F32).astype(o_ref.dtype)


def norm_linear(x, g, w, out_dtype):
    m, d = x.shape
    n = w.shape[1]
    return pl.pallas_call(
        _norm_linear_kernel,
        out_shape=jax.ShapeDtypeStruct((m, n), out_dtype),
        grid=(m // ROW_TILE,),
        in_specs=[
            pl.BlockSpec((ROW_TILE, d), lambda i: (i, 0)),
            pl.BlockSpec((1, d), lambda i: (0, 0)),
            pl.BlockSpec((d, n), lambda i: (0, 0)),
        ],
        out_specs=pl.BlockSpec((ROW_TILE, n), lambda i: (i, 0)),
        compiler_params=_params(("arbitrary",)),
        name="norm_linear",
    )(x, g, w)


def _norm_linear_gelu_kernel(x_ref, g_ref, w_ref, gate_ref, xr_ref):
    h = _rms(x_ref[...], g_ref[...]).astype(BF16)
    n = gate_ref.shape[1]
    gate = jnp.dot(h, w_ref[:, :n], preferred_element_type=F32)
    c = 0.7978845608028654
    gate_ref[...] = 0.5 * gate * (1.0 + jnp.tanh(c * (gate + 0.044715 * (gate * gate * gate))))
    xr_ref[...] = jnp.dot(h, w_ref[:, n:], preferred_element_type=F32)


def norm_linear_gelu(x, g, w):
    m, d = x.shape
    n = w.shape[1] // 2
    return pl.pallas_call(
        _norm_linear_gelu_kernel,
        out_shape=(jax.ShapeDtypeStruct((m, n), F32), jax.ShapeDtypeStruct((m, n), F32)),
        grid=(m // ROW_TILE,),
        in_specs=[
            pl.BlockSpec((ROW_TILE, d), lambda i: (i, 0)),
            pl.BlockSpec((1, d), lambda i: (0, 0)),
            pl.BlockSpec((d, 2 * n), lambda i: (0, 0)),
        ],
        out_specs=(pl.BlockSpec((ROW_TILE, n), lambda i: (i, 0)),
                   pl.BlockSpec((ROW_TILE, n), lambda i: (i, 0))),
        compiler_params=_params(("arbitrary",)),
        name="norm_linear_gelu",
    )(x, g, w)


def _out_ffn_kernel(n_in, *refs):
    x_ref, gmix_ref, gpre_ref, gpost_ref, wg_ref, wu_ref, wd_ref = refs[:7]
    ins = refs[7:7 + n_in]
    ws = refs[7 + n_in:7 + 2 * n_in]
    o_ref = refs[7 + 2 * n_in]
    m = jnp.dot(ins[0][...], ws[0][...], preferred_element_type=F32)
    for a, w in zip(ins[1:], ws[1:]):
        m = m + jnp.dot(a[...], w[...], preferred_element_type=F32)
    x = x_ref[...] + _rms(m, gmix_ref[...])
    h = _rms(x, gpre_ref[...]).astype(BF16)
    gate = jnp.dot(h, wg_ref[...], preferred_element_type=F32)
    up = jnp.dot(h, wu_ref[...], preferred_element_type=F32)
    act = (gate * _sigmoid(gate) * up).astype(BF16)
    f = jnp.dot(act, wd_ref[...], preferred_element_type=F32)
    o_ref[...] = x + _rms(f, gpost_ref[...])


def _resident(shape):
    return pl.BlockSpec(shape, lambda i: (0,) * len(shape), pipeline_mode=pl.Buffered(1))


def mixer_out_ffn(x, gmix, ins, ws, gpre, gpost, wg, wu, wd):
    m, d = x.shape
    n_in = len(ins)
    in_specs = [pl.BlockSpec((ROW_TILE, d), lambda i: (i, 0))]
    in_specs += [_resident(a.shape) for a in (gmix, gpre, gpost, wg, wu, wd)]
    in_specs += [pl.BlockSpec((ROW_TILE, a.shape[1]), lambda i: (i, 0)) for a in ins]
    in_specs += [_resident(w.shape) for w in ws]
    return pl.pallas_call(
        functools.partial(_out_ffn_kernel, n_in),
        out_shape=jax.ShapeDtypeStruct((m, d), F32),
        grid=(m // ROW_TILE,),
        in_specs=in_specs,
        out_specs=pl.BlockSpec((ROW_TILE, d), lambda i: (i, 0)),
        compiler_params=_params(("arbitrary",)),
        name="mixer_out_ffn",
    )(x, gmix, gpre, gpost, wg, wu, wd, *ins, *ws)


def _band_attn_kernel(q_ref, k_ref, v_ref, bias_ref, o_ref, kp_ref, vp_ref):
    t = pl.program_id(2)

    @pl.when(t == 0)
    def _():
        kp_ref[0:A_PAD, :] = jnp.zeros((A_PAD, LANES), BF16)
        vp_ref[0:A_PAD, :] = jnp.zeros((A_PAD, LANES), BF16)
        kp_ref[A_PAD:, :] = k_ref[0]
        vp_ref[A_PAD:, :] = v_ref[0]

    lane = lax.broadcasted_iota(jnp.int32, (1, LANES), 1)
    first = lane < HEAD_DIM
    col = lax.broadcasted_iota(jnp.int32, (1, A_WIN), 1)
    bias = jnp.concatenate([bias_ref[0], bias_ref[1]], axis=0)
    zero = jnp.zeros((A_TILE, LANES), BF16)
    for u in range(A_SUBTILES):
        tile = t * A_SUBTILES + u
        start = pl.multiple_of(tile * A_TILE, A_TILE)
        kw = kp_ref[pl.ds(start, A_WIN), :]
        vw = vp_ref[pl.ds(start, A_WIN), :]
        in_seq = col >= (A_PAD - tile * A_TILE)
        q2 = q_ref[0, u * A_TILE:(u + 1) * A_TILE, :] * (HEAD_DIM ** -0.5)
        qs = jnp.concatenate([jnp.where(first, q2, zero), jnp.where(first, zero, q2)], axis=0)
        s = lax.dot_general(qs, kw, _NT, preferred_element_type=F32) + bias
        s = jnp.where(in_seq, s, NEG_BIG)
        mx = jnp.max(s, axis=-1, keepdims=True)
        p = jnp.exp(s - mx)
        l = jnp.sum(p, axis=-1, keepdims=True)
        o = jnp.dot(p.astype(BF16), vw, preferred_element_type=F32) * (1.0 / l)
        out = jnp.where(first, o[:A_TILE], o[A_TILE:])
        o_ref[0, u * A_TILE:(u + 1) * A_TILE, :] = out.astype(o_ref.dtype)


def band_attention(proj, bias, col0):
    b, s, _ = proj.shape
    nq = col0 // LANES
    nk = (col0 + A_W) // LANES
    nv = (col0 + 2 * A_W) // LANES
    tq = A_SUBTILES * A_TILE
    return pl.pallas_call(
        _band_attn_kernel,
        out_shape=jax.ShapeDtypeStruct((b, s, A_W), BF16),
        grid=(b, A_W // LANES, s // tq),
        in_specs=[
            pl.BlockSpec((1, tq, LANES), lambda bi, p, t: (bi, t, nq + p)),
            pl.BlockSpec((1, s, LANES), lambda bi, p, t: (bi, 0, nk + p)),
            pl.BlockSpec((1, s, LANES), lambda bi, p, t: (bi, 0, nv + p)),
            pl.BlockSpec((2, A_TILE, A_WIN), lambda bi, p, t: (p, 0, 0)),
        ],
        out_specs=pl.BlockSpec((1, tq, LANES), lambda bi, p, t: (bi, t, p)),
        scratch_shapes=[pltpu.VMEM((s + A_PAD, LANES), BF16), pltpu.VMEM((s + A_PAD, LANES), BF16)],
        compiler_params=_params(("arbitrary", "arbitrary", "arbitrary")),
        name="band_attention",
    )(proj, proj, proj, bias)


def band_bias(rel_bias):
    h = rel_bias.shape[0]
    n_f = A_WIN + A_TILE - 1
    n_const = A_PAD + A_TILE - 1 - REL_CLIP
    far = jnp.broadcast_to(rel_bias[:, 2 * REL_CLIP:], (h, n_const))
    near = rel_bias[:, 2 * REL_CLIP + 1 - (n_f - n_const):][:, ::-1]
    f = jnp.concatenate([far, near], axis=1)
    period = n_f + 1
    g = jnp.concatenate([f[:, A_TILE - 1:], jnp.zeros((h, 1), F32), f[:, :A_TILE - 1]], axis=1)
    flat = jnp.tile(g, (1, A_TILE))[:, :A_TILE * (period - 1)]
    toep = flat.reshape(h, A_TILE, period - 1)[:, :, :A_WIN]
    r = jnp.arange(A_TILE)[:, None]
    c = jnp.arange(A_WIN)[None, :]
    dchunk = c // CHUNK - r // CHUNK
    ok = (dchunk >= 0) & (dchunk <= N_LEFT_CHUNKS)
    return jnp.where(ok[None], toep, NEG_BIG).astype(F32)


def _stick_kernel(q_ref, k_ref, v_ref, o_ref, acc_ref, carry_ref, z_ref, sp_ref):
    qi = pl.program_id(2)
    t = SB_TILE
    ns = SB_SUBTILES
    lane = lax.broadcasted_iota(jnp.int32, (1, LANES), 1)
    first = lane < HEAD_DIM
    q2 = q_ref[0] * (HEAD_DIM ** -0.5)
    row = lax.broadcasted_iota(jnp.int32, (t, t), 0)
    col = lax.broadcasted_iota(jnp.int32, (t, t), 1)
    causal = col < row
    true = jnp.ones((t, t), jnp.bool_)
    suffix = jnp.where(row >= col, 1.0, 0.0).astype(BF16)
    zero = jnp.zeros((t, LANES), BF16)
    acc_ref[...] = jnp.zeros_like(acc_ref)
    carry_ref[...] = jnp.zeros_like(carry_ref)

    def split_heads(x):
        return jnp.concatenate([jnp.where(first, x, zero), jnp.where(first, zero, x)], axis=0)

    def scores(j, d, mask):
        start = pl.multiple_of(j * t, t)
        kcat = split_heads(k_ref[0, pl.ds(start, t), :])
        z = lax.dot_general(q2[d * t:], kcat, _NT, preferred_element_type=F32)
        sp = jnp.maximum(z, 0.0) + jnp.log(1.0 + jnp.exp2(jnp.abs(z) * (-LOG2E)))
        if mask is not None:
            sp = jnp.where(mask, sp, 0.0)
        return z, sp.astype(BF16)

    def weights(j, d, mask, z, spb):
        r0 = d * t
        n = (ns - d) * t
        start = pl.multiple_of(j * t, t)
        vcat = split_heads(v_ref[0, pl.ds(start, t), :])
        c = [jnp.dot(spb[:, h * t:(h + 1) * t], suffix, preferred_element_type=F32) for h in range(2)]
        carry = [carry_ref[h, r0:, :] for h in range(2)]
        reps = t // LANES
        shift = jnp.concatenate([c[0]] + [c[1]], axis=1) + jnp.concatenate(
            [carry[0]] * reps + [carry[1]] * reps, axis=1)
        w = jnp.exp(z - shift)
        if mask is not None:
            w = jnp.where(mask, w, 0.0)
        for h in range(2):
            carry_ref[h, r0:, :] = carry[h] + jnp.broadcast_to(c[h][:, 0:1], (n, LANES))
        acc_ref[r0:, :] += jnp.dot(w.astype(BF16), vcat, preferred_element_type=F32)

    for d in range(ns - 1, -1, -1):
        mask = jnp.concatenate([causal] + [true] * (ns - d - 1), axis=0)
        mask = jnp.concatenate([mask, mask], axis=1)
        z, spb = scores(ns * qi + d, d, mask)
        weights(ns * qi + d, d, mask, z, spb)

    def stage1(i, slot):
        z, spb = scores(ns * qi - 1 - i, 0, None)
        z_ref[slot] = z
        sp_ref[slot] = spb

    def stage2(i, slot):
        weights(ns * qi - 1 - i, 0, None, z_ref[slot], sp_ref[slot])

    @pl.when(qi > 0)
    def _():
        stage1(0, 0)

        def body(k, _):
            stage1(2 * k + 1, 1)
            stage2(2 * k, 0)
            stage1(2 * k + 2, 0)
            stage2(2 * k + 1, 1)
            return 0

        n_pairs = (ns // 2) * qi
        lax.fori_loop(0, n_pairs - 1, body, 0)
        last = 2 * (n_pairs - 1)
        stage1(last + 1, 1)
        stage2(last, 0)
        stage2(last + 1, 1)

    o_ref[0] = acc_ref[...].astype(o_ref.dtype)


def stick_breaking_attention(proj, col0):
    b, s, _ = proj.shape
    nq = col0 // LANES
    nk = (col0 + B_W) // LANES
    nv = (col0 + 2 * B_W) // LANES
    tq = SB_SUBTILES * SB_TILE
    return pl.pallas_call(
        _stick_kernel,
        out_shape=jax.ShapeDtypeStruct((b, s, B_W), BF16),
        grid=(b, B_W // LANES, s // tq),
        in_specs=[
            pl.BlockSpec((1, tq, LANES), lambda bi, p, t: (bi, t, nq + p)),
            pl.BlockSpec((1, s, LANES), lambda bi, p, t: (bi, 0, nk + p)),
            pl.BlockSpec((1, s, LANES), lambda bi, p, t: (bi, 0, nv + p)),
        ],
        out_specs=pl.BlockSpec((1, tq, LANES), lambda bi, p, t: (bi, t, p)),
        scratch_shapes=[
            pltpu.VMEM((tq, LANES), F32),
            pltpu.VMEM((2, tq, LANES), F32),
            pltpu.VMEM((2, tq, 2 * SB_TILE), F32),
            pltpu.VMEM((2, tq, 2 * SB_TILE), BF16),
        ],
        compiler_params=_params(("arbitrary", "arbitrary", "arbitrary")),
        name="stick_breaking",
    )(proj, proj, proj)


def _rglru_kernel(xr_ref, gate_ref, cw_ref, cb_ref, wa_ref, ba_ref, wi_ref, bi_ref, lam_ref,
                  y_ref, xpad_ref, a_ref, u_ref, h_ref):
    t = pl.program_id(1)
    n = REC_TILE
    width = xr_ref.shape[2]

    @pl.when(t == 0)
    def _():
        xpad_ref[0:8, :] = jnp.zeros((8, width), F32)
        h_ref[...] = jnp.zeros_like(h_ref)

    xpad_ref[8:8 + n, :] = xr_ref[0]
    cw = cw_ref[...]
    xc = (cb_ref[...]
          + cw[3:4] * xpad_ref[8:8 + n, :]
          + cw[2:3] * xpad_ref[7:7 + n, :]
          + cw[1:2] * xpad_ref[6:6 + n, :]
          + cw[0:1] * xpad_ref[5:5 + n, :])
    xpad_ref[0:8, :] = xpad_ref[n:n + 8, :]
    xcb = xc.astype(BF16)
    lam = lam_ref[...]
    sp_lam = jnp.maximum(-lam, 0.0) + jnp.log1p(jnp.exp(-jnp.abs(lam)))
    for blk in range(LRU_BLOCKS):
        sl = slice(blk * LRU_BLOCK_W, (blk + 1) * LRU_BLOCK_W)
        xb = xcb[:, sl]
        r = _sigmoid(jnp.dot(xb, wa_ref[blk], preferred_element_type=F32) + ba_ref[:, sl])
        i = _sigmoid(jnp.dot(xb, wi_ref[blk], preferred_element_type=F32) + bi_ref[:, sl])
        log_a = (-LRU_C * r) * sp_lam[:, sl]
        a = jnp.exp(log_a)
        mult = jnp.sqrt(-jnp.tanh(log_a) * (a * a + 1.0))
        a_ref[:, sl] = a
        u_ref[:, sl] = mult * (i * xc[:, sl])

    rowid = lax.broadcasted_iota(jnp.int32, (8, width), 0)

    def group(g, hprev):
        r0 = pl.multiple_of(g * 8, 8)
        a = a_ref[pl.ds(r0, 8), :]
        b = u_ref[pl.ds(r0, 8), :]
        for d in (1, 2, 4):
            keep = rowid >= d
            a_sh = jnp.where(keep, pltpu.roll(a, d, 0), 1.0)
            b_sh = jnp.where(keep, pltpu.roll(b, d, 0), 0.0)
            b = a * b_sh + b
            a = a * a_sh
        out = b + a * hprev
        u_ref[pl.ds(r0, 8), :] = out
        return out[7:8, :]

    h_ref[...] = lax.fori_loop(0, n // 8, group, h_ref[...], unroll=4)
    y_ref[0] = (u_ref[...] * gate_ref[0]).astype(y_ref.dtype)


def rglru_core(xr, gate, conv_w, conv_b, w_a, b_a, w_i, b_i, lam):
    b, s, width = xr.shape
    row = lambda bi, t: (0, 0)
    return pl.pallas_call(
        _rglru_kernel,
        out_shape=jax.ShapeDtypeStruct((b, s, width), BF16),
        grid=(b, s // REC_TILE),
        in_specs=[
            pl.BlockSpec((1, REC_TILE, width), lambda bi, t: (bi, t, 0)),
            pl.BlockSpec((1, REC_TILE, width), lambda bi, t: (bi, t, 0)),
            pl.BlockSpec(conv_w.shape, row),
            pl.BlockSpec((1, width), row),
            pl.BlockSpec(w_a.shape, lambda bi, t: (0, 0, 0)),
            pl.BlockSpec((1, width), row),
            pl.BlockSpec(w_i.shape, lambda bi, t: (0, 0, 0)),
            pl.BlockSpec((1, width), row),
            pl.BlockSpec((1, width), row),
        ],
        out_specs=pl.BlockSpec((1, REC_TILE, width), lambda bi, t: (bi, t, 0)),
        scratch_shapes=[
            pltpu.VMEM((REC_TILE + 8, width), F32),
            pltpu.VMEM((REC_TILE, width), F32),
            pltpu.VMEM((REC_TILE, width), F32),
            pltpu.VMEM((1, width), F32),
        ],
        compiler_params=_params(("arbitrary", "arbitrary")),
        name="rglru_core",
    )(xr, gate, conv_w, conv_b, w_a, b_a, w_i, b_i, lam)


def kernel(x, attn_w_in, attn_rel_bias, attn_w_out, rg_w_in, rg_conv_w, rg_conv_b, rg_w_a, rg_b_a,
           rg_w_i, rg_b_i, rg_lambda, rg_w_out, norm_mix_pre, norm_mix_post, norm_ffn_pre,
           norm_ffn_post, ffn_w_gate, ffn_w_up, ffn_w_down):
    b, s, d = x.shape
    depth = norm_mix_pre.shape[0]
    xf = x.reshape(b * s, d)
    row = lambda v: v.reshape(1, -1)
    for layer in range(depth):
        j = layer // 2
        g_pre = row(norm_mix_pre[layer])
        g_post = row(norm_mix_post[layer])
        if layer % 2 == 0:
            proj = norm_linear(xf, g_pre, attn_w_in[j].astype(BF16), BF16).reshape(b, s, -1)
            out_a = band_attention(proj, band_bias(attn_rel_bias[j]), 0)
            out_b = stick_breaking_attention(proj, 3 * A_W)
            w_out = attn_w_out[j].astype(BF16)
            ins = [out_a.reshape(b * s, A_W), out_b.reshape(b * s, B_W)]
            ws = [w_out[:A_W], w_out[A_W:]]
        else:
            gate, xr = norm_linear_gelu(xf, g_pre, rg_w_in[j].astype(BF16))
            width = xr.shape[1]
            y = rglru_core(
                xr.reshape(b, s, width), gate.reshape(b, s, width),
                rg_conv_w[j].reshape(-1, width), row(rg_conv_b[j]),
                rg_w_a[j].astype(BF16), row(rg_b_a[j]),
                rg_w_i[j].astype(BF16), row(rg_b_i[j]), row(rg_lambda[j]))
            ins = [y.reshape(b * s, width)]
            ws = [rg_w_out[j].astype(BF16)]
        xf = mixer_out_ffn(
            xf, g_post, ins, ws, row(norm_ffn_pre[layer]), row(norm_ffn_post[layer]),
            ffn_w_gate[layer].astype(BF16), ffn_w_up[layer].astype(BF16),
            ffn_w_down[layer].astype(BF16))
    return xf.reshape(b, s, d)
```

```python
import functools

import jax
import jax.numpy as jnp
from jax import lax
from jax.experimental import pallas as pl
from jax.experimental.pallas import tpu as pltpu

F32 = jnp.float32
BF16 = jnp.bfloat16

D_MODEL = 1024
HEAD_DIM = 64
CHUNK = 64
N_LEFT_CHUNKS = 8
REL_CLIP = 256
A_W = 512
B_W = 512
LRU_BLOCKS = 4
LRU_BLOCK_W = 256
LRU_C = 8.0
RMS_EPS = 1e-6

LANES = 128
VMEM_LIMIT = 56 * 1024 * 1024

ROW_TILE = 512
SB_TILE = 256
SB_SUBTILES = 4
SB_UNROLL = 4
A_TILE = 2 * CHUNK
A_SUBTILES = 16
A_WIN = (N_LEFT_CHUNKS + 2) * CHUNK
A_PAD = N_LEFT_CHUNKS * CHUNK
REC_TILE = 512
NEG_BIG = -1e30
LOG2E = 1.4426950408889634

_NT = (((1,), (1,)), ((), ()))


def _params(sem):
    return pltpu.CompilerParams(dimension_semantics=sem, vmem_limit_bytes=VMEM_LIMIT)


def _rms(x, g):
    ms = jnp.mean(x * x, axis=-1, keepdims=True)
    return x * lax.rsqrt(ms + RMS_EPS) * g


def _sigmoid(x):
    return 1.0 / (1.0 + jnp.exp(-x))


def _norm_linear_kernel(x_ref, g_ref, w_ref, o_ref):
    h = _rms(x_ref[...], g_ref[...]).astype(BF16)
    o_ref[...] = jnp.dot(h, w_ref[...], preferred_element_type=F32).astype(o_ref.dtype)


def norm_linear(x, g, w, out_dtype):
    m, d = x.shape
    n = w.shape[1]
    return pl.pallas_call(
        _norm_linear_kernel,
        out_shape=jax.ShapeDtypeStruct((m, n), out_dtype),
        grid=(m // ROW_TILE,),
        in_specs=[
            pl.BlockSpec((ROW_TILE, d), lambda i: (i, 0)),
            pl.BlockSpec((1, d), lambda i: (0, 0)),
            pl.BlockSpec((d, n), lambda i: (0, 0)),
        ],
        out_specs=pl.BlockSpec((ROW_TILE, n), lambda i: (i, 0)),
        compiler_params=_params(("arbitrary",)),
        name="norm_linear",
    )(x, g, w)


def _out_ffn_kernel(n_in, *refs):
    x_ref, gmix_ref, gpre_ref, gpost_ref, wg_ref, wu_ref, wd_ref = refs[:7]
    ins = refs[7:7 + n_in]
    ws = refs[7 + n_in:7 + 2 * n_in]
    o_ref = refs[7 + 2 * n_in]
    m = jnp.dot(ins[0][...], ws[0][...], preferred_element_type=F32)
    for a, w in zip(ins[1:], ws[1:]):
        m = m + jnp.dot(a[...], w[...], preferred_element_type=F32)
    x = x_ref[...] + _rms(m, gmix_ref[...])
    h = _rms(x, gpre_ref[...]).astype(BF16)
    gate = jnp.dot(h, wg_ref[...], preferred_element_type=F32)
    up = jnp.dot(h, wu_ref[...], preferred_element_type=F32)
    act = (gate * _sigmoid(gate) * up).astype(BF16)
    f = jnp.dot(act, wd_ref[...], preferred_element_type=F32)
    o_ref[...] = x + _rms(f, gpost_ref[...])


def _resident(shape):
    return pl.BlockSpec(shape, lambda i: (0,) * len(shape), pipeline_mode=pl.Buffered(1))


def mixer_out_ffn(x, gmix, ins, ws, gpre, gpost, wg, wu, wd):
    m, d = x.shape
    n_in = len(ins)
    in_specs = [pl.BlockSpec((ROW_TILE, d), lambda i: (i, 0))]
    in_specs += [_resident(a.shape) for a in (gmix, gpre, gpost, wg, wu, wd)]
    in_specs += [pl.BlockSpec((ROW_TILE, a.shape[1]), lambda i: (i, 0)) for a in ins]
    in_specs += [_resident(w.shape) for w in ws]
    return pl.pallas_call(
        functools.partial(_out_ffn_kernel, n_in),
        out_shape=jax.ShapeDtypeStruct((m, d), F32),
        grid=(m // ROW_TILE,),
        in_specs=in_specs,
        out_specs=pl.BlockSpec((ROW_TILE, d), lambda i: (i, 0)),
        compiler_params=_params(("arbitrary",)),
        name="mixer_out_ffn",
    )(x, gmix, gpre, gpost, wg, wu, wd, *ins, *ws)


def _band_attn_kernel(q_ref, k_ref, v_ref, bias_ref, o_ref, kp_ref, vp_ref, s_ref, p_ref, rinv_ref):
    t = pl.program_id(2)

    @pl.when(t == 0)
    def _():
        kp_ref[0:A_PAD, :] = jnp.zeros((A_PAD, LANES), BF16)
        vp_ref[0:A_PAD, :] = jnp.zeros((A_PAD, LANES), BF16)
        kp_ref[A_PAD:, :] = k_ref[0]
        vp_ref[A_PAD:, :] = v_ref[0]

    lane = lax.broadcasted_iota(jnp.int32, (1, LANES), 1)
    first = lane < HEAD_DIM
    col = lax.broadcasted_iota(jnp.int32, (1, A_WIN), 1)
    bias = jnp.concatenate([bias_ref[0], bias_ref[1]], axis=0)
    zero = jnp.zeros((A_TILE, LANES), BF16)

    def window(u):
        tile = t * A_SUBTILES + u
        return tile, pl.multiple_of(tile * A_TILE, A_TILE)

    def scores(u, slot):
        tile, start = window(u)
        kw = kp_ref[pl.ds(start, A_WIN), :]
        q2 = q_ref[0, pl.ds(pl.multiple_of(u * A_TILE, A_TILE), A_TILE), :] * (HEAD_DIM ** -0.5)
        qs = jnp.concatenate([jnp.where(first, q2, zero), jnp.where(first, zero, q2)], axis=0)
        s = lax.dot_general(qs, kw, _NT, preferred_element_type=F32) + bias
        in_seq = col >= (A_PAD - tile * A_TILE)
        s_ref[slot] = jnp.where(in_seq, s, NEG_BIG)

    def softmax(slot):
        s = s_ref[slot]
        p = jnp.exp(s - jnp.max(s, axis=-1, keepdims=True))
        p_ref[slot] = p.astype(BF16)
        rinv_ref[slot] = jnp.broadcast_to(1.0 / jnp.sum(p, axis=-1, keepdims=True), (2 * A_TILE, LANES))

    def values(u, slot):
        _, start = window(u)
        vw = vp_ref[pl.ds(start, A_WIN), :]
        o = jnp.dot(p_ref[slot], vw, preferred_element_type=F32) * rinv_ref[slot]
        out = jnp.where(first, o[:A_TILE], o[A_TILE:])
        o_ref[0, pl.ds(pl.multiple_of(u * A_TILE, A_TILE), A_TILE), :] = out.astype(o_ref.dtype)

    scores(0, 0)
    scores(1, 1)
    softmax(0)

    def body(k, _):
        values(2 * k - 2, 0)
        softmax(1)
        scores(2 * k, 0)
        values(2 * k - 1, 1)
        softmax(0)
        scores(2 * k + 1, 1)
        return 0

    lax.fori_loop(1, A_SUBTILES // 2, body, 0)
    values(A_SUBTILES - 2, 0)
    softmax(1)
    values(A_SUBTILES - 1, 1)


def band_attention(proj, bias, col0):
    b, s, _ = proj.shape
    nq = col0 // LANES
    nk = (col0 + A_W) // LANES
    nv = (col0 + 2 * A_W) // LANES
    tq = A_SUBTILES * A_TILE
    return pl.pallas_call(
        _band_attn_kernel,
        out_shape=jax.ShapeDtypeStruct((b, s, A_W), BF16),
        grid=(b, A_W // LANES, s // tq),
        in_specs=[
            pl.BlockSpec((1, tq, LANES), lambda bi, p, t: (bi, t, nq + p)),
            pl.BlockSpec((1, s, LANES), lambda bi, p, t: (bi, 0, nk + p)),
            pl.BlockSpec((1, s, LANES), lambda bi, p, t: (bi, 0, nv + p)),
            pl.BlockSpec((2, A_TILE, A_WIN), lambda bi, p, t: (p, 0, 0)),
        ],
        out_specs=pl.BlockSpec((1, tq, LANES), lambda bi, p, t: (bi, t, p)),
        scratch_shapes=[
            pltpu.VMEM((s + A_PAD, LANES), BF16),
            pltpu.VMEM((s + A_PAD, LANES), BF16),
            pltpu.VMEM((2, 2 * A_TILE, A_WIN), F32),
            pltpu.VMEM((2, 2 * A_TILE, A_WIN), BF16),
            pltpu.VMEM((2, 2 * A_TILE, LANES), F32),
        ],
        compiler_params=_params(("arbitrary", "arbitrary", "arbitrary")),
        name="band_attention",
    )(proj, proj, proj, bias)


def band_bias(rel_bias):
    h = rel_bias.shape[0]
    n_f = A_WIN + A_TILE - 1
    n_const = A_PAD + A_TILE - 1 - REL_CLIP
    far = jnp.broadcast_to(rel_bias[:, 2 * REL_CLIP:], (h, n_const))
    near = rel_bias[:, 2 * REL_CLIP + 1 - (n_f - n_const):][:, ::-1]
    f = jnp.concatenate([far, near], axis=1)
    period = n_f + 1
    g = jnp.concatenate([f[:, A_TILE - 1:], jnp.zeros((h, 1), F32), f[:, :A_TILE - 1]], axis=1)
    flat = jnp.tile(g, (1, A_TILE))[:, :A_TILE * (period - 1)]
    toep = flat.reshape(h, A_TILE, period - 1)[:, :, :A_WIN]
    r = jnp.arange(A_TILE)[:, None]
    c = jnp.arange(A_WIN)[None, :]
    dchunk = c // CHUNK - r // CHUNK
    ok = (dchunk >= 0) & (dchunk <= N_LEFT_CHUNKS)
    return jnp.where(ok[None], toep, NEG_BIG).astype(F32)


def _stick_kernel(q_ref, k_ref, v_ref, o_ref, acc_ref, carry_ref, z_ref, sp_ref):
    qi = pl.program_id(2)
    t = SB_TILE
    ns = SB_SUBTILES
    lane = lax.broadcasted_iota(jnp.int32, (1, LANES), 1)
    first = lane < HEAD_DIM
    q2 = q_ref[0] * (HEAD_DIM ** -0.5)
    row = lax.broadcasted_iota(jnp.int32, (t, t), 0)
    col = lax.broadcasted_iota(jnp.int32, (t, t), 1)
    causal = col < row
    true = jnp.ones((t, t), jnp.bool_)
    suffix = jnp.where(row >= col, 1.0, 0.0).astype(BF16)
    zero = jnp.zeros((t, LANES), BF16)
    acc_ref[...] = jnp.zeros_like(acc_ref)
    carry_ref[...] = jnp.zeros_like(carry_ref)

    def split_heads(x):
        return jnp.concatenate([jnp.where(first, x, zero), jnp.where(first, zero, x)], axis=0)

    def scores(j, d, mask):
        start = pl.multiple_of(j * t, t)
        kcat = split_heads(k_ref[0, pl.ds(start, t), :])
        z = lax.dot_general(q2[d * t:], kcat, _NT, preferred_element_type=F32)
        sp = jnp.maximum(z, 0.0) + jnp.log(1.0 + jnp.exp2(jnp.abs(z) * (-LOG2E)))
        if mask is not None:
            sp = jnp.where(mask, sp, 0.0)
        return z, sp.astype(BF16)

    def weights(j, d, mask, z, spb):
        r0 = d * t
        n = (ns - d) * t
        start = pl.multiple_of(j * t, t)
        vcat = split_heads(v_ref[0, pl.ds(start, t), :])
        c = [jnp.dot(spb[:, h * t:(h + 1) * t], suffix, preferred_element_type=F32) for h in range(2)]
        carry = [carry_ref[h, r0:, :] for h in range(2)]
        reps = t // LANES
        shift = jnp.concatenate([c[0]] + [c[1]], axis=1) + jnp.concatenate(
            [carry[0]] * reps + [carry[1]] * reps, axis=1)
        w = jnp.exp(z - shift)
        if mask is not None:
            w = jnp.where(mask, w, 0.0)
        for h in range(2):
            carry_ref[h, r0:, :] = carry[h] + jnp.broadcast_to(c[h][:, 0:1], (n, LANES))
        acc_ref[r0:, :] += jnp.dot(w.astype(BF16), vcat, preferred_element_type=F32)

    for d in range(ns - 1, -1, -1):
        mask = jnp.concatenate([causal] + [true] * (ns - d - 1), axis=0)
        mask = jnp.concatenate([mask, mask], axis=1)
        z, spb = scores(ns * qi + d, d, mask)
        weights(ns * qi + d, d, mask, z, spb)

    def stage1(i, slot):
        z, spb = scores(ns * qi - 1 - i, 0, None)
        z_ref[slot] = z
        sp_ref[slot] = spb

    def stage2(i, slot):
        weights(ns * qi - 1 - i, 0, None, z_ref[slot], sp_ref[slot])

    @pl.when(qi > 0)
    def _():
        stage1(0, 0)

        def steps(first, count):
            for off in range(count):
                stage1(first + off + 1, (off + 1) % 2)
                stage2(first + off, off % 2)

        def body(k, _):
            steps(SB_UNROLL * k, SB_UNROLL)
            return 0

        n_groups = (ns * qi) // SB_UNROLL
        lax.fori_loop(0, n_groups - 1, body, 0)
        last = SB_UNROLL * (n_groups - 1)
        steps(last, SB_UNROLL - 1)
        stage2(last + SB_UNROLL - 1, (SB_UNROLL - 1) % 2)

    o_ref[0] = acc_ref[...].astype(o_ref.dtype)


def stick_breaking_attention(proj, col0):
    b, s, _ = proj.shape
    nq = col0 // LANES
    nk = (col0 + B_W) // LANES
    nv = (col0 + 2 * B_W) // LANES
    tq = SB_SUBTILES * SB_TILE
    return pl.pallas_call(
        _stick_kernel,
        out_shape=jax.ShapeDtypeStruct((b, s, B_W), BF16),
        grid=(b, B_W // LANES, s // tq),
        in_specs=[
            pl.BlockSpec((1, tq, LANES), lambda bi, p, t: (bi, t, nq + p)),
            pl.BlockSpec((1, s, LANES), lambda bi, p, t: (bi, 0, nk + p)),
            pl.BlockSpec((1, s, LANES), lambda bi, p, t: (bi, 0, nv + p)),
        ],
        out_specs=pl.BlockSpec((1, tq, LANES), lambda bi, p, t: (bi, t, p)),
        scratch_shapes=[
            pltpu.VMEM((tq, LANES), F32),
            pltpu.VMEM((2, tq, LANES), F32),
            pltpu.VMEM((2, tq, 2 * SB_TILE), F32),
            pltpu.VMEM((2, tq, 2 * SB_TILE), BF16),
        ],
        compiler_params=_params(("arbitrary", "arbitrary", "arbitrary")),
        name="stick_breaking",
    )(proj, proj, proj)


def _rglru_kernel(x_ref, g_ref, win_ref, cw_ref, cb_ref, wa_ref, ba_ref, wi_ref, bi_ref, lam_ref,
                  y_ref, xpad_ref, a_ref, u_ref, h_ref):
    t = pl.program_id(1)
    n = REC_TILE
    width = y_ref.shape[2]

    @pl.when(t == 0)
    def _():
        xpad_ref[0:8, :] = jnp.zeros((8, width), F32)
        h_ref[...] = jnp.zeros_like(h_ref)

    hn = _rms(x_ref[0], g_ref[...]).astype(BF16)
    gate = jnp.dot(hn, win_ref[:, :width], preferred_element_type=F32)
    c = 0.7978845608028654
    gate = 0.5 * gate * (1.0 + jnp.tanh(c * (gate + 0.044715 * (gate * gate * gate))))
    xpad_ref[8:8 + n, :] = jnp.dot(hn, win_ref[:, width:], preferred_element_type=F32)
    cw = cw_ref[...]
    xc = (cb_ref[...]
          + cw[3:4] * xpad_ref[8:8 + n, :]
          + cw[2:3] * xpad_ref[7:7 + n, :]
          + cw[1:2] * xpad_ref[6:6 + n, :]
          + cw[0:1] * xpad_ref[5:5 + n, :])
    xpad_ref[0:8, :] = xpad_ref[n:n + 8, :]
    xcb = xc.astype(BF16)
    lam = lam_ref[...]
    sp_lam = jnp.maximum(-lam, 0.0) + jnp.log1p(jnp.exp(-jnp.abs(lam)))
    for blk in range(LRU_BLOCKS):
        sl = slice(blk * LRU_BLOCK_W, (blk + 1) * LRU_BLOCK_W)
        xb = xcb[:, sl]
        pre_r = jnp.dot(xb, wa_ref[blk], preferred_element_type=F32) + ba_ref[:, sl]
        pre_i = jnp.dot(xb, wi_ref[blk], preferred_element_type=F32) + bi_ref[:, sl]
        r = 0.5 * jnp.tanh(0.5 * pre_r) + 0.5
        i = 0.5 * jnp.tanh(0.5 * pre_i) + 0.5
        log_a = (-LRU_C * r) * sp_lam[:, sl]
        a = jnp.exp(log_a)
        m2 = -jnp.tanh(log_a) * (a * a + 1.0)
        mult = m2 * lax.rsqrt(jnp.maximum(m2, 1e-30))
        a_ref[:, sl] = a
        u_ref[:, sl] = mult * (i * xc[:, sl])

    rowid = lax.broadcasted_iota(jnp.int32, (8, width), 0)

    def group(g, hprev):
        r0 = pl.multiple_of(g * 8, 8)
        a = a_ref[pl.ds(r0, 8), :]
        b = u_ref[pl.ds(r0, 8), :]
        for d in (1, 2, 4):
            keep = rowid >= d
            a_sh = jnp.where(keep, pltpu.roll(a, d, 0), 1.0)
            b_sh = jnp.where(keep, pltpu.roll(b, d, 0), 0.0)
            b = a * b_sh + b
            a = a * a_sh
        out = b + a * hprev
        u_ref[pl.ds(r0, 8), :] = out
        return out[7:8, :]

    h_ref[...] = lax.fori_loop(0, n // 8, group, h_ref[...], unroll=4)
    y_ref[0] = (u_ref[...] * gate).astype(y_ref.dtype)


def rglru_mixer(x, g, w_in, conv_w, conv_b, w_a, b_a, w_i, b_i, lam):
    b, s, d = x.shape
    width = w_in.shape[1] // 2
    const2 = lambda bi, t: (0, 0)
    const3 = lambda bi, t: (0, 0, 0)
    return pl.pallas_call(
        _rglru_kernel,
        out_shape=jax.ShapeDtypeStruct((b, s, width), BF16),
        grid=(b, s // REC_TILE),
        in_specs=[
            pl.BlockSpec((1, REC_TILE, d), lambda bi, t: (bi, t, 0)),
            pl.BlockSpec((1, d), const2),
            pl.BlockSpec(w_in.shape, const2),
            pl.BlockSpec(conv_w.shape, const2),
            pl.BlockSpec((1, width), const2),
            pl.BlockSpec(w_a.shape, const3),
            pl.BlockSpec((1, width), const2),
            pl.BlockSpec(w_i.shape, const3),
            pl.BlockSpec((1, width), const2),
            pl.BlockSpec((1, width), const2),
        ],
        out_specs=pl.BlockSpec((1, REC_TILE, width), lambda bi, t: (bi, t, 0)),
        scratch_shapes=[
            pltpu.VMEM((REC_TILE + 8, width), F32),
            pltpu.VMEM((REC_TILE, width), F32),
            pltpu.VMEM((REC_TILE, width), F32),
            pltpu.VMEM((1, width), F32),
        ],
        compiler_params=_params(("arbitrary", "arbitrary")),
        name="rglru_mixer",
    )(x, g, w_in, conv_w, conv_b, w_a, b_a, w_i, b_i, lam)


def kernel(x, attn_w_in, attn_rel_bias, attn_w_out, rg_w_in, rg_conv_w, rg_conv_b, rg_w_a, rg_b_a,
           rg_w_i, rg_b_i, rg_lambda, rg_w_out, norm_mix_pre, norm_mix_post, norm_ffn_pre,
           norm_ffn_post, ffn_w_gate, ffn_w_up, ffn_w_down):
    b, s, d = x.shape
    depth = norm_mix_pre.shape[0]
    xf = x.reshape(b * s, d)
    row = lambda v: v.reshape(1, -1)
    for layer in range(depth):
        j = layer // 2
        g_pre = row(norm_mix_pre[layer])
        g_post = row(norm_mix_post[layer])
        if layer % 2 == 0:
            proj = norm_linear(xf, g_pre, attn_w_in[j].astype(BF16), BF16).reshape(b, s, -1)
            out_a = band_attention(proj, band_bias(attn_rel_bias[j]), 0)
            out_b = stick_breaking_attention(proj, 3 * A_W)
            w_out = attn_w_out[j].astype(BF16)
            ins = [out_a.reshape(b * s, A_W), out_b.reshape(b * s, B_W)]
            ws = [w_out[:A_W], w_out[A_W:]]
        else:
            width = rg_w_out.shape[1]
            y = rglru_mixer(
                xf.reshape(b, s, d), g_pre, rg_w_in[j].astype(BF16),
                rg_conv_w[j].reshape(-1, width), row(rg_conv_b[j]),
                rg_w_a[j].astype(BF16), row(rg_b_a[j]),
                rg_w_i[j].astype(BF16), row(rg_b_i[j]), row(rg_lambda[j]))
            ins = [y.reshape(b * s, width)]
            ws = [rg_w_out[j].astype(BF16)]
        xf = mixer_out_ffn(
            xf, g_post, ins, ws, row(norm_ffn_pre[layer]), row(norm_ffn_post[layer]),
            ffn_w_gate[layer].astype(BF16), ffn_w_up[layer].astype(BF16),
            ffn_w_down[layer].astype(BF16))
    return xf.reshape(b, s, d)
```

```python
import functools

import jax
import jax.numpy as jnp
from jax import lax
from jax.experimental import pallas as pl
from jax.experimental.pallas import tpu as pltpu

F32 = jnp.float32
BF16 = jnp.bfloat16

D_MODEL = 1024
HEAD_DIM = 64
CHUNK = 64
N_LEFT_CHUNKS = 8
REL_CLIP = 256
A_W = 512
B_W = 512
LRU_BLOCKS = 4
LRU_BLOCK_W = 256
LRU_C = 8.0
RMS_EPS = 1e-6

LANES = 128
VMEM_LIMIT = 56 * 1024 * 1024

ROW_TILE = 512
SB_TILE = 256
SB_SUBTILES = 4
SB_UNROLL = 2
SB_EXIT = 106.0
A_TILE = 2 * CHUNK
A_SUBTILES = 16
A_WIN = (N_LEFT_CHUNKS + 2) * CHUNK
A_PAD = N_LEFT_CHUNKS * CHUNK
REC_TILE = 512
NEG_BIG = -1e30
LOG2E = 1.4426950408889634

_NT = (((1,), (1,)), ((), ()))


def _params(sem):
    return pltpu.CompilerParams(dimension_semantics=sem, vmem_limit_bytes=VMEM_LIMIT)


def _rms(x, g):
    ms = jnp.mean(x * x, axis=-1, keepdims=True)
    return x * lax.rsqrt(ms + RMS_EPS) * g


def _sigmoid(x):
    return 1.0 / (1.0 + jnp.exp(-x))


def _norm_linear_kernel(x_ref, g_ref, w_ref, o_ref):
    h = _rms(x_ref[...], g_ref[...]).astype(BF16)
    o_ref[...] = jnp.dot(h, w_ref[...], preferred_element_type=F32).astype(o_ref.dtype)


def norm_linear(x, g, w, out_dtype):
    m, d = x.shape
    n = w.shape[1]
    return pl.pallas_call(
        _norm_linear_kernel,
        out_shape=jax.ShapeDtypeStruct((m, n), out_dtype),
        grid=(m // ROW_TILE,),
        in_specs=[
            pl.BlockSpec((ROW_TILE, d), lambda i: (i, 0)),
            pl.BlockSpec((1, d), lambda i: (0, 0)),
            pl.BlockSpec((d, n), lambda i: (0, 0)),
        ],
        out_specs=pl.BlockSpec((ROW_TILE, n), lambda i: (i, 0)),
        compiler_params=_params(("arbitrary",)),
        name="norm_linear",
    )(x, g, w)


def _out_ffn_kernel(n_in, *refs):
    x_ref, gmix_ref, gpre_ref, gpost_ref, wg_ref, wu_ref, wd_ref = refs[:7]
    ins = refs[7:7 + n_in]
    ws = refs[7 + n_in:7 + 2 * n_in]
    o_ref = refs[7 + 2 * n_in]
    m = jnp.dot(ins[0][...], ws[0][...], preferred_element_type=F32)
    for a, w in zip(ins[1:], ws[1:]):
        m = m + jnp.dot(a[...], w[...], preferred_element_type=F32)
    x = x_ref[...] + _rms(m, gmix_ref[...])
    h = _rms(x, gpre_ref[...]).astype(BF16)
    gate = jnp.dot(h, wg_ref[...], preferred_element_type=F32)
    up = jnp.dot(h, wu_ref[...], preferred_element_type=F32)
    act = (gate * _sigmoid(gate) * up).astype(BF16)
    f = jnp.dot(act, wd_ref[...], preferred_element_type=F32)
    o_ref[...] = x + _rms(f, gpost_ref[...])


def _resident(shape):
    return pl.BlockSpec(shape, lambda i: (0,) * len(shape), pipeline_mode=pl.Buffered(1))


def mixer_out_ffn(x, gmix, ins, ws, gpre, gpost, wg, wu, wd):
    m, d = x.shape
    n_in = len(ins)
    in_specs = [pl.BlockSpec((ROW_TILE, d), lambda i: (i, 0))]
    in_specs += [_resident(a.shape) for a in (gmix, gpre, gpost, wg, wu, wd)]
    in_specs += [pl.BlockSpec((ROW_TILE, a.shape[1]), lambda i: (i, 0)) for a in ins]
    in_specs += [_resident(w.shape) for w in ws]
    return pl.pallas_call(
        functools.partial(_out_ffn_kernel, n_in),
        out_shape=jax.ShapeDtypeStruct((m, d), F32),
        grid=(m // ROW_TILE,),
        in_specs=in_specs,
        out_specs=pl.BlockSpec((ROW_TILE, d), lambda i: (i, 0)),
        compiler_params=_params(("arbitrary",)),
        name="mixer_out_ffn",
    )(x, gmix, gpre, gpost, wg, wu, wd, *ins, *ws)


def _band_attn_kernel(q_ref, k_ref, v_ref, bias_ref, o_ref, kp_ref, vp_ref, s_ref, p_ref, rinv_ref):
    t = pl.program_id(2)

    @pl.when(t == 0)
    def _():
        kp_ref[0:A_PAD, :] = jnp.zeros((A_PAD, LANES), BF16)
        vp_ref[0:A_PAD, :] = jnp.zeros((A_PAD, LANES), BF16)
        kp_ref[A_PAD:, :] = k_ref[0]
        vp_ref[A_PAD:, :] = v_ref[0]

    lane = lax.broadcasted_iota(jnp.int32, (1, LANES), 1)
    first = lane < HEAD_DIM
    col = lax.broadcasted_iota(jnp.int32, (1, A_WIN), 1)
    bias = jnp.concatenate([bias_ref[0], bias_ref[1]], axis=0)
    zero = jnp.zeros((A_TILE, LANES), BF16)

    def window(u):
        tile = t * A_SUBTILES + u
        return tile, pl.multiple_of(tile * A_TILE, A_TILE)

    def scores(u, slot):
        tile, start = window(u)
        kw = kp_ref[pl.ds(start, A_WIN), :]
        q2 = q_ref[0, pl.ds(pl.multiple_of(u * A_TILE, A_TILE), A_TILE), :] * (HEAD_DIM ** -0.5)
        qs = jnp.concatenate([jnp.where(first, q2, zero), jnp.where(first, zero, q2)], axis=0)
        s = lax.dot_general(qs, kw, _NT, preferred_element_type=F32) + bias
        in_seq = col >= (A_PAD - tile * A_TILE)
        s_ref[slot] = jnp.where(in_seq, s, NEG_BIG)

    def softmax(slot):
        s = s_ref[slot]
        p = jnp.exp(s - jnp.max(s, axis=-1, keepdims=True))
        p_ref[slot] = p.astype(BF16)
        rinv_ref[slot] = jnp.broadcast_to(1.0 / jnp.sum(p, axis=-1, keepdims=True), (2 * A_TILE, LANES))

    def values(u, slot):
        _, start = window(u)
        vw = vp_ref[pl.ds(start, A_WIN), :]
        o = jnp.dot(p_ref[slot], vw, preferred_element_type=F32) * rinv_ref[slot]
        out = jnp.where(first, o[:A_TILE], o[A_TILE:])
        o_ref[0, pl.ds(pl.multiple_of(u * A_TILE, A_TILE), A_TILE), :] = out.astype(o_ref.dtype)

    scores(0, 0)
    scores(1, 1)
    softmax(0)

    def body(k, _):
        values(2 * k - 2, 0)
        softmax(1)
        scores(2 * k, 0)
        values(2 * k - 1, 1)
        softmax(0)
        scores(2 * k + 1, 1)
        return 0

    lax.fori_loop(1, A_SUBTILES // 2, body, 0)
    values(A_SUBTILES - 2, 0)
    softmax(1)
    values(A_SUBTILES - 1, 1)


def band_attention(proj, bias, col0):
    b, s, _ = proj.shape
    nq = col0 // LANES
    nk = (col0 + A_W) // LANES
    nv = (col0 + 2 * A_W) // LANES
    tq = A_SUBTILES * A_TILE
    return pl.pallas_call(
        _band_attn_kernel,
        out_shape=jax.ShapeDtypeStruct((b, s, A_W), BF16),
        grid=(b, A_W // LANES, s // tq),
        in_specs=[
            pl.BlockSpec((1, tq, LANES), lambda bi, p, t: (bi, t, nq + p)),
            pl.BlockSpec((1, s, LANES), lambda bi, p, t: (bi, 0, nk + p)),
            pl.BlockSpec((1, s, LANES), lambda bi, p, t: (bi, 0, nv + p)),
            pl.BlockSpec((2, A_TILE, A_WIN), lambda bi, p, t: (p, 0, 0)),
        ],
        out_specs=pl.BlockSpec((1, tq, LANES), lambda bi, p, t: (bi, t, p)),
        scratch_shapes=[
            pltpu.VMEM((s + A_PAD, LANES), BF16),
            pltpu.VMEM((s + A_PAD, LANES), BF16),
            pltpu.VMEM((2, 2 * A_TILE, A_WIN), F32),
            pltpu.VMEM((2, 2 * A_TILE, A_WIN), BF16),
            pltpu.VMEM((2, 2 * A_TILE, LANES), F32),
        ],
        compiler_params=_params(("arbitrary", "arbitrary", "arbitrary")),
        name="band_attention",
    )(proj, proj, proj, bias)


def band_bias(rel_bias):
    h = rel_bias.shape[0]
    n_f = A_WIN + A_TILE - 1
    n_const = A_PAD + A_TILE - 1 - REL_CLIP
    far = jnp.broadcast_to(rel_bias[:, 2 * REL_CLIP:], (h, n_const))
    near = rel_bias[:, 2 * REL_CLIP + 1 - (n_f - n_const):][:, ::-1]
    f = jnp.concatenate([far, near], axis=1)
    period = n_f + 1
    g = jnp.concatenate([f[:, A_TILE - 1:], jnp.zeros((h, 1), F32), f[:, :A_TILE - 1]], axis=1)
    flat = jnp.tile(g, (1, A_TILE))[:, :A_TILE * (period - 1)]
    toep = flat.reshape(h, A_TILE, period - 1)[:, :, :A_WIN]
    r = jnp.arange(A_TILE)[:, None]
    c = jnp.arange(A_WIN)[None, :]
    dchunk = c // CHUNK - r // CHUNK
    ok = (dchunk >= 0) & (dchunk <= N_LEFT_CHUNKS)
    return jnp.where(ok[None], toep, NEG_BIG).astype(F32)


def _stick_kernel(q_ref, k_ref, v_ref, o_ref, acc_ref, carry_ref, z_ref, sp_ref, kmax_ref):
    qi = pl.program_id(2)
    t = SB_TILE
    ns = SB_SUBTILES
    lane = lax.broadcasted_iota(jnp.int32, (1, LANES), 1)
    first = lane < HEAD_DIM
    q2 = q_ref[0] * (HEAD_DIM ** -0.5)
    row = lax.broadcasted_iota(jnp.int32, (t, t), 0)
    col = lax.broadcasted_iota(jnp.int32, (t, t), 1)
    causal = col < row
    true = jnp.ones((t, t), jnp.bool_)
    suffix = jnp.where(row >= col, 1.0, 0.0).astype(BF16)
    zero = jnp.zeros((t, LANES), BF16)
    acc_ref[...] = jnp.zeros_like(acc_ref)
    carry_ref[...] = jnp.zeros_like(carry_ref)

    def split_heads(x):
        return jnp.concatenate([jnp.where(first, x, zero), jnp.where(first, zero, x)], axis=0)

    def scores(j, d, mask):
        start = pl.multiple_of(j * t, t)
        kcat = split_heads(k_ref[0, pl.ds(start, t), :])
        z = lax.dot_general(q2[d * t:], kcat, _NT, preferred_element_type=F32)
        sp = jnp.maximum(z, 0.0) + jnp.log(1.0 + jnp.exp2(jnp.abs(z) * (-LOG2E)))
        if mask is not None:
            sp = jnp.where(mask, sp, 0.0)
        return z, sp.astype(BF16)

    def weights(j, d, mask, z, spb):
        r0 = d * t
        n = (ns - d) * t
        start = pl.multiple_of(j * t, t)
        vcat = split_heads(v_ref[0, pl.ds(start, t), :])
        c = [jnp.dot(spb[:, h * t:(h + 1) * t], suffix, preferred_element_type=F32) for h in range(2)]
        carry = [carry_ref[h, r0:, :] for h in range(2)]
        reps = t // LANES
        shift = jnp.concatenate([c[0]] + [c[1]], axis=1) + jnp.concatenate(
            [carry[0]] * reps + [carry[1]] * reps, axis=1)
        w = jnp.exp(z - shift)
        if mask is not None:
            w = jnp.where(mask, w, 0.0)
        for h in range(2):
            carry_ref[h, r0:, :] = carry[h] + jnp.broadcast_to(c[h][:, 0:1], (n, LANES))
        acc_ref[r0:, :] += jnp.dot(w.astype(BF16), vcat, preferred_element_type=F32)

    for d in range(ns - 1, -1, -1):
        mask = jnp.concatenate([causal] + [true] * (ns - d - 1), axis=0)
        mask = jnp.concatenate([mask, mask], axis=1)
        z, spb = scores(ns * qi + d, d, mask)
        weights(ns * qi + d, d, mask, z, spb)

    def stage1(i, slot):
        z, spb = scores(ns * qi - 1 - i, 0, None)
        z_ref[slot] = z
        sp_ref[slot] = spb

    def stage2(i, slot):
        weights(ns * qi - 1 - i, 0, None, z_ref[slot], sp_ref[slot])

    @pl.when(qi == 0)
    def _():
        kf = k_ref[0].astype(F32)
        kn2 = jnp.sum(kf * kf, axis=-1, keepdims=True)
        kmax_ref[...] = jnp.broadcast_to(jnp.max(kn2, axis=0, keepdims=True), kmax_ref.shape)

    @pl.when(qi > 0)
    def _():
        qf = q2.astype(F32)
        qmax2 = jnp.max(jnp.sum(qf * qf, axis=-1, keepdims=True), axis=0, keepdims=True)
        zmax = jnp.sqrt(qmax2 * kmax_ref[0:1, 0:1])
        limit = jnp.max(SB_EXIT + (2.0 ** -8) * zmax)
        n_groups = (ns * qi) // SB_UNROLL

        def cond(state):
            g, done = state
            return jnp.logical_and(g < n_groups, done == 0)

        def body(state):
            g, _ = state
            i0 = SB_UNROLL * g
            stage1(i0, 0)
            for off in range(SB_UNROLL - 1):
                stage1(i0 + off + 1, (off + 1) % 2)
                stage2(i0 + off, off % 2)
            stage2(i0 + SB_UNROLL - 1, (SB_UNROLL - 1) % 2)
            low = jnp.min(jnp.minimum(carry_ref[0], carry_ref[1]), axis=0, keepdims=True)
            return g + 1, (jnp.min(low) >= limit).astype(jnp.int32)

        lax.while_loop(cond, body, (jnp.int32(0), jnp.int32(0)))

    o_ref[0] = acc_ref[...].astype(o_ref.dtype)


def stick_breaking_attention(proj, col0):
    b, s, _ = proj.shape
    nq = col0 // LANES
    nk = (col0 + B_W) // LANES
    nv = (col0 + 2 * B_W) // LANES
    tq = SB_SUBTILES * SB_TILE
    return pl.pallas_call(
        _stick_kernel,
        out_shape=jax.ShapeDtypeStruct((b, s, B_W), BF16),
        grid=(b, B_W // LANES, s // tq),
        in_specs=[
            pl.BlockSpec((1, tq, LANES), lambda bi, p, t: (bi, t, nq + p)),
            pl.BlockSpec((1, s, LANES), lambda bi, p, t: (bi, 0, nk + p)),
            pl.BlockSpec((1, s, LANES), lambda bi, p, t: (bi, 0, nv + p)),
        ],
        out_specs=pl.BlockSpec((1, tq, LANES), lambda bi, p, t: (bi, t, p)),
        scratch_shapes=[
            pltpu.VMEM((tq, LANES), F32),
            pltpu.VMEM((2, tq, LANES), F32),
            pltpu.VMEM((2, tq, 2 * SB_TILE), F32),
            pltpu.VMEM((2, tq, 2 * SB_TILE), BF16),
            pltpu.VMEM((8, LANES), F32),
        ],
        compiler_params=_params(("arbitrary", "arbitrary", "arbitrary")),
        name="stick_breaking",
    )(proj, proj, proj)


def _rglru_kernel(x_ref, g_ref, win_ref, cw_ref, cb_ref, wa_ref, ba_ref, wi_ref, bi_ref, lam_ref,
                  y_ref, xpad_ref, a_ref, u_ref, h_ref):
    t = pl.program_id(1)
    n = REC_TILE
    width = y_ref.shape[2]

    @pl.when(t == 0)
    def _():
        xpad_ref[0:8, :] = jnp.zeros((8, width), F32)
        h_ref[...] = jnp.zeros_like(h_ref)

    hn = _rms(x_ref[0], g_ref[...]).astype(BF16)
    gate = jnp.dot(hn, win_ref[:, :width], preferred_element_type=F32)
    c = 0.7978845608028654
    gate = 0.5 * gate * (1.0 + jnp.tanh(c * (gate + 0.044715 * (gate * gate * gate))))
    xpad_ref[8:8 + n, :] = jnp.dot(hn, win_ref[:, width:], preferred_element_type=F32)
    cw = cw_ref[...]
    xc = (cb_ref[...]
          + cw[3:4] * xpad_ref[8:8 + n, :]
          + cw[2:3] * xpad_ref[7:7 + n, :]
          + cw[1:2] * xpad_ref[6:6 + n, :]
          + cw[0:1] * xpad_ref[5:5 + n, :])
    xpad_ref[0:8, :] = xpad_ref[n:n + 8, :]
    xcb = xc.astype(BF16)
    lam = lam_ref[...]
    sp_lam = jnp.maximum(-lam, 0.0) + jnp.log1p(jnp.exp(-jnp.abs(lam)))
    for blk in range(LRU_BLOCKS):
        sl = slice(blk * LRU_BLOCK_W, (blk + 1) * LRU_BLOCK_W)
        xb = xcb[:, sl]
        pre_r = jnp.dot(xb, wa_ref[blk], preferred_element_type=F32) + ba_ref[:, sl]
        pre_i = jnp.dot(xb, wi_ref[blk], preferred_element_type=F32) + bi_ref[:, sl]
        r = 0.5 * jnp.tanh(0.5 * pre_r) + 0.5
        i = 0.5 * jnp.tanh(0.5 * pre_i) + 0.5
        log_a = (-LRU_C * r) * sp_lam[:, sl]
        a = jnp.exp(log_a)
        m2 = -jnp.tanh(log_a) * (a * a + 1.0)
        mult = m2 * lax.rsqrt(jnp.maximum(m2, 1e-30))
        a_ref[:, sl] = a
        u_ref[:, sl] = mult * (i * xc[:, sl])

    rowid = lax.broadcasted_iota(jnp.int32, (8, width), 0)

    def group(g, hprev):
        r0 = pl.multiple_of(g * 8, 8)
        a = a_ref[pl.ds(r0, 8), :]
        b = u_ref[pl.ds(r0, 8), :]
        for d in (1, 2, 4):
            keep = rowid >= d
            a_sh = jnp.where(keep, pltpu.roll(a, d, 0), 1.0)
            b_sh = jnp.where(keep, pltpu.roll(b, d, 0), 0.0)
            b = a * b_sh + b
            a = a * a_sh
        out = b + a * hprev
        u_ref[pl.ds(r0, 8), :] = out
        return out[7:8, :]

    h_ref[...] = lax.fori_loop(0, n // 8, group, h_ref[...], unroll=4)
    y_ref[0] = (u_ref[...] * gate).astype(y_ref.dtype)


def rglru_mixer(x, g, w_in, conv_w, conv_b, w_a, b_a, w_i, b_i, lam):
    b, s, d = x.shape
    width = w_in.shape[1] // 2
    const2 = lambda bi, t: (0, 0)
    const3 = lambda bi, t: (0, 0, 0)
    return pl.pallas_call(
        _rglru_kernel,
        out_shape=jax.ShapeDtypeStruct((b, s, width), BF16),
        grid=(b, s // REC_TILE),
        in_specs=[
            pl.BlockSpec((1, REC_TILE, d), lambda bi, t: (bi, t, 0)),
            pl.BlockSpec((1, d), const2),
            pl.BlockSpec(w_in.shape, const2),
            pl.BlockSpec(conv_w.shape, const2),
            pl.BlockSpec((1, width), const2),
            pl.BlockSpec(w_a.shape, const3),
            pl.BlockSpec((1, width), const2),
            pl.BlockSpec(w_i.shape, const3),
            pl.BlockSpec((1, width), const2),
            pl.BlockSpec((1, width), const2),
        ],
        out_specs=pl.BlockSpec((1, REC_TILE, width), lambda bi, t: (bi, t, 0)),
        scratch_shapes=[
            pltpu.VMEM((REC_TILE + 8, width), F32),
            pltpu.VMEM((REC_TILE, width), F32),
            pltpu.VMEM((REC_TILE, width), F32),
            pltpu.VMEM((1, width), F32),
        ],
        compiler_params=_params(("arbitrary", "arbitrary")),
        name="rglru_mixer",
    )(x, g, w_in, conv_w, conv_b, w_a, b_a, w_i, b_i, lam)


def kernel(x, attn_w_in, attn_rel_bias, attn_w_out, rg_w_in, rg_conv_w, rg_conv_b, rg_w_a, rg_b_a,
           rg_w_i, rg_b_i, rg_lambda, rg_w_out, norm_mix_pre, norm_mix_post, norm_ffn_pre,
           norm_ffn_post, ffn_w_gate, ffn_w_up, ffn_w_down):
    b, s, d = x.shape
    depth = norm_mix_pre.shape[0]
    xf = x.reshape(b * s, d)
    row = lambda v: v.reshape(1, -1)
    for layer in range(depth):
        j = layer // 2
        g_pre = row(norm_mix_pre[layer])
        g_post = row(norm_mix_post[layer])
        if layer % 2 == 0:
            proj = norm_linear(xf, g_pre, attn_w_in[j].astype(BF16), BF16).reshape(b, s, -1)
            out_a = band_attention(proj, band_bias(attn_rel_bias[j]), 0)
            out_b = stick_breaking_attention(proj, 3 * A_W)
            w_out = attn_w_out[j].astype(BF16)
            ins = [out_a.reshape(b * s, A_W), out_b.reshape(b * s, B_W)]
            ws = [w_out[:A_W], w_out[A_W:]]
        else:
            width = rg_w_out.shape[1]
            y = rglru_mixer(
                xf.reshape(b, s, d), g_pre, rg_w_in[j].astype(BF16),
                rg_conv_w[j].reshape(-1, width), row(rg_conv_b[j]),
                rg_w_a[j].astype(BF16), row(rg_b_a[j]),
                rg_w_i[j].astype(BF16), row(rg_b_i[j]), row(rg_lambda[j]))
            ins = [y.reshape(b * s, width)]
            ws = [rg_w_out[j].astype(BF16)]
        xf = mixer_out_ffn(
            xf, g_post, ins, ws, row(norm_ffn_pre[layer]), row(norm_ffn_post[layer]),
            ffn_w_gate[layer].astype(BF16), ffn_w_up[layer].astype(BF16),
            ffn_w_down[layer].astype(BF16))
    return xf.reshape(b, s, d)
```

```python
import functools

import jax
import jax.numpy as jnp
from jax import lax
from jax.experimental import pallas as pl
from jax.experimental.pallas import tpu as pltpu

F32 = jnp.float32
BF16 = jnp.bfloat16

D_MODEL = 1024
HEAD_DIM = 64
CHUNK = 64
N_LEFT_CHUNKS = 8
REL_CLIP = 256
A_W = 512
B_W = 512
LRU_BLOCKS = 4
LRU_BLOCK_W = 256
LRU_C = 8.0
RMS_EPS = 1e-6

LANES = 128
VMEM_LIMIT = 56 * 1024 * 1024

ROW_TILE = 512
SB_TILE = 256
SB_SUBTILES = 2
SB_UNROLL = 1
SB_EXIT = 106.0
A_TILE = 2 * CHUNK
A_SUBTILES = 16
A_WIN = (N_LEFT_CHUNKS + 2) * CHUNK
A_PAD = N_LEFT_CHUNKS * CHUNK
REC_TILE = 512
NEG_BIG = -1e30
LOG2E = 1.4426950408889634

_NT = (((1,), (1,)), ((), ()))


def _params(sem):
    return pltpu.CompilerParams(dimension_semantics=sem, vmem_limit_bytes=VMEM_LIMIT)


def _rms(x, g):
    ms = jnp.mean(x * x, axis=-1, keepdims=True)
    return x * lax.rsqrt(ms + RMS_EPS) * g


def _sigmoid(x):
    return 1.0 / (1.0 + jnp.exp(-x))


def _norm_linear_kernel(x_ref, g_ref, w_ref, o_ref):
    h = _rms(x_ref[...], g_ref[...]).astype(BF16)
    o_ref[...] = jnp.dot(h, w_ref[...], preferred_element_type=F32).astype(o_ref.dtype)


def norm_linear(x, g, w, out_dtype):
    m, d = x.shape
    n = w.shape[1]
    return pl.pallas_call(
        _norm_linear_kernel,
        out_shape=jax.ShapeDtypeStruct((m, n), out_dtype),
        grid=(m // ROW_TILE,),
        in_specs=[
            pl.BlockSpec((ROW_TILE, d), lambda i: (i, 0)),
            pl.BlockSpec((1, d), lambda i: (0, 0)),
            pl.BlockSpec((d, n), lambda i: (0, 0)),
        ],
        out_specs=pl.BlockSpec((ROW_TILE, n), lambda i: (i, 0)),
        compiler_params=_params(("arbitrary",)),
        name="norm_linear",
    )(x, g, w)


def _out_ffn_kernel(n_in, *refs):
    x_ref, gmix_ref, gpre_ref, gpost_ref, wg_ref, wu_ref, wd_ref = refs[:7]
    ins = refs[7:7 + n_in]
    ws = refs[7 + n_in:7 + 2 * n_in]
    o_ref = refs[7 + 2 * n_in]
    m = jnp.dot(ins[0][...], ws[0][...], preferred_element_type=F32)
    for a, w in zip(ins[1:], ws[1:]):
        m = m + jnp.dot(a[...], w[...], preferred_element_type=F32)
    x = x_ref[...] + _rms(m, gmix_ref[...])
    h = _rms(x, gpre_ref[...]).astype(BF16)
    gate = jnp.dot(h, wg_ref[...], preferred_element_type=F32)
    up = jnp.dot(h, wu_ref[...], preferred_element_type=F32)
    act = (gate * _sigmoid(gate) * up).astype(BF16)
    f = jnp.dot(act, wd_ref[...], preferred_element_type=F32)
    o_ref[...] = x + _rms(f, gpost_ref[...])


def _resident(shape):
    return pl.BlockSpec(shape, lambda i: (0,) * len(shape), pipeline_mode=pl.Buffered(1))


def mixer_out_ffn(x, gmix, ins, ws, gpre, gpost, wg, wu, wd):
    m, d = x.shape
    n_in = len(ins)
    in_specs = [pl.BlockSpec((ROW_TILE, d), lambda i: (i, 0))]
    in_specs += [_resident(a.shape) for a in (gmix, gpre, gpost, wg, wu, wd)]
    in_specs += [pl.BlockSpec((ROW_TILE, a.shape[1]), lambda i: (i, 0)) for a in ins]
    in_specs += [_resident(w.shape) for w in ws]
    return pl.pallas_call(
        functools.partial(_out_ffn_kernel, n_in),
        out_shape=jax.ShapeDtypeStruct((m, d), F32),
        grid=(m // ROW_TILE,),
        in_specs=in_specs,
        out_specs=pl.BlockSpec((ROW_TILE, d), lambda i: (i, 0)),
        compiler_params=_params(("arbitrary",)),
        name="mixer_out_ffn",
    )(x, gmix, gpre, gpost, wg, wu, wd, *ins, *ws)


def _band_attn_kernel(q_ref, k_ref, v_ref, bias_ref, o_ref, kp_ref, vp_ref, s_ref, p_ref, rinv_ref):
    t = pl.program_id(2)

    @pl.when(t == 0)
    def _():
        kp_ref[0:A_PAD, :] = jnp.zeros((A_PAD, LANES), BF16)
        vp_ref[0:A_PAD, :] = jnp.zeros((A_PAD, LANES), BF16)
        kp_ref[A_PAD:, :] = k_ref[0]
        vp_ref[A_PAD:, :] = v_ref[0]

    lane = lax.broadcasted_iota(jnp.int32, (1, LANES), 1)
    first = lane < HEAD_DIM
    col = lax.broadcasted_iota(jnp.int32, (1, A_WIN), 1)
    bias = jnp.concatenate([bias_ref[0], bias_ref[1]], axis=0)
    zero = jnp.zeros((A_TILE, LANES), BF16)

    def window(u):
        tile = t * A_SUBTILES + u
        return tile, pl.multiple_of(tile * A_TILE, A_TILE)

    def scores(u, slot):
        tile, start = window(u)
        kw = kp_ref[pl.ds(start, A_WIN), :]
        q2 = q_ref[0, pl.ds(pl.multiple_of(u * A_TILE, A_TILE), A_TILE), :] * (HEAD_DIM ** -0.5)
        qs = jnp.concatenate([jnp.where(first, q2, zero), jnp.where(first, zero, q2)], axis=0)
        s = lax.dot_general(qs, kw, _NT, preferred_element_type=F32) + bias
        in_seq = col >= (A_PAD - tile * A_TILE)
        s_ref[slot] = jnp.where(in_seq, s, NEG_BIG)

    def softmax(slot):
        s = s_ref[slot]
        p = jnp.exp(s - jnp.max(s, axis=-1, keepdims=True))
        p_ref[slot] = p.astype(BF16)
        rinv_ref[slot] = jnp.broadcast_to(1.0 / jnp.sum(p, axis=-1, keepdims=True), (2 * A_TILE, LANES))

    def values(u, slot):
        _, start = window(u)
        vw = vp_ref[pl.ds(start, A_WIN), :]
        o = jnp.dot(p_ref[slot], vw, preferred_element_type=F32) * rinv_ref[slot]
        out = jnp.where(first, o[:A_TILE], o[A_TILE:])
        o_ref[0, pl.ds(pl.multiple_of(u * A_TILE, A_TILE), A_TILE), :] = out.astype(o_ref.dtype)

    scores(0, 0)
    scores(1, 1)
    softmax(0)

    def body(k, _):
        values(2 * k - 2, 0)
        softmax(1)
        scores(2 * k, 0)
        values(2 * k - 1, 1)
        softmax(0)
        scores(2 * k + 1, 1)
        return 0

    lax.fori_loop(1, A_SUBTILES // 2, body, 0)
    values(A_SUBTILES - 2, 0)
    softmax(1)
    values(A_SUBTILES - 1, 1)


def band_attention(proj, bias, col0):
    b, s, _ = proj.shape
    nq = col0 // LANES
    nk = (col0 + A_W) // LANES
    nv = (col0 + 2 * A_W) // LANES
    tq = A_SUBTILES * A_TILE
    return pl.pallas_call(
        _band_attn_kernel,
        out_shape=jax.ShapeDtypeStruct((b, s, A_W), BF16),
        grid=(b, A_W // LANES, s // tq),
        in_specs=[
            pl.BlockSpec((1, tq, LANES), lambda bi, p, t: (bi, t, nq + p)),
            pl.BlockSpec((1, s, LANES), lambda bi, p, t: (bi, 0, nk + p)),
            pl.BlockSpec((1, s, LANES), lambda bi, p, t: (bi, 0, nv + p)),
            pl.BlockSpec((2, A_TILE, A_WIN), lambda bi, p, t: (p, 0, 0)),
        ],
        out_specs=pl.BlockSpec((1, tq, LANES), lambda bi, p, t: (bi, t, p)),
        scratch_shapes=[
            pltpu.VMEM((s + A_PAD, LANES), BF16),
            pltpu.VMEM((s + A_PAD, LANES), BF16),
            pltpu.VMEM((2, 2 * A_TILE, A_WIN), F32),
            pltpu.VMEM((2, 2 * A_TILE, A_WIN), BF16),
            pltpu.VMEM((2, 2 * A_TILE, LANES), F32),
        ],
        compiler_params=_params(("arbitrary", "arbitrary", "arbitrary")),
        name="band_attention",
    )(proj, proj, proj, bias)


def band_bias(rel_bias):
    h = rel_bias.shape[0]
    n_f = A_WIN + A_TILE - 1
    n_const = A_PAD + A_TILE - 1 - REL_CLIP
    far = jnp.broadcast_to(rel_bias[:, 2 * REL_CLIP:], (h, n_const))
    near = rel_bias[:, 2 * REL_CLIP + 1 - (n_f - n_const):][:, ::-1]
    f = jnp.concatenate([far, near], axis=1)
    period = n_f + 1
    g = jnp.concatenate([f[:, A_TILE - 1:], jnp.zeros((h, 1), F32), f[:, :A_TILE - 1]], axis=1)
    flat = jnp.tile(g, (1, A_TILE))[:, :A_TILE * (period - 1)]
    toep = flat.reshape(h, A_TILE, period - 1)[:, :, :A_WIN]
    r = jnp.arange(A_TILE)[:, None]
    c = jnp.arange(A_WIN)[None, :]
    dchunk = c // CHUNK - r // CHUNK
    ok = (dchunk >= 0) & (dchunk <= N_LEFT_CHUNKS)
    return jnp.where(ok[None], toep, NEG_BIG).astype(F32)


def _stick_kernel(q_ref, k_ref, v_ref, o_ref, acc_ref, carry_ref, z_ref, sp_ref, kmax_ref):
    qi = pl.program_id(2)
    t = SB_TILE
    ns = SB_SUBTILES
    lane = lax.broadcasted_iota(jnp.int32, (1, LANES), 1)
    first = lane < HEAD_DIM
    q2 = q_ref[0] * (HEAD_DIM ** -0.5)
    row = lax.broadcasted_iota(jnp.int32, (t, t), 0)
    col = lax.broadcasted_iota(jnp.int32, (t, t), 1)
    causal = col < row
    true = jnp.ones((t, t), jnp.bool_)
    suffix = jnp.where(row >= col, 1.0, 0.0).astype(BF16)
    zero = jnp.zeros((t, LANES), BF16)
    acc_ref[...] = jnp.zeros_like(acc_ref)
    carry_ref[...] = jnp.zeros_like(carry_ref)

    def split_heads(x):
        return jnp.concatenate([jnp.where(first, x, zero), jnp.where(first, zero, x)], axis=0)

    def scores(j, d, mask):
        start = pl.multiple_of(j * t, t)
        kcat = split_heads(k_ref[0, pl.ds(start, t), :])
        z = lax.dot_general(q2[d * t:], kcat, _NT, preferred_element_type=F32)
        sp = jnp.maximum(z, 0.0) + jnp.log(1.0 + jnp.exp2(jnp.abs(z) * (-LOG2E)))
        if mask is not None:
            sp = jnp.where(mask, sp, 0.0)
        return z, sp.astype(BF16)

    def weights(j, d, mask, z, spb):
        r0 = d * t
        n = (ns - d) * t
        start = pl.multiple_of(j * t, t)
        vcat = split_heads(v_ref[0, pl.ds(start, t), :])
        c = [jnp.dot(spb[:, h * t:(h + 1) * t], suffix, preferred_element_type=F32) for h in range(2)]
        carry = [carry_ref[h, r0:, :] for h in range(2)]
        reps = t // LANES
        shift = jnp.concatenate([c[0]] + [c[1]], axis=1) + jnp.concatenate(
            [carry[0]] * reps + [carry[1]] * reps, axis=1)
        w = jnp.exp(z - shift)
        if mask is not None:
            w = jnp.where(mask, w, 0.0)
        for h in range(2):
            carry_ref[h, r0:, :] = carry[h] + jnp.broadcast_to(c[h][:, 0:1], (n, LANES))
        acc_ref[r0:, :] += jnp.dot(w.astype(BF16), vcat, preferred_element_type=F32)

    for d in range(ns - 1, -1, -1):
        mask = jnp.concatenate([causal] + [true] * (ns - d - 1), axis=0)
        mask = jnp.concatenate([mask, mask], axis=1)
        z, spb = scores(ns * qi + d, d, mask)
        weights(ns * qi + d, d, mask, z, spb)

    def stage1(i, slot):
        z, spb = scores(ns * qi - 1 - i, 0, None)
        z_ref[slot] = z
        sp_ref[slot] = spb

    def stage2(i, slot):
        weights(ns * qi - 1 - i, 0, None, z_ref[slot], sp_ref[slot])

    @pl.when(qi == 0)
    def _():
        kf = k_ref[0].astype(F32)
        kn2 = jnp.sum(kf * kf, axis=-1, keepdims=True)
        kmax_ref[...] = jnp.broadcast_to(jnp.max(kn2, axis=0, keepdims=True), kmax_ref.shape)

    @pl.when(qi > 0)
    def _():
        qf = q2.astype(F32)
        qmax2 = jnp.max(jnp.sum(qf * qf, axis=-1, keepdims=True), axis=0, keepdims=True)
        zmax = jnp.sqrt(qmax2 * kmax_ref[0:1, 0:1])
        limit = jnp.max(SB_EXIT + (2.0 ** -8) * zmax)
        n_groups = (ns * qi) // SB_UNROLL

        def cond(state):
            g, done = state
            return jnp.logical_and(g < n_groups, done == 0)

        def body(state):
            g, _ = state
            i0 = SB_UNROLL * g
            stage1(i0, 0)
            for off in range(SB_UNROLL - 1):
                stage1(i0 + off + 1, (off + 1) % 2)
                stage2(i0 + off, off % 2)
            stage2(i0 + SB_UNROLL - 1, (SB_UNROLL - 1) % 2)
            low = jnp.min(jnp.minimum(carry_ref[0], carry_ref[1]), axis=0, keepdims=True)
            return g + 1, (jnp.min(low) >= limit).astype(jnp.int32)

        lax.while_loop(cond, body, (jnp.int32(0), jnp.int32(0)))

    o_ref[0] = acc_ref[...].astype(o_ref.dtype)


def stick_breaking_attention(proj, col0):
    b, s, _ = proj.shape
    nq = col0 // LANES
    nk = (col0 + B_W) // LANES
    nv = (col0 + 2 * B_W) // LANES
    tq = SB_SUBTILES * SB_TILE
    return pl.pallas_call(
        _stick_kernel,
        out_shape=jax.ShapeDtypeStruct((b, s, B_W), BF16),
        grid=(b, B_W // LANES, s // tq),
        in_specs=[
            pl.BlockSpec((1, tq, LANES), lambda bi, p, t: (bi, t, nq + p)),
            pl.BlockSpec((1, s, LANES), lambda bi, p, t: (bi, 0, nk + p)),
            pl.BlockSpec((1, s, LANES), lambda bi, p, t: (bi, 0, nv + p)),
        ],
        out_specs=pl.BlockSpec((1, tq, LANES), lambda bi, p, t: (bi, t, p)),
        scratch_shapes=[
            pltpu.VMEM((tq, LANES), F32),
            pltpu.VMEM((2, tq, LANES), F32),
            pltpu.VMEM((2, tq, 2 * SB_TILE), F32),
            pltpu.VMEM((2, tq, 2 * SB_TILE), BF16),
            pltpu.VMEM((8, LANES), F32),
        ],
        compiler_params=_params(("arbitrary", "arbitrary", "arbitrary")),
        name="stick_breaking",
    )(proj, proj, proj)


def _rglru_kernel(x_ref, g_ref, win_ref, cw_ref, cb_ref, wa_ref, ba_ref, wi_ref, bi_ref, lam_ref,
                  y_ref, xpad_ref, a_ref, u_ref, h_ref):
    t = pl.program_id(1)
    n = REC_TILE
    width = y_ref.shape[2]

    @pl.when(t == 0)
    def _():
        xpad_ref[0:8, :] = jnp.zeros((8, width), F32)
        h_ref[...] = jnp.zeros_like(h_ref)

    hn = _rms(x_ref[0], g_ref[...]).astype(BF16)
    gate = jnp.dot(hn, win_ref[:, :width], preferred_element_type=F32)
    c = 0.7978845608028654
    gate = 0.5 * gate * (1.0 + jnp.tanh(c * (gate + 0.044715 * (gate * gate * gate))))
    xpad_ref[8:8 + n, :] = jnp.dot(hn, win_ref[:, width:], preferred_element_type=F32)
    cw = cw_ref[...]
    xc = (cb_ref[...]
          + cw[3:4] * xpad_ref[8:8 + n, :]
          + cw[2:3] * xpad_ref[7:7 + n, :]
          + cw[1:2] * xpad_ref[6:6 + n, :]
          + cw[0:1] * xpad_ref[5:5 + n, :])
    xpad_ref[0:8, :] = xpad_ref[n:n + 8, :]
    xcb = xc.astype(BF16)
    lam = lam_ref[...]
    sp_lam = jnp.maximum(-lam, 0.0) + jnp.log1p(jnp.exp(-jnp.abs(lam)))
    for blk in range(LRU_BLOCKS):
        sl = slice(blk * LRU_BLOCK_W, (blk + 1) * LRU_BLOCK_W)
        xb = xcb[:, sl]
        pre_r = jnp.dot(xb, wa_ref[blk], preferred_element_type=F32) + ba_ref[:, sl]
        pre_i = jnp.dot(xb, wi_ref[blk], preferred_element_type=F32) + bi_ref[:, sl]
        r = 0.5 * jnp.tanh(0.5 * pre_r) + 0.5
        i = 0.5 * jnp.tanh(0.5 * pre_i) + 0.5
        log_a = (-LRU_C * r) * sp_lam[:, sl]
        a = jnp.exp(log_a)
        m2 = -jnp.tanh(log_a) * (a * a + 1.0)
        mult = m2 * lax.rsqrt(jnp.maximum(m2, 1e-30))
        a_ref[:, sl] = a
        u_ref[:, sl] = mult * (i * xc[:, sl])

    rowid = lax.broadcasted_iota(jnp.int32, (8, width), 0)

    def group(g, hprev):
        r0 = pl.multiple_of(g * 8, 8)
        a = a_ref[pl.ds(r0, 8), :]
        b = u_ref[pl.ds(r0, 8), :]
        for d in (1, 2, 4):
            keep = rowid >= d
            a_sh = jnp.where(keep, pltpu.roll(a, d, 0), 1.0)
            b_sh = jnp.where(keep, pltpu.roll(b, d, 0), 0.0)
            b = a * b_sh + b
            a = a * a_sh
        out = b + a * hprev
        u_ref[pl.ds(r0, 8), :] = out
        return out[7:8, :]

    h_ref[...] = lax.fori_loop(0, n // 8, group, h_ref[...], unroll=4)
    y_ref[0] = (u_ref[...] * gate).astype(y_ref.dtype)


def rglru_mixer(x, g, w_in, conv_w, conv_b, w_a, b_a, w_i, b_i, lam):
    b, s, d = x.shape
    width = w_in.shape[1] // 2
    const2 = lambda bi, t: (0, 0)
    const3 = lambda bi, t: (0, 0, 0)
    return pl.pallas_call(
        _rglru_kernel,
        out_shape=jax.ShapeDtypeStruct((b, s, width), BF16),
        grid=(b, s // REC_TILE),
        in_specs=[
            pl.BlockSpec((1, REC_TILE, d), lambda bi, t: (bi, t, 0)),
            pl.BlockSpec((1, d), const2),
            pl.BlockSpec(w_in.shape, const2),
            pl.BlockSpec(conv_w.shape, const2),
            pl.BlockSpec((1, width), const2),
            pl.BlockSpec(w_a.shape, const3),
            pl.BlockSpec((1, width), const2),
            pl.BlockSpec(w_i.shape, const3),
            pl.BlockSpec((1, width), const2),
            pl.BlockSpec((1, width), const2),
        ],
        out_specs=pl.BlockSpec((1, REC_TILE, width), lambda bi, t: (bi, t, 0)),
        scratch_shapes=[
            pltpu.VMEM((REC_TILE + 8, width), F32),
            pltpu.VMEM((REC_TILE, width), F32),
            pltpu.VMEM((REC_TILE, width), F32),
            pltpu.VMEM((1, width), F32),
        ],
        compiler_params=_params(("arbitrary", "arbitrary")),
        name="rglru_mixer",
    )(x, g, w_in, conv_w, conv_b, w_a, b_a, w_i, b_i, lam)


def kernel(x, attn_w_in, attn_rel_bias, attn_w_out, rg_w_in, rg_conv_w, rg_conv_b, rg_w_a, rg_b_a,
           rg_w_i, rg_b_i, rg_lambda, rg_w_out, norm_mix_pre, norm_mix_post, norm_ffn_pre,
           norm_ffn_post, ffn_w_gate, ffn_w_up, ffn_w_down):
    b, s, d = x.shape
    depth = norm_mix_pre.shape[0]
    xf = x.reshape(b * s, d)
    row = lambda v: v.reshape(1, -1)
    for layer in range(depth):
        j = layer // 2
        g_pre = row(norm_mix_pre[layer])
        g_post = row(norm_mix_post[layer])
        if layer % 2 == 0:
            proj = norm_linear(xf, g_pre, attn_w_in[j].astype(BF16), BF16).reshape(b, s, -1)
            out_a = band_attention(proj, band_bias(attn_rel_bias[j]), 0)
            out_b = stick_breaking_attention(proj, 3 * A_W)
            w_out = attn_w_out[j].astype(BF16)
            ins = [out_a.reshape(b * s, A_W), out_b.reshape(b * s, B_W)]
            ws = [w_out[:A_W], w_out[A_W:]]
        else:
            width = rg_w_out.shape[1]
            y = rglru_mixer(
                xf.reshape(b, s, d), g_pre, rg_w_in[j].astype(BF16),
                rg_conv_w[j].reshape(-1, width), row(rg_conv_b[j]),
                rg_w_a[j].astype(BF16), row(rg_b_a[j]),
                rg_w_i[j].astype(BF16), row(rg_b_i[j]), row(rg_lambda[j]))
            ins = [y.reshape(b * s, width)]
            ws = [rg_w_out[j].astype(BF16)]
        xf = mixer_out_ffn(
            xf, g_post, ins, ws, row(norm_ffn_pre[layer]), row(norm_ffn_post[layer]),
            ffn_w_gate[layer].astype(BF16), ffn_w_up[layer].astype(BF16),
            ffn_w_down[layer].astype(BF16))
    return xf.reshape(b, s, d)
```

```python
import functools

import jax
import jax.numpy as jnp
from jax import lax
from jax.experimental import pallas as pl
from jax.experimental.pallas import tpu as pltpu

F32 = jnp.float32
BF16 = jnp.bfloat16

D_MODEL = 1024
HEAD_DIM = 64
CHUNK = 64
N_LEFT_CHUNKS = 8
REL_CLIP = 256
A_W = 512
B_W = 512
LRU_BLOCKS = 4
LRU_BLOCK_W = 256
LRU_C = 8.0
RMS_EPS = 1e-6

LANES = 128
VMEM_LIMIT = 56 * 1024 * 1024

ROW_TILE = 512
SB_TILE = 256
SB_SUBTILES = 2
SB_UNROLL = 1
SB_EXIT = 106.0
A_TILE = 2 * CHUNK
A_SUBTILES = 16
A_WIN = (N_LEFT_CHUNKS + 2) * CHUNK
A_PAD = N_LEFT_CHUNKS * CHUNK
REC_TILE = 512
NEG_BIG = -1e30
LOG2E = 1.4426950408889634

_NT = (((1,), (1,)), ((), ()))


def _params(sem):
    return pltpu.CompilerParams(dimension_semantics=sem, vmem_limit_bytes=VMEM_LIMIT)


def _rms(x, g):
    ms = jnp.mean(x * x, axis=-1, keepdims=True)
    return x * lax.rsqrt(ms + RMS_EPS) * g


def _sigmoid(x):
    return 1.0 / (1.0 + jnp.exp(-x))


def _resident(shape):
    return pl.BlockSpec(shape, lambda *_: (0,) * len(shape), pipeline_mode=pl.Buffered(1))


def _layer_block(stack, layer, rows=None, row_block=0):
    shape = stack.shape[1:] if rows is None else (rows,) + stack.shape[2:]
    index = (layer, row_block) + (0,) * (len(shape) - 1)
    return pl.BlockSpec((None,) + shape, lambda *_: index, pipeline_mode=pl.Buffered(1))


def _norm_linear_kernel(x_ref, g_ref, w_ref, o_ref):
    h = _rms(x_ref[...], g_ref[...]).astype(BF16)
    o_ref[...] = jnp.dot(h, w_ref[...], preferred_element_type=F32).astype(o_ref.dtype)


def norm_linear(x, g, w_stack, layer, out_dtype):
    m, d = x.shape
    n = w_stack.shape[2]
    return pl.pallas_call(
        _norm_linear_kernel,
        out_shape=jax.ShapeDtypeStruct((m, n), out_dtype),
        grid=(m // ROW_TILE,),
        in_specs=[
            pl.BlockSpec((ROW_TILE, d), lambda i: (i, 0)),
            pl.BlockSpec((1, d), lambda i: (0, 0)),
            _layer_block(w_stack, layer),
        ],
        out_specs=pl.BlockSpec((ROW_TILE, n), lambda i: (i, 0)),
        compiler_params=_params(("arbitrary",)),
        name="norm_linear",
    )(x, g, w_stack)


def _out_ffn_kernel(n_in, *refs):
    x_ref, gmix_ref, gpre_ref, gpost_ref, wg_ref, wu_ref, wd_ref = refs[:7]
    ins = refs[7:7 + n_in]
    ws = refs[7 + n_in:7 + 2 * n_in]
    o_ref = refs[7 + 2 * n_in]
    m = jnp.dot(ins[0][...], ws[0][...], preferred_element_type=F32)
    for a, w in zip(ins[1:], ws[1:]):
        m = m + jnp.dot(a[...], w[...], preferred_element_type=F32)
    x = x_ref[...] + _rms(m, gmix_ref[...])
    h = _rms(x, gpre_ref[...]).astype(BF16)
    gate = jnp.dot(h, wg_ref[...], preferred_element_type=F32)
    up = jnp.dot(h, wu_ref[...], preferred_element_type=F32)
    act = (gate * _sigmoid(gate) * up).astype(BF16)
    f = jnp.dot(act, wd_ref[...], preferred_element_type=F32)
    o_ref[...] = x + _rms(f, gpost_ref[...])


def mixer_out_ffn(x, gmix, ins, w_out_stack, j, gpre, gpost, wg_stack, wu_stack, wd_stack, layer):
    m, d = x.shape
    n_in = len(ins)
    in_specs = [pl.BlockSpec((ROW_TILE, d), lambda i: (i, 0))]
    in_specs += [_resident(a.shape) for a in (gmix, gpre, gpost)]
    in_specs += [_layer_block(w, layer) for w in (wg_stack, wu_stack, wd_stack)]
    in_specs += [pl.BlockSpec((ROW_TILE, a.shape[1]), lambda i: (i, 0)) for a in ins]
    in_specs += [_layer_block(w_out_stack, j, a.shape[1], k) for k, a in enumerate(ins)]
    return pl.pallas_call(
        functools.partial(_out_ffn_kernel, n_in),
        out_shape=jax.ShapeDtypeStruct((m, d), F32),
        grid=(m // ROW_TILE,),
        in_specs=in_specs,
        out_specs=pl.BlockSpec((ROW_TILE, d), lambda i: (i, 0)),
        compiler_params=_params(("arbitrary",)),
        name="mixer_out_ffn",
    )(x, gmix, gpre, gpost, wg_stack, wu_stack, wd_stack, *ins, *([w_out_stack] * n_in))


def _band_attn_kernel(q_ref, k_ref, v_ref, bias_ref, o_ref, kp_ref, vp_ref, s_ref, p_ref, rinv_ref):
    t = pl.program_id(2)

    @pl.when(t == 0)
    def _():
        kp_ref[0:A_PAD, :] = jnp.zeros((A_PAD, LANES), BF16)
        vp_ref[0:A_PAD, :] = jnp.zeros((A_PAD, LANES), BF16)
        kp_ref[A_PAD:, :] = k_ref[0]
        vp_ref[A_PAD:, :] = v_ref[0]

    lane = lax.broadcasted_iota(jnp.int32, (1, LANES), 1)
    first = lane < HEAD_DIM
    col = lax.broadcasted_iota(jnp.int32, (1, A_WIN), 1)
    bias = jnp.concatenate([bias_ref[0], bias_ref[1]], axis=0)
    zero = jnp.zeros((A_TILE, LANES), BF16)

    def window(u):
        tile = t * A_SUBTILES + u
        return tile, pl.multiple_of(tile * A_TILE, A_TILE)

    def scores(u, slot):
        tile, start = window(u)
        kw = kp_ref[pl.ds(start, A_WIN), :]
        q2 = q_ref[0, pl.ds(pl.multiple_of(u * A_TILE, A_TILE), A_TILE), :] * (HEAD_DIM ** -0.5)
        qs = jnp.concatenate([jnp.where(first, q2, zero), jnp.where(first, zero, q2)], axis=0)
        s = lax.dot_general(qs, kw, _NT, preferred_element_type=F32) + bias
        in_seq = col >= (A_PAD - tile * A_TILE)
        s_ref[slot] = jnp.where(in_seq, s, NEG_BIG)

    def softmax(slot):
        s = s_ref[slot]
        p = jnp.exp(s - jnp.max(s, axis=-1, keepdims=True))
        p_ref[slot] = p.astype(BF16)
        rinv_ref[slot] = jnp.broadcast_to(1.0 / jnp.sum(p, axis=-1, keepdims=True), (2 * A_TILE, LANES))

    def values(u, slot):
        _, start = window(u)
        vw = vp_ref[pl.ds(start, A_WIN), :]
        o = jnp.dot(p_ref[slot], vw, preferred_element_type=F32) * rinv_ref[slot]
        out = jnp.where(first, o[:A_TILE], o[A_TILE:])
        o_ref[0, pl.ds(pl.multiple_of(u * A_TILE, A_TILE), A_TILE), :] = out.astype(o_ref.dtype)

    scores(0, 0)
    scores(1, 1)
    softmax(0)

    def body(k, _):
        values(2 * k - 2, 0)
        softmax(1)
        scores(2 * k, 0)
        values(2 * k - 1, 1)
        softmax(0)
        scores(2 * k + 1, 1)
        return 0

    lax.fori_loop(1, A_SUBTILES // 2, body, 0)
    values(A_SUBTILES - 2, 0)
    softmax(1)
    values(A_SUBTILES - 1, 1)


def band_attention(proj, bias, col0):
    b, s, _ = proj.shape
    nq = col0 // LANES
    nk = (col0 + A_W) // LANES
    nv = (col0 + 2 * A_W) // LANES
    tq = A_SUBTILES * A_TILE
    return pl.pallas_call(
        _band_attn_kernel,
        out_shape=jax.ShapeDtypeStruct((b, s, A_W), BF16),
        grid=(b, A_W // LANES, s // tq),
        in_specs=[
            pl.BlockSpec((1, tq, LANES), lambda bi, p, t: (bi, t, nq + p)),
            pl.BlockSpec((1, s, LANES), lambda bi, p, t: (bi, 0, nk + p)),
            pl.BlockSpec((1, s, LANES), lambda bi, p, t: (bi, 0, nv + p)),
            pl.BlockSpec((2, A_TILE, A_WIN), lambda bi, p, t: (p, 0, 0)),
        ],
        out_specs=pl.BlockSpec((1, tq, LANES), lambda bi, p, t: (bi, t, p)),
        scratch_shapes=[
            pltpu.VMEM((s + A_PAD, LANES), BF16),
            pltpu.VMEM((s + A_PAD, LANES), BF16),
            pltpu.VMEM((2, 2 * A_TILE, A_WIN), F32),
            pltpu.VMEM((2, 2 * A_TILE, A_WIN), BF16),
            pltpu.VMEM((2, 2 * A_TILE, LANES), F32),
        ],
        compiler_params=_params(("arbitrary", "arbitrary", "arbitrary")),
        name="band_attention",
    )(proj, proj, proj, bias)


def band_bias(rel_bias):
    h = rel_bias.shape[0]
    n_f = A_WIN + A_TILE - 1
    n_const = A_PAD + A_TILE - 1 - REL_CLIP
    far = jnp.broadcast_to(rel_bias[:, 2 * REL_CLIP:], (h, n_const))
    near = rel_bias[:, 2 * REL_CLIP + 1 - (n_f - n_const):][:, ::-1]
    f = jnp.concatenate([far, near], axis=1)
    period = n_f + 1
    g = jnp.concatenate([f[:, A_TILE - 1:], jnp.zeros((h, 1), F32), f[:, :A_TILE - 1]], axis=1)
    flat = jnp.tile(g, (1, A_TILE))[:, :A_TILE * (period - 1)]
    toep = flat.reshape(h, A_TILE, period - 1)[:, :, :A_WIN]
    r = jnp.arange(A_TILE)[:, None]
    c = jnp.arange(A_WIN)[None, :]
    dchunk = c // CHUNK - r // CHUNK
    ok = (dchunk >= 0) & (dchunk <= N_LEFT_CHUNKS)
    return jnp.where(ok[None], toep, NEG_BIG).astype(F32)


def _stick_kernel(q_ref, k_ref, v_ref, o_ref, acc_ref, carry_ref, z_ref, sp_ref, kmax_ref):
    qi = pl.program_id(2)
    t = SB_TILE
    ns = SB_SUBTILES
    lane = lax.broadcasted_iota(jnp.int32, (1, LANES), 1)
    first = lane < HEAD_DIM
    q2 = q_ref[0] * (HEAD_DIM ** -0.5)
    row = lax.broadcasted_iota(jnp.int32, (t, t), 0)
    col = lax.broadcasted_iota(jnp.int32, (t, t), 1)
    causal = col < row
    true = jnp.ones((t, t), jnp.bool_)
    suffix = jnp.where(row >= col, 1.0, 0.0).astype(BF16)
    zero = jnp.zeros((t, LANES), BF16)
    acc_ref[...] = jnp.zeros_like(acc_ref)
    carry_ref[...] = jnp.zeros_like(carry_ref)

    def split_heads(x):
        return jnp.concatenate([jnp.where(first, x, zero), jnp.where(first, zero, x)], axis=0)

    def scores(j, d, mask):
        start = pl.multiple_of(j * t, t)
        kcat = split_heads(k_ref[0, pl.ds(start, t), :])
        z = lax.dot_general(q2[d * t:], kcat, _NT, preferred_element_type=F32)
        sp = jnp.maximum(z, 0.0) + jnp.log(1.0 + jnp.exp2(jnp.abs(z) * (-LOG2E)))
        if mask is not None:
            sp = jnp.where(mask, sp, 0.0)
        return z, sp.astype(BF16)

    def weights(j, d, mask, z, spb):
        r0 = d * t
        n = (ns - d) * t
        start = pl.multiple_of(j * t, t)
        vcat = split_heads(v_ref[0, pl.ds(start, t), :])
        c = [jnp.dot(spb[:, h * t:(h + 1) * t], suffix, preferred_element_type=F32) for h in range(2)]
        carry = [carry_ref[h, r0:, :] for h in range(2)]
        reps = t // LANES
        shift = jnp.concatenate([c[0]] + [c[1]], axis=1) + jnp.concatenate(
            [carry[0]] * reps + [carry[1]] * reps, axis=1)
        w = jnp.exp(z - shift)
        if mask is not None:
            w = jnp.where(mask, w, 0.0)
        for h in range(2):
            carry_ref[h, r0:, :] = carry[h] + jnp.broadcast_to(c[h][:, 0:1], (n, LANES))
        acc_ref[r0:, :] += jnp.dot(w.astype(BF16), vcat, preferred_element_type=F32)

    for d in range(ns - 1, -1, -1):
        mask = jnp.concatenate([causal] + [true] * (ns - d - 1), axis=0)
        mask = jnp.concatenate([mask, mask], axis=1)
        z, spb = scores(ns * qi + d, d, mask)
        weights(ns * qi + d, d, mask, z, spb)

    def stage1(i, slot):
        z, spb = scores(ns * qi - 1 - i, 0, None)
        z_ref[slot] = z
        sp_ref[slot] = spb

    def stage2(i, slot):
        weights(ns * qi - 1 - i, 0, None, z_ref[slot], sp_ref[slot])

    @pl.when(qi == 0)
    def _():
        kf = k_ref[0].astype(F32)
        kn2 = jnp.sum(kf * kf, axis=-1, keepdims=True)
        kmax_ref[...] = jnp.broadcast_to(jnp.max(kn2, axis=0, keepdims=True), kmax_ref.shape)

    @pl.when(qi > 0)
    def _():
        qf = q2.astype(F32)
        qmax2 = jnp.max(jnp.sum(qf * qf, axis=-1, keepdims=True), axis=0, keepdims=True)
        zmax = jnp.sqrt(qmax2 * kmax_ref[0:1, 0:1])
        limit = jnp.max(SB_EXIT + (2.0 ** -8) * zmax)
        n_groups = (ns * qi) // SB_UNROLL

        def cond(state):
            g, done = state
            return jnp.logical_and(g < n_groups, done == 0)

        def body(state):
            g, _ = state
            i0 = SB_UNROLL * g
            stage1(i0, 0)
            for off in range(SB_UNROLL - 1):
                stage1(i0 + off + 1, (off + 1) % 2)
                stage2(i0 + off, off % 2)
            stage2(i0 + SB_UNROLL - 1, (SB_UNROLL - 1) % 2)
            low = jnp.min(jnp.minimum(carry_ref[0], carry_ref[1]), axis=0, keepdims=True)
            return g + 1, (jnp.min(low) >= limit).astype(jnp.int32)

        lax.while_loop(cond, body, (jnp.int32(0), jnp.int32(0)))

    o_ref[0] = acc_ref[...].astype(o_ref.dtype)


def stick_breaking_attention(proj, col0):
    b, s, _ = proj.shape
    nq = col0 // LANES
    nk = (col0 + B_W) // LANES
    nv = (col0 + 2 * B_W) // LANES
    tq = SB_SUBTILES * SB_TILE
    return pl.pallas_call(
        _stick_kernel,
        out_shape=jax.ShapeDtypeStruct((b, s, B_W), BF16),
        grid=(b, B_W // LANES, s // tq),
        in_specs=[
            pl.BlockSpec((1, tq, LANES), lambda bi, p, t: (bi, t, nq + p)),
            pl.BlockSpec((1, s, LANES), lambda bi, p, t: (bi, 0, nk + p)),
            pl.BlockSpec((1, s, LANES), lambda bi, p, t: (bi, 0, nv + p)),
        ],
        out_specs=pl.BlockSpec((1, tq, LANES), lambda bi, p, t: (bi, t, p)),
        scratch_shapes=[
            pltpu.VMEM((tq, LANES), F32),
            pltpu.VMEM((2, tq, LANES), F32),
            pltpu.VMEM((2, tq, 2 * SB_TILE), F32),
            pltpu.VMEM((2, tq, 2 * SB_TILE), BF16),
            pltpu.VMEM((8, LANES), F32),
        ],
        compiler_params=_params(("arbitrary", "arbitrary", "arbitrary")),
        name="stick_breaking",
    )(proj, proj, proj)


def _rglru_kernel(x_ref, g_ref, win_ref, cw_ref, cb_ref, wa_ref, ba_ref, wi_ref, bi_ref, lam_ref,
                  y_ref, xpad_ref, a_ref, u_ref, h_ref):
    t = pl.program_id(1)
    n = REC_TILE
    width = y_ref.shape[2]

    @pl.when(t == 0)
    def _():
        xpad_ref[0:8, :] = jnp.zeros((8, width), F32)
        h_ref[...] = jnp.zeros_like(h_ref)

    hn = _rms(x_ref[0], g_ref[...]).astype(BF16)
    gate = jnp.dot(hn, win_ref[:, :width], preferred_element_type=F32)
    c = 0.7978845608028654
    gate = 0.5 * gate * (1.0 + jnp.tanh(c * (gate + 0.044715 * (gate * gate * gate))))
    xpad_ref[8:8 + n, :] = jnp.dot(hn, win_ref[:, width:], preferred_element_type=F32)
    cw = cw_ref[...]
    xc = (cb_ref[...]
          + cw[3:4] * xpad_ref[8:8 + n, :]
          + cw[2:3] * xpad_ref[7:7 + n, :]
          + cw[1:2] * xpad_ref[6:6 + n, :]
          + cw[0:1] * xpad_ref[5:5 + n, :])
    xpad_ref[0:8, :] = xpad_ref[n:n + 8, :]
    xcb = xc.astype(BF16)
    lam = lam_ref[...]
    sp_lam = jnp.maximum(-lam, 0.0) + jnp.log1p(jnp.exp(-jnp.abs(lam)))
    for blk in range(LRU_BLOCKS):
        sl = slice(blk * LRU_BLOCK_W, (blk + 1) * LRU_BLOCK_W)
        xb = xcb[:, sl]
        pre_r = jnp.dot(xb, wa_ref[blk], preferred_element_type=F32) + ba_ref[:, sl]
        pre_i = jnp.dot(xb, wi_ref[blk], preferred_element_type=F32) + bi_ref[:, sl]
        r = 0.5 * jnp.tanh(0.5 * pre_r) + 0.5
        i = 0.5 * jnp.tanh(0.5 * pre_i) + 0.5
        log_a = (-LRU_C * r) * sp_lam[:, sl]
        a = jnp.exp(log_a)
        m2 = -jnp.tanh(log_a) * (a * a + 1.0)
        mult = m2 * lax.rsqrt(jnp.maximum(m2, 1e-30))
        a_ref[:, sl] = a
        u_ref[:, sl] = mult * (i * xc[:, sl])

    rowid = lax.broadcasted_iota(jnp.int32, (8, width), 0)

    def group(g, hprev):
        r0 = pl.multiple_of(g * 8, 8)
        a = a_ref[pl.ds(r0, 8), :]
        b = u_ref[pl.ds(r0, 8), :]
        for d in (1, 2, 4):
            keep = rowid >= d
            a_sh = jnp.where(keep, pltpu.roll(a, d, 0), 1.0)
            b_sh = jnp.where(keep, pltpu.roll(b, d, 0), 0.0)
            b = a * b_sh + b
            a = a * a_sh
        out = b + a * hprev
        u_ref[pl.ds(r0, 8), :] = out
        return out[7:8, :]

    h_ref[...] = lax.fori_loop(0, n // 8, group, h_ref[...], unroll=4)
    y_ref[0] = (u_ref[...] * gate).astype(y_ref.dtype)


def rglru_mixer(x, g, w_in_stack, conv_w, conv_b, w_a_stack, b_a, w_i_stack, b_i, lam, j):
    b, s, d = x.shape
    width = w_in_stack.shape[2] // 2
    const2 = lambda bi, t: (0, 0)
    return pl.pallas_call(
        _rglru_kernel,
        out_shape=jax.ShapeDtypeStruct((b, s, width), BF16),
        grid=(b, s // REC_TILE),
        in_specs=[
            pl.BlockSpec((1, REC_TILE, d), lambda bi, t: (bi, t, 0)),
            pl.BlockSpec((1, d), const2),
            _layer_block(w_in_stack, j),
            pl.BlockSpec(conv_w.shape, const2),
            pl.BlockSpec((1, width), const2),
            _layer_block(w_a_stack, j),
            pl.BlockSpec((1, width), const2),
            _layer_block(w_i_stack, j),
            pl.BlockSpec((1, width), const2),
            pl.BlockSpec((1, width), const2),
        ],
        out_specs=pl.BlockSpec((1, REC_TILE, width), lambda bi, t: (bi, t, 0)),
        scratch_shapes=[
            pltpu.VMEM((REC_TILE + 8, width), F32),
            pltpu.VMEM((REC_TILE, width), F32),
            pltpu.VMEM((REC_TILE, width), F32),
            pltpu.VMEM((1, width), F32),
        ],
        compiler_params=_params(("arbitrary", "arbitrary")),
        name="rglru_mixer",
    )(x, g, w_in_stack, conv_w, conv_b, w_a_stack, b_a, w_i_stack, b_i, lam)


def kernel(x, attn_w_in, attn_rel_bias, attn_w_out, rg_w_in, rg_conv_w, rg_conv_b, rg_w_a, rg_b_a,
           rg_w_i, rg_b_i, rg_lambda, rg_w_out, norm_mix_pre, norm_mix_post, norm_ffn_pre,
           norm_ffn_post, ffn_w_gate, ffn_w_up, ffn_w_down):
    b, s, d = x.shape
    depth = norm_mix_pre.shape[0]
    xf = x.reshape(b * s, d)
    row = lambda v: v.reshape(1, -1)
    attn_w_in, attn_w_out, rg_w_in, rg_w_a, rg_w_i, rg_w_out, ffn_w_gate, ffn_w_up, ffn_w_down = (
        w.astype(BF16) for w in (attn_w_in, attn_w_out, rg_w_in, rg_w_a, rg_w_i, rg_w_out,
                                 ffn_w_gate, ffn_w_up, ffn_w_down))
    for layer in range(depth):
        j = layer // 2
        g_pre = row(norm_mix_pre[layer])
        if layer % 2 == 0:
            proj = norm_linear(xf, g_pre, attn_w_in, j, BF16).reshape(b, s, -1)
            out_a = band_attention(proj, band_bias(attn_rel_bias[j]), 0)
            out_b = stick_breaking_attention(proj, 3 * A_W)
            ins = [out_a.reshape(b * s, A_W), out_b.reshape(b * s, B_W)]
            w_out = attn_w_out
        else:
            width = rg_w_out.shape[1]
            y = rglru_mixer(
                xf.reshape(b, s, d), g_pre, rg_w_in,
                rg_conv_w[j].reshape(-1, width), row(rg_conv_b[j]),
                rg_w_a, row(rg_b_a[j]), rg_w_i, row(rg_b_i[j]), row(rg_lambda[j]), j)
            ins = [y.reshape(b * s, width)]
            w_out = rg_w_out
        xf = mixer_out_ffn(
            xf, row(norm_mix_post[layer]), ins, w_out, j,
            row(norm_ffn_pre[layer]), row(norm_ffn_post[layer]),
            ffn_w_gate, ffn_w_up, ffn_w_down, layer)
    return xf.reshape(b, s, d)
```

```python
import functools

import jax
import jax.numpy as jnp
from jax import lax
from jax.experimental import pallas as pl
from jax.experimental.pallas import tpu as pltpu

F32 = jnp.float32
BF16 = jnp.bfloat16

D_MODEL = 1024
HEAD_DIM = 64
CHUNK = 64
N_LEFT_CHUNKS = 8
REL_CLIP = 256
A_W = 512
B_W = 512
LRU_BLOCKS = 4
LRU_BLOCK_W = 256
LRU_C = 8.0
RMS_EPS = 1e-6

LANES = 128
VMEM_LIMIT = 56 * 1024 * 1024

ROW_TILE = 512
SB_TILE = 256
SB_SUBTILES = 2
SB_UNROLL = 1
SB_EXIT = 106.0
A_TILE = 2 * CHUNK
A_SUBTILES = 32
A_WIN = (N_LEFT_CHUNKS + 2) * CHUNK
A_PAD = N_LEFT_CHUNKS * CHUNK
REC_TILE = 512
NEG_BIG = -1e30
LOG2E = 1.4426950408889634

_NT = (((1,), (1,)), ((), ()))


def _params(sem):
    return pltpu.CompilerParams(dimension_semantics=sem, vmem_limit_bytes=VMEM_LIMIT)


def _rms(x, g):
    ms = jnp.mean(x * x, axis=-1, keepdims=True)
    return x * lax.rsqrt(ms + RMS_EPS) * g


def _sigmoid(x):
    return 1.0 / (1.0 + jnp.exp(-x))


def _resident(shape):
    return pl.BlockSpec(shape, lambda *_: (0,) * len(shape), pipeline_mode=pl.Buffered(1))


def _layer_block(stack, layer, rows=None, row_block=0):
    shape = stack.shape[1:] if rows is None else (rows,) + stack.shape[2:]
    index = (layer, row_block) + (0,) * (len(shape) - 1)
    return pl.BlockSpec((None,) + shape, lambda *_: index, pipeline_mode=pl.Buffered(1))


def _norm_linear_kernel(x_ref, g_ref, w_ref, o_ref):
    h = _rms(x_ref[...], g_ref[...]).astype(BF16)
    o_ref[...] = jnp.dot(h, w_ref[...], preferred_element_type=F32).astype(o_ref.dtype)


def norm_linear(x, g, w_stack, layer, out_dtype):
    m, d = x.shape
    n = w_stack.shape[2]
    return pl.pallas_call(
        _norm_linear_kernel,
        out_shape=jax.ShapeDtypeStruct((m, n), out_dtype),
        grid=(m // ROW_TILE,),
        in_specs=[
            pl.BlockSpec((ROW_TILE, d), lambda i: (i, 0)),
            pl.BlockSpec((1, d), lambda i: (0, 0)),
            _layer_block(w_stack, layer),
        ],
        out_specs=pl.BlockSpec((ROW_TILE, n), lambda i: (i, 0)),
        compiler_params=_params(("arbitrary",)),
        name="norm_linear",
    )(x, g, w_stack)


def _out_ffn_kernel(n_in, *refs):
    x_ref, gmix_ref, gpre_ref, gpost_ref, wg_ref, wu_ref, wd_ref = refs[:7]
    ins = refs[7:7 + n_in]
    ws = refs[7 + n_in:7 + 2 * n_in]
    o_ref = refs[7 + 2 * n_in]
    m = jnp.dot(ins[0][...], ws[0][...], preferred_element_type=F32)
    for a, w in zip(ins[1:], ws[1:]):
        m = m + jnp.dot(a[...], w[...], preferred_element_type=F32)
    x = x_ref[...] + _rms(m, gmix_ref[...])
    h = _rms(x, gpre_ref[...]).astype(BF16)
    gate = jnp.dot(h, wg_ref[...], preferred_element_type=F32)
    up = jnp.dot(h, wu_ref[...], preferred_element_type=F32)
    act = (gate * _sigmoid(gate) * up).astype(BF16)
    f = jnp.dot(act, wd_ref[...], preferred_element_type=F32)
    o_ref[...] = x + _rms(f, gpost_ref[...])


def mixer_out_ffn(x, gmix, ins, w_out_stack, j, gpre, gpost, wg_stack, wu_stack, wd_stack, layer):
    m, d = x.shape
    n_in = len(ins)
    in_specs = [pl.BlockSpec((ROW_TILE, d), lambda i: (i, 0))]
    in_specs += [_resident(a.shape) for a in (gmix, gpre, gpost)]
    in_specs += [_layer_block(w, layer) for w in (wg_stack, wu_stack, wd_stack)]
    in_specs += [pl.BlockSpec((ROW_TILE, a.shape[1]), lambda i: (i, 0)) for a in ins]
    in_specs += [_layer_block(w_out_stack, j, a.shape[1], k) for k, a in enumerate(ins)]
    return pl.pallas_call(
        functools.partial(_out_ffn_kernel, n_in),
        out_shape=jax.ShapeDtypeStruct((m, d), F32),
        grid=(m // ROW_TILE,),
        in_specs=in_specs,
        out_specs=pl.BlockSpec((ROW_TILE, d), lambda i: (i, 0)),
        compiler_params=_params(("arbitrary",)),
        name="mixer_out_ffn",
    )(x, gmix, gpre, gpost, wg_stack, wu_stack, wd_stack, *ins, *([w_out_stack] * n_in))


def _band_attn_kernel(q_ref, k_ref, v_ref, bias_ref, o_ref, kp_ref, vp_ref, s_ref, p_ref, rinv_ref):
    t = pl.program_id(2)

    @pl.when(t == 0)
    def _():
        kp_ref[0:A_PAD, :] = jnp.zeros((A_PAD, LANES), BF16)
        vp_ref[0:A_PAD, :] = jnp.zeros((A_PAD, LANES), BF16)
        kp_ref[A_PAD:, :] = k_ref[0]
        vp_ref[A_PAD:, :] = v_ref[0]

    lane = lax.broadcasted_iota(jnp.int32, (1, LANES), 1)
    first = lane < HEAD_DIM
    col = lax.broadcasted_iota(jnp.int32, (1, A_WIN), 1)
    bias = jnp.concatenate([bias_ref[0], bias_ref[1]], axis=0)
    zero = jnp.zeros((A_TILE, LANES), BF16)

    def window(u):
        tile = t * A_SUBTILES + u
        return tile, pl.multiple_of(tile * A_TILE, A_TILE)

    def scores(u, slot, head_of_sequence):
        tile, start = window(u)
        kw = kp_ref[pl.ds(start, A_WIN), :]
        q2 = q_ref[0, pl.ds(pl.multiple_of(u * A_TILE, A_TILE), A_TILE), :] * (HEAD_DIM ** -0.5)
        qs = jnp.concatenate([jnp.where(first, q2, zero), jnp.where(first, zero, q2)], axis=0)
        s = lax.dot_general(qs, kw, _NT, preferred_element_type=F32) + bias
        if head_of_sequence:
            s = jnp.where(col >= (A_PAD - tile * A_TILE), s, NEG_BIG)
        s_ref[slot] = s

    def softmax(slot):
        s = s_ref[slot]
        p = jnp.exp(s - jnp.max(s, axis=-1, keepdims=True))
        p_ref[slot] = p.astype(BF16)
        rinv_ref[slot] = jnp.broadcast_to(1.0 / jnp.sum(p, axis=-1, keepdims=True), (2 * A_TILE, LANES))

    def values(u, slot):
        _, start = window(u)
        vw = vp_ref[pl.ds(start, A_WIN), :]
        o = jnp.dot(p_ref[slot], vw, preferred_element_type=F32) * rinv_ref[slot]
        out = jnp.where(first, o[:A_TILE], o[A_TILE:])
        o_ref[0, pl.ds(pl.multiple_of(u * A_TILE, A_TILE), A_TILE), :] = out.astype(o_ref.dtype)

    scores(0, 0, True)
    scores(1, 1, True)
    softmax(0)

    def pair(k, head_of_sequence):
        values(2 * k - 2, 0)
        softmax(1)
        scores(2 * k, 0, head_of_sequence)
        values(2 * k - 1, 1)
        softmax(0)
        scores(2 * k + 1, 1, head_of_sequence)

    n_head = A_PAD // A_TILE // 2
    for k in range(1, n_head):
        pair(k, True)

    def body(k, _):
        pair(k, False)
        return 0

    lax.fori_loop(n_head, A_SUBTILES // 2, body, 0)
    values(A_SUBTILES - 2, 0)
    softmax(1)
    values(A_SUBTILES - 1, 1)


def band_attention(proj, bias, col0):
    b, s, _ = proj.shape
    nq = col0 // LANES
    nk = (col0 + A_W) // LANES
    nv = (col0 + 2 * A_W) // LANES
    tq = A_SUBTILES * A_TILE
    return pl.pallas_call(
        _band_attn_kernel,
        out_shape=jax.ShapeDtypeStruct((b, s, A_W), BF16),
        grid=(b, A_W // LANES, s // tq),
        in_specs=[
            pl.BlockSpec((1, tq, LANES), lambda bi, p, t: (bi, t, nq + p)),
            pl.BlockSpec((1, s, LANES), lambda bi, p, t: (bi, 0, nk + p)),
            pl.BlockSpec((1, s, LANES), lambda bi, p, t: (bi, 0, nv + p)),
            pl.BlockSpec((2, A_TILE, A_WIN), lambda bi, p, t: (p, 0, 0)),
        ],
        out_specs=pl.BlockSpec((1, tq, LANES), lambda bi, p, t: (bi, t, p)),
        scratch_shapes=[
            pltpu.VMEM((s + A_PAD, LANES), BF16),
            pltpu.VMEM((s + A_PAD, LANES), BF16),
            pltpu.VMEM((2, 2 * A_TILE, A_WIN), F32),
            pltpu.VMEM((2, 2 * A_TILE, A_WIN), BF16),
            pltpu.VMEM((2, 2 * A_TILE, LANES), F32),
        ],
        compiler_params=_params(("arbitrary", "arbitrary", "arbitrary")),
        name="band_attention",
    )(proj, proj, proj, bias)


def band_bias(rel_bias):
    h = rel_bias.shape[0]
    n_f = A_WIN + A_TILE - 1
    n_const = A_PAD + A_TILE - 1 - REL_CLIP
    far = jnp.broadcast_to(rel_bias[:, 2 * REL_CLIP:], (h, n_const))
    near = rel_bias[:, 2 * REL_CLIP + 1 - (n_f - n_const):][:, ::-1]
    f = jnp.concatenate([far, near], axis=1)
    period = n_f + 1
    g = jnp.concatenate([f[:, A_TILE - 1:], jnp.zeros((h, 1), F32), f[:, :A_TILE - 1]], axis=1)
    flat = jnp.tile(g, (1, A_TILE))[:, :A_TILE * (period - 1)]
    toep = flat.reshape(h, A_TILE, period - 1)[:, :, :A_WIN]
    r = jnp.arange(A_TILE)[:, None]
    c = jnp.arange(A_WIN)[None, :]
    dchunk = c // CHUNK - r // CHUNK
    ok = (dchunk >= 0) & (dchunk <= N_LEFT_CHUNKS)
    return jnp.where(ok[None], toep, NEG_BIG).astype(F32)


def _stick_kernel(q_ref, k_ref, v_ref, o_ref, acc_ref, carry_ref, z_ref, sp_ref, kmax_ref):
    qi = pl.program_id(2)
    t = SB_TILE
    ns = SB_SUBTILES
    lane = lax.broadcasted_iota(jnp.int32, (1, LANES), 1)
    first = lane < HEAD_DIM
    q2 = q_ref[0] * (HEAD_DIM ** -0.5)
    row = lax.broadcasted_iota(jnp.int32, (t, t), 0)
    col = lax.broadcasted_iota(jnp.int32, (t, t), 1)
    causal = col < row
    true = jnp.ones((t, t), jnp.bool_)
    suffix = jnp.where(row >= col, 1.0, 0.0).astype(BF16)
    zero = jnp.zeros((t, LANES), BF16)
    acc_ref[...] = jnp.zeros_like(acc_ref)
    carry_ref[...] = jnp.zeros_like(carry_ref)

    def split_heads(x):
        return jnp.concatenate([jnp.where(first, x, zero), jnp.where(first, zero, x)], axis=0)

    def scores(j, d, mask):
        start = pl.multiple_of(j * t, t)
        kcat = split_heads(k_ref[0, pl.ds(start, t), :])
        z = lax.dot_general(q2[d * t:], kcat, _NT, preferred_element_type=F32)
        sp = jnp.maximum(z, 0.0) + jnp.log(1.0 + jnp.exp2(jnp.abs(z) * (-LOG2E)))
        if mask is not None:
            sp = jnp.where(mask, sp, 0.0)
        return z, sp.astype(BF16)

    def weights(j, d, mask, z, spb):
        r0 = d * t
        n = (ns - d) * t
        start = pl.multiple_of(j * t, t)
        vcat = split_heads(v_ref[0, pl.ds(start, t), :])
        c = [jnp.dot(spb[:, h * t:(h + 1) * t], suffix, preferred_element_type=F32) for h in range(2)]
        carry = [carry_ref[h, r0:, :] for h in range(2)]
        reps = t // LANES
        shift = jnp.concatenate([c[0]] + [c[1]], axis=1) + jnp.concatenate(
            [carry[0]] * reps + [carry[1]] * reps, axis=1)
        w = jnp.exp(z - shift)
        if mask is not None:
            w = jnp.where(mask, w, 0.0)
        for h in range(2):
            carry_ref[h, r0:, :] = carry[h] + jnp.broadcast_to(c[h][:, 0:1], (n, LANES))
        acc_ref[r0:, :] += jnp.dot(w.astype(BF16), vcat, preferred_element_type=F32)

    for d in range(ns - 1, -1, -1):
        mask = jnp.concatenate([causal] + [true] * (ns - d - 1), axis=0)
        mask = jnp.concatenate([mask, mask], axis=1)
        z, spb = scores(ns * qi + d, d, mask)
        weights(ns * qi + d, d, mask, z, spb)

    def stage1(i, slot):
        z, spb = scores(ns * qi - 1 - i, 0, None)
        z_ref[slot] = z
        sp_ref[slot] = spb

    def stage2(i, slot):
        weights(ns * qi - 1 - i, 0, None, z_ref[slot], sp_ref[slot])

    @pl.when(qi == 0)
    def _():
        kf = k_ref[0].astype(F32)
        kn2 = jnp.sum(kf * kf, axis=-1, keepdims=True)
        kmax_ref[...] = jnp.broadcast_to(jnp.max(kn2, axis=0, keepdims=True), kmax_ref.shape)

    @pl.when(qi > 0)
    def _():
        qf = q2.astype(F32)
        qmax2 = jnp.max(jnp.sum(qf * qf, axis=-1, keepdims=True), axis=0, keepdims=True)
        zmax = jnp.sqrt(qmax2 * kmax_ref[0:1, 0:1])
        limit = jnp.max(SB_EXIT + (2.0 ** -8) * zmax)
        n_groups = (ns * qi) // SB_UNROLL

        def cond(state):
            g, done = state
            return jnp.logical_and(g < n_groups, done == 0)

        def body(state):
            g, _ = state
            i0 = SB_UNROLL * g
            stage1(i0, 0)
            for off in range(SB_UNROLL - 1):
                stage1(i0 + off + 1, (off + 1) % 2)
                stage2(i0 + off, off % 2)
            stage2(i0 + SB_UNROLL - 1, (SB_UNROLL - 1) % 2)
            low = jnp.min(jnp.minimum(carry_ref[0], carry_ref[1]), axis=0, keepdims=True)
            return g + 1, (jnp.min(low) >= limit).astype(jnp.int32)

        lax.while_loop(cond, body, (jnp.int32(0), jnp.int32(0)))

    o_ref[0] = acc_ref[...].astype(o_ref.dtype)


def stick_breaking_attention(proj, col0):
    b, s, _ = proj.shape
    nq = col0 // LANES
    nk = (col0 + B_W) // LANES
    nv = (col0 + 2 * B_W) // LANES
    tq = SB_SUBTILES * SB_TILE
    return pl.pallas_call(
        _stick_kernel,
        out_shape=jax.ShapeDtypeStruct((b, s, B_W), BF16),
        grid=(b, B_W // LANES, s // tq),
        in_specs=[
            pl.BlockSpec((1, tq, LANES), lambda bi, p, t: (bi, t, nq + p)),
            pl.BlockSpec((1, s, LANES), lambda bi, p, t: (bi, 0, nk + p)),
            pl.BlockSpec((1, s, LANES), lambda bi, p, t: (bi, 0, nv + p)),
        ],
        out_specs=pl.BlockSpec((1, tq, LANES), lambda bi, p, t: (bi, t, p)),
        scratch_shapes=[
            pltpu.VMEM((tq, LANES), F32),
            pltpu.VMEM((2, tq, LANES), F32),
            pltpu.VMEM((2, tq, 2 * SB_TILE), F32),
            pltpu.VMEM((2, tq, 2 * SB_TILE), BF16),
            pltpu.VMEM((8, LANES), F32),
        ],
        compiler_params=_params(("arbitrary", "arbitrary", "arbitrary")),
        name="stick_breaking",
    )(proj, proj, proj)


def _rglru_kernel(x_ref, g_ref, win_ref, cw_ref, cb_ref, wa_ref, ba_ref, wi_ref, bi_ref, lam_ref,
                  y_ref, xpad_ref, a_ref, u_ref, h_ref):
    t = pl.program_id(1)
    n = REC_TILE
    width = y_ref.shape[2]

    @pl.when(t == 0)
    def _():
        xpad_ref[0:8, :] = jnp.zeros((8, width), F32)
        h_ref[...] = jnp.zeros_like(h_ref)

    hn = _rms(x_ref[0], g_ref[...]).astype(BF16)
    gate = jnp.dot(hn, win_ref[:, :width], preferred_element_type=F32)
    c = 0.7978845608028654
    gate = 0.5 * gate * (1.0 + jnp.tanh(c * (gate + 0.044715 * (gate * gate * gate))))
    xpad_ref[8:8 + n, :] = jnp.dot(hn, win_ref[:, width:], preferred_element_type=F32)
    cw = cw_ref[...]
    xc = (cb_ref[...]
          + cw[3:4] * xpad_ref[8:8 + n, :]
          + cw[2:3] * xpad_ref[7:7 + n, :]
          + cw[1:2] * xpad_ref[6:6 + n, :]
          + cw[0:1] * xpad_ref[5:5 + n, :])
    xpad_ref[0:8, :] = xpad_ref[n:n + 8, :]
    xcb = xc.astype(BF16)
    lam = lam_ref[...]
    sp_lam = jnp.maximum(-lam, 0.0) + jnp.log1p(jnp.exp(-jnp.abs(lam)))
    for blk in range(LRU_BLOCKS):
        sl = slice(blk * LRU_BLOCK_W, (blk + 1) * LRU_BLOCK_W)
        xb = xcb[:, sl]
        pre_r = jnp.dot(xb, wa_ref[blk], preferred_element_type=F32) + ba_ref[:, sl]
        pre_i = jnp.dot(xb, wi_ref[blk], preferred_element_type=F32) + bi_ref[:, sl]
        r = 0.5 * jnp.tanh(0.5 * pre_r) + 0.5
        i = 0.5 * jnp.tanh(0.5 * pre_i) + 0.5
        log_a = (-LRU_C * r) * sp_lam[:, sl]
        a = jnp.exp(log_a)
        m2 = -jnp.tanh(log_a) * (a * a + 1.0)
        mult = m2 * lax.rsqrt(jnp.maximum(m2, 1e-30))
        a_ref[:, sl] = a
        u_ref[:, sl] = mult * (i * xc[:, sl])

    rowid = lax.broadcasted_iota(jnp.int32, (8, width), 0)

    def group(g, hprev):
        r0 = pl.multiple_of(g * 8, 8)
        a = a_ref[pl.ds(r0, 8), :]
        b = u_ref[pl.ds(r0, 8), :]
        for d in (1, 2, 4):
            keep = rowid >= d
            a_sh = jnp.where(keep, pltpu.roll(a, d, 0), 1.0)
            b_sh = jnp.where(keep, pltpu.roll(b, d, 0), 0.0)
            b = a * b_sh + b
            a = a * a_sh
        out = b + a * hprev
        u_ref[pl.ds(r0, 8), :] = out
        return out[7:8, :]

    h_ref[...] = lax.fori_loop(0, n // 8, group, h_ref[...], unroll=4)
    y_ref[0] = (u_ref[...] * gate).astype(y_ref.dtype)


def rglru_mixer(x, g, w_in_stack, conv_w, conv_b, w_a_stack, b_a, w_i_stack, b_i, lam, j):
    b, s, d = x.shape
    width = w_in_stack.shape[2] // 2
    const2 = lambda bi, t: (0, 0)
    return pl.pallas_call(
        _rglru_kernel,
        out_shape=jax.ShapeDtypeStruct((b, s, width), BF16),
        grid=(b, s // REC_TILE),
        in_specs=[
            pl.BlockSpec((1, REC_TILE, d), lambda bi, t: (bi, t, 0)),
            pl.BlockSpec((1, d), const2),
            _layer_block(w_in_stack, j),
            pl.BlockSpec(conv_w.shape, const2),
            pl.BlockSpec((1, width), const2),
            _layer_block(w_a_stack, j),
            pl.BlockSpec((1, width), const2),
            _layer_block(w_i_stack, j),
            pl.BlockSpec((1, width), const2),
            pl.BlockSpec((1, width), const2),
        ],
        out_specs=pl.BlockSpec((1, REC_TILE, width), lambda bi, t: (bi, t, 0)),
        scratch_shapes=[
            pltpu.VMEM((REC_TILE + 8, width), F32),
            pltpu.VMEM((REC_TILE, width), F32),
            pltpu.VMEM((REC_TILE, width), F32),
            pltpu.VMEM((1, width), F32),
        ],
        compiler_params=_params(("arbitrary", "arbitrary")),
        name="rglru_mixer",
    )(x, g, w_in_stack, conv_w, conv_b, w_a_stack, b_a, w_i_stack, b_i, lam)


def kernel(x, attn_w_in, attn_rel_bias, attn_w_out, rg_w_in, rg_conv_w, rg_conv_b, rg_w_a, rg_b_a,
           rg_w_i, rg_b_i, rg_lambda, rg_w_out, norm_mix_pre, norm_mix_post, norm_ffn_pre,
           norm_ffn_post, ffn_w_gate, ffn_w_up, ffn_w_down):
    b, s, d = x.shape
    depth = norm_mix_pre.shape[0]
    xf = x.reshape(b * s, d)
    row = lambda v: v.reshape(1, -1)
    attn_w_in, attn_w_out, rg_w_in, rg_w_a, rg_w_i, rg_w_out, ffn_w_gate, ffn_w_up, ffn_w_down = (
        w.astype(BF16) for w in (attn_w_in, attn_w_out, rg_w_in, rg_w_a, rg_w_i, rg_w_out,
                                 ffn_w_gate, ffn_w_up, ffn_w_down))
    for layer in range(depth):
        j = layer // 2
        g_pre = row(norm_mix_pre[layer])
        if layer % 2 == 0:
            proj = norm_linear(xf, g_pre, attn_w_in, j, BF16).reshape(b, s, -1)
            out_a = band_attention(proj, band_bias(attn_rel_bias[j]), 0)
            out_b = stick_breaking_attention(proj, 3 * A_W)
            ins = [out_a.reshape(b * s, A_W), out_b.reshape(b * s, B_W)]
            w_out = attn_w_out
        else:
            width = rg_w_out.shape[1]
            y = rglru_mixer(
                xf.reshape(b, s, d), g_pre, rg_w_in,
                rg_conv_w[j].reshape(-1, width), row(rg_conv_b[j]),
                rg_w_a, row(rg_b_a[j]), rg_w_i, row(rg_b_i[j]), row(rg_lambda[j]), j)
            ins = [y.reshape(b * s, width)]
            w_out = rg_w_out
        xf = mixer_out_ffn(
            xf, row(norm_mix_post[layer]), ins, w_out, j,
            row(norm_ffn_pre[layer]), row(norm_ffn_post[layer]),
            ffn_w_gate, ffn_w_up, ffn_w_down, layer)
    return xf.reshape(b, s, d)
```

```python
import functools

import jax
import jax.numpy as jnp
from jax import lax
from jax.experimental import pallas as pl
from jax.experimental.pallas import tpu as pltpu

F32 = jnp.float32
BF16 = jnp.bfloat16

D_MODEL = 1024
HEAD_DIM = 64
CHUNK = 64
N_LEFT_CHUNKS = 8
REL_CLIP = 256
A_W = 512
B_W = 512
LRU_BLOCKS = 4
LRU_BLOCK_W = 256
LRU_C = 8.0
RMS_EPS = 1e-6

LANES = 128
VMEM_LIMIT = 56 * 1024 * 1024

ROW_TILE = 512
SB_TILE = 256
SB_SUBTILES = 2
SB_UNROLL = 1
SB_EXIT = 128.0
A_TILE = 2 * CHUNK
A_SUBTILES = 32
A_WIN = (N_LEFT_CHUNKS + 2) * CHUNK
A_PAD = N_LEFT_CHUNKS * CHUNK
REC_TILE = 512
NEG_BIG = -1e30
LOG2E = 1.4426950408889634

_NT = (((1,), (1,)), ((), ()))


def _params(sem):
    return pltpu.CompilerParams(dimension_semantics=sem, vmem_limit_bytes=VMEM_LIMIT)


def _rms(x, g):
    ms = jnp.mean(x * x, axis=-1, keepdims=True)
    return x * lax.rsqrt(ms + RMS_EPS) * g


def _sigmoid(x):
    return 1.0 / (1.0 + jnp.exp(-x))


def _resident(shape):
    return pl.BlockSpec(shape, lambda *_: (0,) * len(shape), pipeline_mode=pl.Buffered(1))


def _layer_block(stack, layer, rows=None, row_block=0):
    shape = stack.shape[1:] if rows is None else (rows,) + stack.shape[2:]
    index = (layer, row_block) + (0,) * (len(shape) - 1)
    return pl.BlockSpec((None,) + shape, lambda *_: index, pipeline_mode=pl.Buffered(1))


def _norm_linear_kernel(x_ref, g_ref, w_ref, o_ref):
    h = _rms(x_ref[...], g_ref[...]).astype(BF16)
    o_ref[...] = jnp.dot(h, w_ref[...], preferred_element_type=F32).astype(o_ref.dtype)


def norm_linear(x, g, w_stack, layer, out_dtype):
    m, d = x.shape
    n = w_stack.shape[2]
    return pl.pallas_call(
        _norm_linear_kernel,
        out_shape=jax.ShapeDtypeStruct((m, n), out_dtype),
        grid=(m // ROW_TILE,),
        in_specs=[
            pl.BlockSpec((ROW_TILE, d), lambda i: (i, 0)),
            pl.BlockSpec((1, d), lambda i: (0, 0)),
            _layer_block(w_stack, layer),
        ],
        out_specs=pl.BlockSpec((ROW_TILE, n), lambda i: (i, 0)),
        compiler_params=_params(("arbitrary",)),
        name="norm_linear",
    )(x, g, w_stack)


def _out_ffn_kernel(n_in, *refs):
    x_ref, gmix_ref, gpre_ref, gpost_ref, wg_ref, wu_ref, wd_ref = refs[:7]
    ins = refs[7:7 + n_in]
    ws = refs[7 + n_in:7 + 2 * n_in]
    o_ref = refs[7 + 2 * n_in]
    m = jnp.dot(ins[0][...], ws[0][...], preferred_element_type=F32)
    for a, w in zip(ins[1:], ws[1:]):
        m = m + jnp.dot(a[...], w[...], preferred_element_type=F32)
    x = x_ref[...] + _rms(m, gmix_ref[...])
    h = _rms(x, gpre_ref[...]).astype(BF16)
    gate = jnp.dot(h, wg_ref[...], preferred_element_type=F32)
    up = jnp.dot(h, wu_ref[...], preferred_element_type=F32)
    act = (gate * _sigmoid(gate) * up).astype(BF16)
    f = jnp.dot(act, wd_ref[...], preferred_element_type=F32)
    o_ref[...] = x + _rms(f, gpost_ref[...])


def mixer_out_ffn(x, gmix, ins, w_out_stack, j, gpre, gpost, wg_stack, wu_stack, wd_stack, layer):
    m, d = x.shape
    n_in = len(ins)
    in_specs = [pl.BlockSpec((ROW_TILE, d), lambda i: (i, 0))]
    in_specs += [_resident(a.shape) for a in (gmix, gpre, gpost)]
    in_specs += [_layer_block(w, layer) for w in (wg_stack, wu_stack, wd_stack)]
    in_specs += [pl.BlockSpec((ROW_TILE, a.shape[1]), lambda i: (i, 0)) for a in ins]
    in_specs += [_layer_block(w_out_stack, j, a.shape[1], k) for k, a in enumerate(ins)]
    return pl.pallas_call(
        functools.partial(_out_ffn_kernel, n_in),
        out_shape=jax.ShapeDtypeStruct((m, d), F32),
        grid=(m // ROW_TILE,),
        in_specs=in_specs,
        out_specs=pl.BlockSpec((ROW_TILE, d), lambda i: (i, 0)),
        compiler_params=_params(("arbitrary",)),
        name="mixer_out_ffn",
    )(x, gmix, gpre, gpost, wg_stack, wu_stack, wd_stack, *ins, *([w_out_stack] * n_in))


def _band_attn_kernel(q_ref, k_ref, v_ref, bias_ref, o_ref, kp_ref, vp_ref, s_ref, p_ref, rinv_ref):
    t = pl.program_id(2)

    @pl.when(t == 0)
    def _():
        kp_ref[0:A_PAD, :] = jnp.zeros((A_PAD, LANES), BF16)
        vp_ref[0:A_PAD, :] = jnp.zeros((A_PAD, LANES), BF16)
        kp_ref[A_PAD:, :] = k_ref[0]
        vp_ref[A_PAD:, :] = v_ref[0]

    lane = lax.broadcasted_iota(jnp.int32, (1, LANES), 1)
    first = lane < HEAD_DIM
    col = lax.broadcasted_iota(jnp.int32, (1, A_WIN), 1)
    bias = jnp.concatenate([bias_ref[0], bias_ref[1]], axis=0)
    zero = jnp.zeros((A_TILE, LANES), BF16)

    def window(u):
        tile = t * A_SUBTILES + u
        return tile, pl.multiple_of(tile * A_TILE, A_TILE)

    def scores(u, slot, head_of_sequence):
        tile, start = window(u)
        kw = kp_ref[pl.ds(start, A_WIN), :]
        q2 = q_ref[0, pl.ds(pl.multiple_of(u * A_TILE, A_TILE), A_TILE), :] * (HEAD_DIM ** -0.5)
        qs = jnp.concatenate([jnp.where(first, q2, zero), jnp.where(first, zero, q2)], axis=0)
        s = lax.dot_general(qs, kw, _NT, preferred_element_type=F32) + bias
        if head_of_sequence:
            s = jnp.where(col >= (A_PAD - tile * A_TILE), s, NEG_BIG)
        s_ref[slot] = s

    def softmax(slot):
        s = s_ref[slot]
        p = jnp.exp(s - jnp.max(s, axis=-1, keepdims=True))
        p_ref[slot] = p.astype(BF16)
        rinv_ref[slot] = jnp.broadcast_to(1.0 / jnp.sum(p, axis=-1, keepdims=True), (2 * A_TILE, LANES))

    def values(u, slot):
        _, start = window(u)
        vw = vp_ref[pl.ds(start, A_WIN), :]
        o = jnp.dot(p_ref[slot], vw, preferred_element_type=F32) * rinv_ref[slot]
        out = jnp.where(first, o[:A_TILE], o[A_TILE:])
        o_ref[0, pl.ds(pl.multiple_of(u * A_TILE, A_TILE), A_TILE), :] = out.astype(o_ref.dtype)

    scores(0, 0, True)
    scores(1, 1, True)
    softmax(0)

    def pair(k, head_of_sequence):
        values(2 * k - 2, 0)
        softmax(1)
        scores(2 * k, 0, head_of_sequence)
        values(2 * k - 1, 1)
        softmax(0)
        scores(2 * k + 1, 1, head_of_sequence)

    n_head = A_PAD // A_TILE // 2
    for k in range(1, n_head):
        pair(k, True)

    def body(k, _):
        pair(k, False)
        return 0

    lax.fori_loop(n_head, A_SUBTILES // 2, body, 0)
    values(A_SUBTILES - 2, 0)
    softmax(1)
    values(A_SUBTILES - 1, 1)


def band_attention(proj, bias, col0):
    b, s, _ = proj.shape
    nq = col0 // LANES
    nk = (col0 + A_W) // LANES
    nv = (col0 + 2 * A_W) // LANES
    tq = A_SUBTILES * A_TILE
    return pl.pallas_call(
        _band_attn_kernel,
        out_shape=jax.ShapeDtypeStruct((b, s, A_W), BF16),
        grid=(b, A_W // LANES, s // tq),
        in_specs=[
            pl.BlockSpec((1, tq, LANES), lambda bi, p, t: (bi, t, nq + p)),
            pl.BlockSpec((1, s, LANES), lambda bi, p, t: (bi, 0, nk + p)),
            pl.BlockSpec((1, s, LANES), lambda bi, p, t: (bi, 0, nv + p)),
            pl.BlockSpec((2, A_TILE, A_WIN), lambda bi, p, t: (p, 0, 0)),
        ],
        out_specs=pl.BlockSpec((1, tq, LANES), lambda bi, p, t: (bi, t, p)),
        scratch_shapes=[
            pltpu.VMEM((s + A_PAD, LANES), BF16),
            pltpu.VMEM((s + A_PAD, LANES), BF16),
            pltpu.VMEM((2, 2 * A_TILE, A_WIN), F32),
            pltpu.VMEM((2, 2 * A_TILE, A_WIN), BF16),
            pltpu.VMEM((2, 2 * A_TILE, LANES), F32),
        ],
        compiler_params=_params(("arbitrary", "arbitrary", "arbitrary")),
        name="band_attention",
    )(proj, proj, proj, bias)


def band_bias(rel_bias):
    h = rel_bias.shape[0]
    n_f = A_WIN + A_TILE - 1
    n_const = A_PAD + A_TILE - 1 - REL_CLIP
    far = jnp.broadcast_to(rel_bias[:, 2 * REL_CLIP:], (h, n_const))
    near = rel_bias[:, 2 * REL_CLIP + 1 - (n_f - n_const):][:, ::-1]
    f = jnp.concatenate([far, near], axis=1)
    period = n_f + 1
    g = jnp.concatenate([f[:, A_TILE - 1:], jnp.zeros((h, 1), F32), f[:, :A_TILE - 1]], axis=1)
    flat = jnp.tile(g, (1, A_TILE))[:, :A_TILE * (period - 1)]
    toep = flat.reshape(h, A_TILE, period - 1)[:, :, :A_WIN]
    r = jnp.arange(A_TILE)[:, None]
    c = jnp.arange(A_WIN)[None, :]
    dchunk = c // CHUNK - r // CHUNK
    ok = (dchunk >= 0) & (dchunk <= N_LEFT_CHUNKS)
    return jnp.where(ok[None], toep, NEG_BIG).astype(F32)


def _stick_kernel(q_ref, k_ref, v_ref, o_ref, acc_ref, carry_ref, z_ref, sp_ref):
    qi = pl.program_id(2)
    t = SB_TILE
    ns = SB_SUBTILES
    lane = lax.broadcasted_iota(jnp.int32, (1, LANES), 1)
    first = lane < HEAD_DIM
    q2 = q_ref[0] * (HEAD_DIM ** -0.5)
    row = lax.broadcasted_iota(jnp.int32, (t, t), 0)
    col = lax.broadcasted_iota(jnp.int32, (t, t), 1)
    causal = col < row
    true = jnp.ones((t, t), jnp.bool_)
    suffix = jnp.where(row >= col, 1.0, 0.0).astype(BF16)
    zero = jnp.zeros((t, LANES), BF16)
    acc_ref[...] = jnp.zeros_like(acc_ref)
    carry_ref[...] = jnp.zeros_like(carry_ref)

    def split_heads(x):
        return jnp.concatenate([jnp.where(first, x, zero), jnp.where(first, zero, x)], axis=0)

    def scores(j, d, mask):
        start = pl.multiple_of(j * t, t)
        kcat = split_heads(k_ref[0, pl.ds(start, t), :])
        z = lax.dot_general(q2[d * t:], kcat, _NT, preferred_element_type=F32)
        sp = jnp.maximum(z, 0.0) + jnp.log(1.0 + jnp.exp2(jnp.abs(z) * (-LOG2E)))
        if mask is not None:
            sp = jnp.where(mask, sp, 0.0)
        spb = sp.astype(BF16)
        return z - (sp - spb.astype(F32)), spb

    def weights(j, d, mask, tz, spb):
        r0 = d * t
        n = (ns - d) * t
        start = pl.multiple_of(j * t, t)
        vcat = split_heads(v_ref[0, pl.ds(start, t), :])
        c = [jnp.dot(spb[:, h * t:(h + 1) * t], suffix, preferred_element_type=F32) for h in range(2)]
        carry = [carry_ref[h, r0:, :] for h in range(2)]
        reps = t // LANES
        shift = jnp.concatenate([c[0]] + [c[1]], axis=1) + jnp.concatenate(
            [carry[0]] * reps + [carry[1]] * reps, axis=1)
        w = jnp.exp(tz - shift)
        if mask is not None:
            w = jnp.where(mask, w, 0.0)
        for h in range(2):
            carry_ref[h, r0:, :] = carry[h] + jnp.broadcast_to(c[h][:, 0:1], (n, LANES))
        acc_ref[r0:, :] += jnp.dot(w.astype(BF16), vcat, preferred_element_type=F32)

    for d in range(ns - 1, -1, -1):
        mask = jnp.concatenate([causal] + [true] * (ns - d - 1), axis=0)
        mask = jnp.concatenate([mask, mask], axis=1)
        tz, spb = scores(ns * qi + d, d, mask)
        weights(ns * qi + d, d, mask, tz, spb)

    def stage1(i, slot):
        tz, spb = scores(ns * qi - 1 - i, 0, None)
        z_ref[slot] = tz
        sp_ref[slot] = spb

    def stage2(i, slot):
        weights(ns * qi - 1 - i, 0, None, z_ref[slot], sp_ref[slot])

    @pl.when(qi > 0)
    def _():
        n_groups = (ns * qi) // SB_UNROLL

        def cond(state):
            g, done = state
            return jnp.logical_and(g < n_groups, done == 0)

        def body(state):
            g, _ = state
            i0 = SB_UNROLL * g
            stage1(i0, 0)
            for off in range(SB_UNROLL - 1):
                stage1(i0 + off + 1, (off + 1) % 2)
                stage2(i0 + off, off % 2)
            stage2(i0 + SB_UNROLL - 1, (SB_UNROLL - 1) % 2)
            low = jnp.min(jnp.minimum(carry_ref[0], carry_ref[1]), axis=0, keepdims=True)
            return g + 1, (jnp.min(low) >= SB_EXIT).astype(jnp.int32)

        lax.while_loop(cond, body, (jnp.int32(0), jnp.int32(0)))

    o_ref[0] = acc_ref[...].astype(o_ref.dtype)


def stick_breaking_attention(proj, col0):
    b, s, _ = proj.shape
    nq = col0 // LANES
    nk = (col0 + B_W) // LANES
    nv = (col0 + 2 * B_W) // LANES
    tq = SB_SUBTILES * SB_TILE
    return pl.pallas_call(
        _stick_kernel,
        out_shape=jax.ShapeDtypeStruct((b, s, B_W), BF16),
        grid=(b, B_W // LANES, s // tq),
        in_specs=[
            pl.BlockSpec((1, tq, LANES), lambda bi, p, t: (bi, t, nq + p)),
            pl.BlockSpec((1, s, LANES), lambda bi, p, t: (bi, 0, nk + p)),
            pl.BlockSpec((1, s, LANES), lambda bi, p, t: (bi, 0, nv + p)),
        ],
        out_specs=pl.BlockSpec((1, tq, LANES), lambda bi, p, t: (bi, t, p)),
        scratch_shapes=[
            pltpu.VMEM((tq, LANES), F32),
            pltpu.VMEM((2, tq, LANES), F32),
            pltpu.VMEM((2, tq, 2 * SB_TILE), F32),
            pltpu.VMEM((2, tq, 2 * SB_TILE), BF16),
        ],
        compiler_params=_params(("arbitrary", "arbitrary", "arbitrary")),
        name="stick_breaking",
    )(proj, proj, proj)


def _rglru_kernel(x_ref, g_ref, win_ref, cw_ref, cb_ref, wa_ref, ba_ref, wi_ref, bi_ref, lam_ref,
                  y_ref, xpad_ref, a_ref, u_ref, h_ref):
    t = pl.program_id(1)
    n = REC_TILE
    width = y_ref.shape[2]

    @pl.when(t == 0)
    def _():
        xpad_ref[0:8, :] = jnp.zeros((8, width), F32)
        h_ref[...] = jnp.zeros_like(h_ref)

    hn = _rms(x_ref[0], g_ref[...]).astype(BF16)
    gate = jnp.dot(hn, win_ref[:, :width], preferred_element_type=F32)
    c = 0.7978845608028654
    gate = 0.5 * gate * (1.0 + jnp.tanh(c * (gate + 0.044715 * (gate * gate * gate))))
    xpad_ref[8:8 + n, :] = jnp.dot(hn, win_ref[:, width:], preferred_element_type=F32)
    cw = cw_ref[...]
    xc = (cb_ref[...]
          + cw[3:4] * xpad_ref[8:8 + n, :]
          + cw[2:3] * xpad_ref[7:7 + n, :]
          + cw[1:2] * xpad_ref[6:6 + n, :]
          + cw[0:1] * xpad_ref[5:5 + n, :])
    xpad_ref[0:8, :] = xpad_ref[n:n + 8, :]
    xcb = xc.astype(BF16)
    lam = lam_ref[...]
    sp_lam = jnp.maximum(-lam, 0.0) + jnp.log1p(jnp.exp(-jnp.abs(lam)))
    for blk in range(LRU_BLOCKS):
        sl = slice(blk * LRU_BLOCK_W, (blk + 1) * LRU_BLOCK_W)
        xb = xcb[:, sl]
        pre_r = jnp.dot(xb, wa_ref[blk], preferred_element_type=F32) + ba_ref[:, sl]
        pre_i = jnp.dot(xb, wi_ref[blk], preferred_element_type=F32) + bi_ref[:, sl]
        r = 0.5 * jnp.tanh(0.5 * pre_r) + 0.5
        i = 0.5 * jnp.tanh(0.5 * pre_i) + 0.5
        log_a = (-LRU_C * r) * sp_lam[:, sl]
        a = jnp.exp(log_a)
        m2 = -jnp.tanh(log_a) * (a * a + 1.0)
        mult = m2 * lax.rsqrt(jnp.maximum(m2, 1e-30))
        a_ref[:, sl] = a
        u_ref[:, sl] = mult * (i * xc[:, sl])

    rowid = lax.broadcasted_iota(jnp.int32, (8, width), 0)

    def group(g, hprev):
        r0 = pl.multiple_of(g * 8, 8)
        a = a_ref[pl.ds(r0, 8), :]
        b = u_ref[pl.ds(r0, 8), :]
        for d in (1, 2, 4):
            keep = rowid >= d
            a_sh = jnp.where(keep, pltpu.roll(a, d, 0), 1.0)
            b_sh = jnp.where(keep, pltpu.roll(b, d, 0), 0.0)
            b = a * b_sh + b
            a = a * a_sh
        out = b + a * hprev
        u_ref[pl.ds(r0, 8), :] = out
        return out[7:8, :]

    h_ref[...] = lax.fori_loop(0, n // 8, group, h_ref[...], unroll=4)
    y_ref[0] = (u_ref[...] * gate).astype(y_ref.dtype)


def rglru_mixer(x, g, w_in_stack, conv_w, conv_b, w_a_stack, b_a, w_i_stack, b_i, lam, j):
    b, s, d = x.shape
    width = w_in_stack.shape[2] // 2
    const2 = lambda bi, t: (0, 0)
    return pl.pallas_call(
        _rglru_kernel,
        out_shape=jax.ShapeDtypeStruct((b, s, width), BF16),
        grid=(b, s // REC_TILE),
        in_specs=[
            pl.BlockSpec((1, REC_TILE, d), lambda bi, t: (bi, t, 0)),
            pl.BlockSpec((1, d), const2),
            _layer_block(w_in_stack, j),
            pl.BlockSpec(conv_w.shape, const2),
            pl.BlockSpec((1, width), const2),
            _layer_block(w_a_stack, j),
            pl.BlockSpec((1, width), const2),
            _layer_block(w_i_stack, j),
            pl.BlockSpec((1, width), const2),
            pl.BlockSpec((1, width), const2),
        ],
        out_specs=pl.BlockSpec((1, REC_TILE, width), lambda bi, t: (bi, t, 0)),
        scratch_shapes=[
            pltpu.VMEM((REC_TILE + 8, width), F32),
            pltpu.VMEM((REC_TILE, width), F32),
            pltpu.VMEM((REC_TILE, width), F32),
            pltpu.VMEM((1, width), F32),
        ],
        compiler_params=_params(("arbitrary", "arbitrary")),
        name="rglru_mixer",
    )(x, g, w_in_stack, conv_w, conv_b, w_a_stack, b_a, w_i_stack, b_i, lam)


def kernel(x, attn_w_in, attn_rel_bias, attn_w_out, rg_w_in, rg_conv_w, rg_conv_b, rg_w_a, rg_b_a,
           rg_w_i, rg_b_i, rg_lambda, rg_w_out, norm_mix_pre, norm_mix_post, norm_ffn_pre,
           norm_ffn_post, ffn_w_gate, ffn_w_up, ffn_w_down):
    b, s, d = x.shape
    depth = norm_mix_pre.shape[0]
    xf = x.reshape(b * s, d)
    row = lambda v: v.reshape(1, -1)
    attn_w_in, attn_w_out, rg_w_in, rg_w_a, rg_w_i, rg_w_out, ffn_w_gate, ffn_w_up, ffn_w_down = (
        w.astype(BF16) for w in (attn_w_in, attn_w_out, rg_w_in, rg_w_a, rg_w_i, rg_w_out,
                                 ffn_w_gate, ffn_w_up, ffn_w_down))
    for layer in range(depth):
        j = layer // 2
        g_pre = row(norm_mix_pre[layer])
        if layer % 2 == 0:
            proj = norm_linear(xf, g_pre, attn_w_in, j, BF16).reshape(b, s, -1)
            out_a = band_attention(proj, band_bias(attn_rel_bias[j]), 0)
            out_b = stick_breaking_attention(proj, 3 * A_W)
            ins = [out_a.reshape(b * s, A_W), out_b.reshape(b * s, B_W)]
            w_out = attn_w_out
        else:
            width = rg_w_out.shape[1]
            y = rglru_mixer(
                xf.reshape(b, s, d), g_pre, rg_w_in,
                rg_conv_w[j].reshape(-1, width), row(rg_conv_b[j]),
                rg_w_a, row(rg_b_a[j]), rg_w_i, row(rg_b_i[j]), row(rg_lambda[j]), j)
            ins = [y.reshape(b * s, width)]
            w_out = rg_w_out
        xf = mixer_out_ffn(
            xf, row(norm_mix_post[layer]), ins, w_out, j,
            row(norm_ffn_pre[layer]), row(norm_ffn_post[layer]),
            ffn_w_gate, ffn_w_up, ffn_w_down, layer)
    return xf.reshape(b, s, d)
```

```python
import functools

import jax
import jax.numpy as jnp
from jax import lax
from jax.experimental import pallas as pl
from jax.experimental.pallas import tpu as pltpu

F32 = jnp.float32
BF16 = jnp.bfloat16

D_MODEL = 1024
HEAD_DIM = 64
CHUNK = 64
N_LEFT_CHUNKS = 8
REL_CLIP = 256
A_W = 512
B_W = 512
LRU_BLOCKS = 4
LRU_BLOCK_W = 256
LRU_C = 8.0
RMS_EPS = 1e-6

LANES = 128
VMEM_LIMIT = 56 * 1024 * 1024

ROW_TILE = 512
SB_TILE = 256
SB_SUBTILES = 2
SB_UNROLL = 1
SB_PAIRS = 4
SB_EXIT = 128.0
A_TILE = 2 * CHUNK
A_SUBTILES = 32
A_WIN = (N_LEFT_CHUNKS + 2) * CHUNK
A_PAD = N_LEFT_CHUNKS * CHUNK
REC_TILE = 512
NEG_BIG = -1e30
LOG2E = 1.4426950408889634

_NT = (((1,), (1,)), ((), ()))


def _params(sem):
    return pltpu.CompilerParams(dimension_semantics=sem, vmem_limit_bytes=VMEM_LIMIT)


def _rms(x, g):
    ms = jnp.mean(x * x, axis=-1, keepdims=True)
    return x * lax.rsqrt(ms + RMS_EPS) * g


def _sigmoid(x):
    return 1.0 / (1.0 + jnp.exp(-x))


def _resident(shape):
    return pl.BlockSpec(shape, lambda *_: (0,) * len(shape), pipeline_mode=pl.Buffered(1))


def _layer_block(stack, layer, rows=None, row_block=0):
    shape = stack.shape[1:] if rows is None else (rows,) + stack.shape[2:]
    index = (layer, row_block) + (0,) * (len(shape) - 1)
    return pl.BlockSpec((None,) + shape, lambda *_: index, pipeline_mode=pl.Buffered(1))


def _norm_linear_kernel(x_ref, g_ref, w_ref, o_ref):
    h = _rms(x_ref[...], g_ref[...]).astype(BF16)
    o_ref[...] = jnp.dot(h, w_ref[...], preferred_element_type=F32).astype(o_ref.dtype)


def norm_linear(x, g, w_stack, layer, out_dtype):
    m, d = x.shape
    n = w_stack.shape[2]
    return pl.pallas_call(
        _norm_linear_kernel,
        out_shape=jax.ShapeDtypeStruct((m, n), out_dtype),
        grid=(m // ROW_TILE,),
        in_specs=[
            pl.BlockSpec((ROW_TILE, d), lambda i: (i, 0)),
            pl.BlockSpec((1, d), lambda i: (0, 0)),
            _layer_block(w_stack, layer),
        ],
        out_specs=pl.BlockSpec((ROW_TILE, n), lambda i: (i, 0)),
        compiler_params=_params(("arbitrary",)),
        name="norm_linear",
    )(x, g, w_stack)


def _out_ffn_kernel(n_in, *refs):
    x_ref, gmix_ref, gpre_ref, gpost_ref, wg_ref, wu_ref, wd_ref = refs[:7]
    ins = refs[7:7 + n_in]
    ws = refs[7 + n_in:7 + 2 * n_in]
    o_ref = refs[7 + 2 * n_in]
    m = jnp.dot(ins[0][...], ws[0][...], preferred_element_type=F32)
    for a, w in zip(ins[1:], ws[1:]):
        m = m + jnp.dot(a[...], w[...], preferred_element_type=F32)
    x = x_ref[...] + _rms(m, gmix_ref[...])
    h = _rms(x, gpre_ref[...]).astype(BF16)
    gate = jnp.dot(h, wg_ref[...], preferred_element_type=F32)
    up = jnp.dot(h, wu_ref[...], preferred_element_type=F32)
    act = (gate * _sigmoid(gate) * up).astype(BF16)
    f = jnp.dot(act, wd_ref[...], preferred_element_type=F32)
    o_ref[...] = x + _rms(f, gpost_ref[...])


def mixer_out_ffn(x, gmix, ins, w_out_stack, j, gpre, gpost, wg_stack, wu_stack, wd_stack, layer):
    m, d = x.shape
    n_in = len(ins)
    in_specs = [pl.BlockSpec((ROW_TILE, d), lambda i: (i, 0))]
    in_specs += [_resident(a.shape) for a in (gmix, gpre, gpost)]
    in_specs += [_layer_block(w, layer) for w in (wg_stack, wu_stack, wd_stack)]
    in_specs += [pl.BlockSpec((ROW_TILE, a.shape[1]), lambda i: (i, 0)) for a in ins]
    in_specs += [_layer_block(w_out_stack, j, a.shape[1], k) for k, a in enumerate(ins)]
    return pl.pallas_call(
        functools.partial(_out_ffn_kernel, n_in),
        out_shape=jax.ShapeDtypeStruct((m, d), F32),
        grid=(m // ROW_TILE,),
        in_specs=in_specs,
        out_specs=pl.BlockSpec((ROW_TILE, d), lambda i: (i, 0)),
        compiler_params=_params(("arbitrary",)),
        name="mixer_out_ffn",
    )(x, gmix, gpre, gpost, wg_stack, wu_stack, wd_stack, *ins, *([w_out_stack] * n_in))


def _band_attn_kernel(q_ref, k_ref, v_ref, bias_ref, o_ref, kp_ref, vp_ref, s_ref, p_ref, rinv_ref):
    t = pl.program_id(2)

    @pl.when(t == 0)
    def _():
        kp_ref[0:A_PAD, :] = jnp.zeros((A_PAD, LANES), BF16)
        vp_ref[0:A_PAD, :] = jnp.zeros((A_PAD, LANES), BF16)
        kp_ref[A_PAD:, :] = k_ref[0]
        vp_ref[A_PAD:, :] = v_ref[0]

    lane = lax.broadcasted_iota(jnp.int32, (1, LANES), 1)
    first = lane < HEAD_DIM
    col = lax.broadcasted_iota(jnp.int32, (1, A_WIN), 1)
    bias = jnp.concatenate([bias_ref[0], bias_ref[1]], axis=0)
    zero = jnp.zeros((A_TILE, LANES), BF16)

    def window(u):
        tile = t * A_SUBTILES + u
        return tile, pl.multiple_of(tile * A_TILE, A_TILE)

    def scores(u, slot, head_of_sequence):
        tile, start = window(u)
        kw = kp_ref[pl.ds(start, A_WIN), :]
        q2 = q_ref[0, pl.ds(pl.multiple_of(u * A_TILE, A_TILE), A_TILE), :] * (HEAD_DIM ** -0.5)
        qs = jnp.concatenate([jnp.where(first, q2, zero), jnp.where(first, zero, q2)], axis=0)
        s = lax.dot_general(qs, kw, _NT, preferred_element_type=F32) + bias
        if head_of_sequence:
            s = jnp.where(col >= (A_PAD - tile * A_TILE), s, NEG_BIG)
        s_ref[slot] = s

    def softmax(slot):
        s = s_ref[slot]
        p = jnp.exp(s - jnp.max(s, axis=-1, keepdims=True))
        p_ref[slot] = p.astype(BF16)
        rinv_ref[slot] = jnp.broadcast_to(1.0 / jnp.sum(p, axis=-1, keepdims=True), (2 * A_TILE, LANES))

    def values(u, slot):
        _, start = window(u)
        vw = vp_ref[pl.ds(start, A_WIN), :]
        o = jnp.dot(p_ref[slot], vw, preferred_element_type=F32) * rinv_ref[slot]
        out = jnp.where(first, o[:A_TILE], o[A_TILE:])
        o_ref[0, pl.ds(pl.multiple_of(u * A_TILE, A_TILE), A_TILE), :] = out.astype(o_ref.dtype)

    scores(0, 0, True)
    scores(1, 1, True)
    softmax(0)

    def pair(k, head_of_sequence):
        values(2 * k - 2, 0)
        softmax(1)
        scores(2 * k, 0, head_of_sequence)
        values(2 * k - 1, 1)
        softmax(0)
        scores(2 * k + 1, 1, head_of_sequence)

    n_head = A_PAD // A_TILE // 2
    for k in range(1, n_head):
        pair(k, True)

    def body(k, _):
        pair(k, False)
        return 0

    lax.fori_loop(n_head, A_SUBTILES // 2, body, 0)
    values(A_SUBTILES - 2, 0)
    softmax(1)
    values(A_SUBTILES - 1, 1)


def band_attention(proj, bias, col0):
    b, s, _ = proj.shape
    nq = col0 // LANES
    nk = (col0 + A_W) // LANES
    nv = (col0 + 2 * A_W) // LANES
    tq = A_SUBTILES * A_TILE
    return pl.pallas_call(
        _band_attn_kernel,
        out_shape=jax.ShapeDtypeStruct((b, s, A_W), BF16),
        grid=(b, A_W // LANES, s // tq),
        in_specs=[
            pl.BlockSpec((1, tq, LANES), lambda bi, p, t: (bi, t, nq + p)),
            pl.BlockSpec((1, s, LANES), lambda bi, p, t: (bi, 0, nk + p)),
            pl.BlockSpec((1, s, LANES), lambda bi, p, t: (bi, 0, nv + p)),
            pl.BlockSpec((2, A_TILE, A_WIN), lambda bi, p, t: (p, 0, 0)),
        ],
        out_specs=pl.BlockSpec((1, tq, LANES), lambda bi, p, t: (bi, t, p)),
        scratch_shapes=[
            pltpu.VMEM((s + A_PAD, LANES), BF16),
            pltpu.VMEM((s + A_PAD, LANES), BF16),
            pltpu.VMEM((2, 2 * A_TILE, A_WIN), F32),
            pltpu.VMEM((2, 2 * A_TILE, A_WIN), BF16),
            pltpu.VMEM((2, 2 * A_TILE, LANES), F32),
        ],
        compiler_params=_params(("arbitrary", "arbitrary", "arbitrary")),
        name="band_attention",
    )(proj, proj, proj, bias)


def band_bias(rel_bias):
    h = rel_bias.shape[0]
    n_f = A_WIN + A_TILE - 1
    n_const = A_PAD + A_TILE - 1 - REL_CLIP
    far = jnp.broadcast_to(rel_bias[:, 2 * REL_CLIP:], (h, n_const))
    near = rel_bias[:, 2 * REL_CLIP + 1 - (n_f - n_const):][:, ::-1]
    f = jnp.concatenate([far, near], axis=1)
    period = n_f + 1
    g = jnp.concatenate([f[:, A_TILE - 1:], jnp.zeros((h, 1), F32), f[:, :A_TILE - 1]], axis=1)
    flat = jnp.tile(g, (1, A_TILE))[:, :A_TILE * (period - 1)]
    toep = flat.reshape(h, A_TILE, period - 1)[:, :, :A_WIN]
    r = jnp.arange(A_TILE)[:, None]
    c = jnp.arange(A_WIN)[None, :]
    dchunk = c // CHUNK - r // CHUNK
    ok = (dchunk >= 0) & (dchunk <= N_LEFT_CHUNKS)
    return jnp.where(ok[None], toep, NEG_BIG).astype(F32)


def _stick_kernel(q_ref, k_ref, v_ref, o_ref, acc_ref, carry_ref, z_ref, sp_ref):
    qi = pl.program_id(2)
    t = SB_TILE
    ns = SB_SUBTILES
    lane = lax.broadcasted_iota(jnp.int32, (1, LANES), 1)
    first = lane < HEAD_DIM
    row = lax.broadcasted_iota(jnp.int32, (t, t), 0)
    col = lax.broadcasted_iota(jnp.int32, (t, t), 1)
    causal = col < row
    true = jnp.ones((t, t), jnp.bool_)
    suffix = jnp.where(row >= col, 1.0, 0.0).astype(BF16)
    zero = jnp.zeros((t, LANES), BF16)

    def one_pair(lanes):
        q2 = q_ref[0, :, lanes] * (HEAD_DIM ** -0.5)
        acc_ref[...] = jnp.zeros_like(acc_ref)
        carry_ref[...] = jnp.zeros_like(carry_ref)

        def split_heads(x):
            return jnp.concatenate([jnp.where(first, x, zero), jnp.where(first, zero, x)], axis=0)

        def scores(j, d, mask):
            start = pl.multiple_of(j * t, t)
            kcat = split_heads(k_ref[0, pl.ds(start, t), lanes])
            z = lax.dot_general(q2[d * t:], kcat, _NT, preferred_element_type=F32)
            sp = jnp.maximum(z, 0.0) + jnp.log(1.0 + jnp.exp2(jnp.abs(z) * (-LOG2E)))
            if mask is not None:
                sp = jnp.where(mask, sp, 0.0)
            spb = sp.astype(BF16)
            return z - (sp - spb.astype(F32)), spb

        def weights(j, d, mask, tz, spb):
            r0 = d * t
            n = (ns - d) * t
            start = pl.multiple_of(j * t, t)
            vcat = split_heads(v_ref[0, pl.ds(start, t), lanes])
            c = [jnp.dot(spb[:, h * t:(h + 1) * t], suffix, preferred_element_type=F32) for h in range(2)]
            carry = [carry_ref[h, r0:, :] for h in range(2)]
            reps = t // LANES
            shift = jnp.concatenate([c[0]] + [c[1]], axis=1) + jnp.concatenate(
                [carry[0]] * reps + [carry[1]] * reps, axis=1)
            w = jnp.exp(tz - shift)
            if mask is not None:
                w = jnp.where(mask, w, 0.0)
            for h in range(2):
                carry_ref[h, r0:, :] = carry[h] + jnp.broadcast_to(c[h][:, 0:1], (n, LANES))
            acc_ref[r0:, :] += jnp.dot(w.astype(BF16), vcat, preferred_element_type=F32)

        for d in range(ns - 1, -1, -1):
            mask = jnp.concatenate([causal] + [true] * (ns - d - 1), axis=0)
            mask = jnp.concatenate([mask, mask], axis=1)
            tz, spb = scores(ns * qi + d, d, mask)
            weights(ns * qi + d, d, mask, tz, spb)

        def stage1(i, slot):
            tz, spb = scores(ns * qi - 1 - i, 0, None)
            z_ref[slot] = tz
            sp_ref[slot] = spb

        def stage2(i, slot):
            weights(ns * qi - 1 - i, 0, None, z_ref[slot], sp_ref[slot])

        @pl.when(qi > 0)
        def _():
            n_groups = (ns * qi) // SB_UNROLL

            def cond(state):
                g, done = state
                return jnp.logical_and(g < n_groups, done == 0)

            def body(state):
                g, _ = state
                i0 = SB_UNROLL * g
                stage1(i0, 0)
                for off in range(SB_UNROLL - 1):
                    stage1(i0 + off + 1, (off + 1) % 2)
                    stage2(i0 + off, off % 2)
                stage2(i0 + SB_UNROLL - 1, (SB_UNROLL - 1) % 2)
                low = jnp.min(jnp.minimum(carry_ref[0], carry_ref[1]), axis=0, keepdims=True)
                return g + 1, (jnp.min(low) >= SB_EXIT).astype(jnp.int32)

            lax.while_loop(cond, body, (jnp.int32(0), jnp.int32(0)))

        o_ref[0, :, lanes] = acc_ref[...].astype(o_ref.dtype)

    for pair in range(SB_PAIRS):
        one_pair(slice(pair * LANES, (pair + 1) * LANES))


def stick_breaking_attention(proj, col0):
    b, s, _ = proj.shape
    width = SB_PAIRS * LANES
    nq = col0 // width
    nk = (col0 + B_W) // width
    nv = (col0 + 2 * B_W) // width
    tq = SB_SUBTILES * SB_TILE
    return pl.pallas_call(
        _stick_kernel,
        out_shape=jax.ShapeDtypeStruct((b, s, B_W), BF16),
        grid=(b, B_W // width, s // tq),
        in_specs=[
            pl.BlockSpec((1, tq, width), lambda bi, p, t: (bi, t, nq + p)),
            pl.BlockSpec((1, s, width), lambda bi, p, t: (bi, 0, nk + p)),
            pl.BlockSpec((1, s, width), lambda bi, p, t: (bi, 0, nv + p)),
        ],
        out_specs=pl.BlockSpec((1, tq, width), lambda bi, p, t: (bi, t, p)),
        scratch_shapes=[
            pltpu.VMEM((tq, LANES), F32),
            pltpu.VMEM((2, tq, LANES), F32),
            pltpu.VMEM((2, tq, 2 * SB_TILE), F32),
            pltpu.VMEM((2, tq, 2 * SB_TILE), BF16),
        ],
        compiler_params=_params(("arbitrary", "arbitrary", "arbitrary")),
        name="stick_breaking",
    )(proj, proj, proj)


def _rglru_kernel(x_ref, g_ref, win_ref, cw_ref, cb_ref, wa_ref, ba_ref, wi_ref, bi_ref, lam_ref,
                  y_ref, xpad_ref, a_ref, u_ref, h_ref):
    t = pl.program_id(1)
    n = REC_TILE
    width = y_ref.shape[2]

    @pl.when(t == 0)
    def _():
        xpad_ref[0:8, :] = jnp.zeros((8, width), F32)
        h_ref[...] = jnp.zeros_like(h_ref)

    hn = _rms(x_ref[0], g_ref[...]).astype(BF16)
    gate = jnp.dot(hn, win_ref[:, :width], preferred_element_type=F32)
    c = 0.7978845608028654
    gate = 0.5 * gate * (1.0 + jnp.tanh(c * (gate + 0.044715 * (gate * gate * gate))))
    xpad_ref[8:8 + n, :] = jnp.dot(hn, win_ref[:, width:], preferred_element_type=F32)
    cw = cw_ref[...]
    xc = (cb_ref[...]
          + cw[3:4] * xpad_ref[8:8 + n, :]
          + cw[2:3] * xpad_ref[7:7 + n, :]
          + cw[1:2] * xpad_ref[6:6 + n, :]
          + cw[0:1] * xpad_ref[5:5 + n, :])
    xpad_ref[0:8, :] = xpad_ref[n:n + 8, :]
    xcb = xc.astype(BF16)
    lam = lam_ref[...]
    sp_lam = jnp.maximum(-lam, 0.0) + jnp.log1p(jnp.exp(-jnp.abs(lam)))
    for blk in range(LRU_BLOCKS):
        sl = slice(blk * LRU_BLOCK_W, (blk + 1) * LRU_BLOCK_W)
        xb = xcb[:, sl]
        pre_r = jnp.dot(xb, wa_ref[blk], preferred_element_type=F32) + ba_ref[:, sl]
        pre_i = jnp.dot(xb, wi_ref[blk], preferred_element_type=F32) + bi_ref[:, sl]
        r = 0.5 * jnp.tanh(0.5 * pre_r) + 0.5
        i = 0.5 * jnp.tanh(0.5 * pre_i) + 0.5
        log_a = (-LRU_C * r) * sp_lam[:, sl]
        a = jnp.exp(log_a)
        m2 = -jnp.tanh(log_a) * (a * a + 1.0)
        mult = m2 * lax.rsqrt(jnp.maximum(m2, 1e-30))
        a_ref[:, sl] = a
        u_ref[:, sl] = mult * (i * xc[:, sl])

    rowid = lax.broadcasted_iota(jnp.int32, (8, width), 0)

    def group(g, hprev):
        r0 = pl.multiple_of(g * 8, 8)
        a = a_ref[pl.ds(r0, 8), :]
        b = u_ref[pl.ds(r0, 8), :]
        for d in (1, 2, 4):
            keep = rowid >= d
            a_sh = jnp.where(keep, pltpu.roll(a, d, 0), 1.0)
            b_sh = jnp.where(keep, pltpu.roll(b, d, 0), 0.0)
            b = a * b_sh + b
            a = a * a_sh
        out = b + a * hprev
        u_ref[pl.ds(r0, 8), :] = out
        return out[7:8, :]

    h_ref[...] = lax.fori_loop(0, n // 8, group, h_ref[...], unroll=4)
    y_ref[0] = (u_ref[...] * gate).astype(y_ref.dtype)


def rglru_mixer(x, g, w_in_stack, conv_w, conv_b, w_a_stack, b_a, w_i_stack, b_i, lam, j):
    b, s, d = x.shape
    width = w_in_stack.shape[2] // 2
    const2 = lambda bi, t: (0, 0)
    return pl.pallas_call(
        _rglru_kernel,
        out_shape=jax.ShapeDtypeStruct((b, s, width), BF16),
        grid=(b, s // REC_TILE),
        in_specs=[
            pl.BlockSpec((1, REC_TILE, d), lambda bi, t: (bi, t, 0)),
            pl.BlockSpec((1, d), const2),
            _layer_block(w_in_stack, j),
            pl.BlockSpec(conv_w.shape, const2),
            pl.BlockSpec((1, width), const2),
            _layer_block(w_a_stack, j),
            pl.BlockSpec((1, width), const2),
            _layer_block(w_i_stack, j),
            pl.BlockSpec((1, width), const2),
            pl.BlockSpec((1, width), const2),
        ],
        out_specs=pl.BlockSpec((1, REC_TILE, width), lambda bi, t: (bi, t, 0)),
        scratch_shapes=[
            pltpu.VMEM((REC_TILE + 8, width), F32),
            pltpu.VMEM((REC_TILE, width), F32),
            pltpu.VMEM((REC_TILE, width), F32),
            pltpu.VMEM((1, width), F32),
        ],
        compiler_params=_params(("arbitrary", "arbitrary")),
        name="rglru_mixer",
    )(x, g, w_in_stack, conv_w, conv_b, w_a_stack, b_a, w_i_stack, b_i, lam)


def kernel(x, attn_w_in, attn_rel_bias, attn_w_out, rg_w_in, rg_conv_w, rg_conv_b, rg_w_a, rg_b_a,
           rg_w_i, rg_b_i, rg_lambda, rg_w_out, norm_mix_pre, norm_mix_post, norm_ffn_pre,
           norm_ffn_post, ffn_w_gate, ffn_w_up, ffn_w_down):
    b, s, d = x.shape
    depth = norm_mix_pre.shape[0]
    xf = x.reshape(b * s, d)
    row = lambda v: v.reshape(1, -1)
    attn_w_in, attn_w_out, rg_w_in, rg_w_a, rg_w_i, rg_w_out, ffn_w_gate, ffn_w_up, ffn_w_down = (
        w.astype(BF16) for w in (attn_w_in, attn_w_out, rg_w_in, rg_w_a, rg_w_i, rg_w_out,
                                 ffn_w_gate, ffn_w_up, ffn_w_down))
    for layer in range(depth):
        j = layer // 2
        g_pre = row(norm_mix_pre[layer])
        if layer % 2 == 0:
            proj = norm_linear(xf, g_pre, attn_w_in, j, BF16).reshape(b, s, -1)
            out_a = band_attention(proj, band_bias(attn_rel_bias[j]), 0)
            out_b = stick_breaking_attention(proj, 3 * A_W)
            ins = [out_a.reshape(b * s, A_W), out_b.reshape(b * s, B_W)]
            w_out = attn_w_out
        else:
            width = rg_w_out.shape[1]
            y = rglru_mixer(
                xf.reshape(b, s, d), g_pre, rg_w_in,
                rg_conv_w[j].reshape(-1, width), row(rg_conv_b[j]),
                rg_w_a, row(rg_b_a[j]), rg_w_i, row(rg_b_i[j]), row(rg_lambda[j]), j)
            ins = [y.reshape(b * s, width)]
            w_out = rg_w_out
        xf = mixer_out_ffn(
            xf, row(norm_mix_post[layer]), ins, w_out, j,
            row(norm_ffn_pre[layer]), row(norm_ffn_post[layer]),
            ffn_w_gate, ffn_w_up, ffn_w_down, layer)
    return xf.reshape(b, s, d)
```

```python
import functools

import jax
import jax.numpy as jnp
from jax import lax
from jax.experimental import pallas as pl
from jax.experimental.pallas import tpu as pltpu

F32 = jnp.float32
BF16 = jnp.bfloat16

D_MODEL = 1024
HEAD_DIM = 64
CHUNK = 64
N_LEFT_CHUNKS = 8
REL_CLIP = 256
A_W = 512
B_W = 512
LRU_BLOCKS = 4
LRU_BLOCK_W = 256
LRU_C = 8.0
RMS_EPS = 1e-6

LANES = 128
VMEM_LIMIT = 56 * 1024 * 1024

ROW_TILE = 512
SB_TILE = 256
SB_SUBTILES = 2
SB_UNROLL = 1
SB_PAIRS = 4
SB_EXIT = 128.0
A_TILE = 2 * CHUNK
A_SUBTILES = 32
A_WIN = (N_LEFT_CHUNKS + 2) * CHUNK
A_PAD = N_LEFT_CHUNKS * CHUNK
REC_TILE = 512
REC_PITCH = REC_TILE // 8 + 8
NEG_BIG = -1e30
LOG2E = 1.4426950408889634

_NT = (((1,), (1,)), ((), ()))


def _params(sem):
    return pltpu.CompilerParams(dimension_semantics=sem, vmem_limit_bytes=VMEM_LIMIT)


def _rms(x, g):
    ms = jnp.mean(x * x, axis=-1, keepdims=True)
    return x * lax.rsqrt(ms + RMS_EPS) * g


def _sigmoid(x):
    return 1.0 / (1.0 + jnp.exp(-x))


def _resident(shape):
    return pl.BlockSpec(shape, lambda *_: (0,) * len(shape), pipeline_mode=pl.Buffered(1))


def _layer_block(stack, layer, rows=None, row_block=0):
    shape = stack.shape[1:] if rows is None else (rows,) + stack.shape[2:]
    index = (layer, row_block) + (0,) * (len(shape) - 1)
    return pl.BlockSpec((None,) + shape, lambda *_: index, pipeline_mode=pl.Buffered(1))


def _norm_linear_kernel(x_ref, g_ref, w_ref, o_ref):
    h = _rms(x_ref[...], g_ref[...]).astype(BF16)
    o_ref[...] = jnp.dot(h, w_ref[...], preferred_element_type=F32).astype(o_ref.dtype)


def norm_linear(x, g, w_stack, layer, out_dtype):
    m, d = x.shape
    n = w_stack.shape[2]
    return pl.pallas_call(
        _norm_linear_kernel,
        out_shape=jax.ShapeDtypeStruct((m, n), out_dtype),
        grid=(m // ROW_TILE,),
        in_specs=[
            pl.BlockSpec((ROW_TILE, d), lambda i: (i, 0)),
            pl.BlockSpec((1, d), lambda i: (0, 0)),
            _layer_block(w_stack, layer),
        ],
        out_specs=pl.BlockSpec((ROW_TILE, n), lambda i: (i, 0)),
        compiler_params=_params(("arbitrary",)),
        name="norm_linear",
    )(x, g, w_stack)


def _out_ffn_kernel(n_in, *refs):
    x_ref, gmix_ref, gpre_ref, gpost_ref, wg_ref, wu_ref, wd_ref = refs[:7]
    ins = refs[7:7 + n_in]
    ws = refs[7 + n_in:7 + 2 * n_in]
    o_ref = refs[7 + 2 * n_in]
    m = jnp.dot(ins[0][...], ws[0][...], preferred_element_type=F32)
    for a, w in zip(ins[1:], ws[1:]):
        m = m + jnp.dot(a[...], w[...], preferred_element_type=F32)
    x = x_ref[...] + _rms(m, gmix_ref[...])
    h = _rms(x, gpre_ref[...]).astype(BF16)
    gate = jnp.dot(h, wg_ref[...], preferred_element_type=F32)
    up = jnp.dot(h, wu_ref[...], preferred_element_type=F32)
    act = (gate * _sigmoid(gate) * up).astype(BF16)
    f = jnp.dot(act, wd_ref[...], preferred_element_type=F32)
    o_ref[...] = x + _rms(f, gpost_ref[...])


def mixer_out_ffn(x, gmix, ins, w_out_stack, j, gpre, gpost, wg_stack, wu_stack, wd_stack, layer):
    m, d = x.shape
    n_in = len(ins)
    in_specs = [pl.BlockSpec((ROW_TILE, d), lambda i: (i, 0))]
    in_specs += [_resident(a.shape) for a in (gmix, gpre, gpost)]
    in_specs += [_layer_block(w, layer) for w in (wg_stack, wu_stack, wd_stack)]
    in_specs += [pl.BlockSpec((ROW_TILE, a.shape[1]), lambda i: (i, 0)) for a in ins]
    in_specs += [_layer_block(w_out_stack, j, a.shape[1], k) for k, a in enumerate(ins)]
    return pl.pallas_call(
        functools.partial(_out_ffn_kernel, n_in),
        out_shape=jax.ShapeDtypeStruct((m, d), F32),
        grid=(m // ROW_TILE,),
        in_specs=in_specs,
        out_specs=pl.BlockSpec((ROW_TILE, d), lambda i: (i, 0)),
        compiler_params=_params(("arbitrary",)),
        name="mixer_out_ffn",
    )(x, gmix, gpre, gpost, wg_stack, wu_stack, wd_stack, *ins, *([w_out_stack] * n_in))


def _band_attn_kernel(q_ref, k_ref, v_ref, bias_ref, o_ref, kp_ref, vp_ref, s_ref, p_ref, rinv_ref):
    t = pl.program_id(2)

    @pl.when(t == 0)
    def _():
        kp_ref[0:A_PAD, :] = jnp.zeros((A_PAD, LANES), BF16)
        vp_ref[0:A_PAD, :] = jnp.zeros((A_PAD, LANES), BF16)
        kp_ref[A_PAD:, :] = k_ref[0]
        vp_ref[A_PAD:, :] = v_ref[0]

    lane = lax.broadcasted_iota(jnp.int32, (1, LANES), 1)
    first = lane < HEAD_DIM
    col = lax.broadcasted_iota(jnp.int32, (1, A_WIN), 1)
    bias = jnp.concatenate([bias_ref[0], bias_ref[1]], axis=0)
    zero = jnp.zeros((A_TILE, LANES), BF16)

    def window(u):
        tile = t * A_SUBTILES + u
        return tile, pl.multiple_of(tile * A_TILE, A_TILE)

    def scores(u, slot, head_of_sequence):
        tile, start = window(u)
        kw = kp_ref[pl.ds(start, A_WIN), :]
        q2 = q_ref[0, pl.ds(pl.multiple_of(u * A_TILE, A_TILE), A_TILE), :] * (HEAD_DIM ** -0.5)
        qs = jnp.concatenate([jnp.where(first, q2, zero), jnp.where(first, zero, q2)], axis=0)
        s = lax.dot_general(qs, kw, _NT, preferred_element_type=F32) + bias
        if head_of_sequence:
            s = jnp.where(col >= (A_PAD - tile * A_TILE), s, NEG_BIG)
        s_ref[slot] = s

    def softmax(slot):
        s = s_ref[slot]
        p = jnp.exp(s - jnp.max(s, axis=-1, keepdims=True))
        p_ref[slot] = p.astype(BF16)
        rinv_ref[slot] = jnp.broadcast_to(1.0 / jnp.sum(p, axis=-1, keepdims=True), (2 * A_TILE, LANES))

    def values(u, slot):
        _, start = window(u)
        vw = vp_ref[pl.ds(start, A_WIN), :]
        o = jnp.dot(p_ref[slot], vw, preferred_element_type=F32) * rinv_ref[slot]
        out = jnp.where(first, o[:A_TILE], o[A_TILE:])
        o_ref[0, pl.ds(pl.multiple_of(u * A_TILE, A_TILE), A_TILE), :] = out.astype(o_ref.dtype)

    scores(0, 0, True)
    scores(1, 1, True)
    softmax(0)

    def pair(k, head_of_sequence):
        values(2 * k - 2, 0)
        softmax(1)
        scores(2 * k, 0, head_of_sequence)
        values(2 * k - 1, 1)
        softmax(0)
        scores(2 * k + 1, 1, head_of_sequence)

    n_head = A_PAD // A_TILE // 2
    for k in range(1, n_head):
        pair(k, True)

    def body(k, _):
        pair(k, False)
        return 0

    lax.fori_loop(n_head, A_SUBTILES // 2, body, 0)
    values(A_SUBTILES - 2, 0)
    softmax(1)
    values(A_SUBTILES - 1, 1)


def band_attention(proj, bias, col0):
    b, s, _ = proj.shape
    nq = col0 // LANES
    nk = (col0 + A_W) // LANES
    nv = (col0 + 2 * A_W) // LANES
    tq = A_SUBTILES * A_TILE
    return pl.pallas_call(
        _band_attn_kernel,
        out_shape=jax.ShapeDtypeStruct((b, s, A_W), BF16),
        grid=(b, A_W // LANES, s // tq),
        in_specs=[
            pl.BlockSpec((1, tq, LANES), lambda bi, p, t: (bi, t, nq + p)),
            pl.BlockSpec((1, s, LANES), lambda bi, p, t: (bi, 0, nk + p)),
            pl.BlockSpec((1, s, LANES), lambda bi, p, t: (bi, 0, nv + p)),
            pl.BlockSpec((2, A_TILE, A_WIN), lambda bi, p, t: (p, 0, 0)),
        ],
        out_specs=pl.BlockSpec((1, tq, LANES), lambda bi, p, t: (bi, t, p)),
        scratch_shapes=[
            pltpu.VMEM((s + A_PAD, LANES), BF16),
            pltpu.VMEM((s + A_PAD, LANES), BF16),
            pltpu.VMEM((2, 2 * A_TILE, A_WIN), F32),
            pltpu.VMEM((2, 2 * A_TILE, A_WIN), BF16),
            pltpu.VMEM((2, 2 * A_TILE, LANES), F32),
        ],
        compiler_params=_params(("arbitrary", "arbitrary", "arbitrary")),
        name="band_attention",
    )(proj, proj, proj, bias)


def band_bias(rel_bias):
    h = rel_bias.shape[0]
    n_f = A_WIN + A_TILE - 1
    n_const = A_PAD + A_TILE - 1 - REL_CLIP
    far = jnp.broadcast_to(rel_bias[:, 2 * REL_CLIP:], (h, n_const))
    near = rel_bias[:, 2 * REL_CLIP + 1 - (n_f - n_const):][:, ::-1]
    f = jnp.concatenate([far, near], axis=1)
    period = n_f + 1
    g = jnp.concatenate([f[:, A_TILE - 1:], jnp.zeros((h, 1), F32), f[:, :A_TILE - 1]], axis=1)
    flat = jnp.tile(g, (1, A_TILE))[:, :A_TILE * (period - 1)]
    toep = flat.reshape(h, A_TILE, period - 1)[:, :, :A_WIN]
    r = jnp.arange(A_TILE)[:, None]
    c = jnp.arange(A_WIN)[None, :]
    dchunk = c // CHUNK - r // CHUNK
    ok = (dchunk >= 0) & (dchunk <= N_LEFT_CHUNKS)
    return jnp.where(ok[None], toep, NEG_BIG).astype(F32)


def _stick_kernel(q_ref, k_ref, v_ref, o_ref, acc_ref, carry_ref, z_ref, sp_ref):
    qi = pl.program_id(2)
    t = SB_TILE
    ns = SB_SUBTILES
    lane = lax.broadcasted_iota(jnp.int32, (1, LANES), 1)
    first = lane < HEAD_DIM
    row = lax.broadcasted_iota(jnp.int32, (t, t), 0)
    col = lax.broadcasted_iota(jnp.int32, (t, t), 1)
    causal = col < row
    true = jnp.ones((t, t), jnp.bool_)
    suffix = jnp.where(row >= col, 1.0, 0.0).astype(BF16)
    zero = jnp.zeros((t, LANES), BF16)

    def one_pair(lanes):
        q2 = q_ref[0, :, lanes] * (HEAD_DIM ** -0.5)
        acc_ref[...] = jnp.zeros_like(acc_ref)
        carry_ref[...] = jnp.zeros_like(carry_ref)

        def split_heads(x):
            return jnp.concatenate([jnp.where(first, x, zero), jnp.where(first, zero, x)], axis=0)

        def scores(j, d, mask):
            start = pl.multiple_of(j * t, t)
            kcat = split_heads(k_ref[0, pl.ds(start, t), lanes])
            z = lax.dot_general(q2[d * t:], kcat, _NT, preferred_element_type=F32)
            sp = jnp.maximum(z, 0.0) + jnp.log(1.0 + jnp.exp2(jnp.abs(z) * (-LOG2E)))
            if mask is not None:
                sp = jnp.where(mask, sp, 0.0)
            spb = sp.astype(BF16)
            return z - (sp - spb.astype(F32)), spb

        def weights(j, d, mask, tz, spb):
            r0 = d * t
            n = (ns - d) * t
            start = pl.multiple_of(j * t, t)
            vcat = split_heads(v_ref[0, pl.ds(start, t), lanes])
            c = [jnp.dot(spb[:, h * t:(h + 1) * t], suffix, preferred_element_type=F32) for h in range(2)]
            carry = [carry_ref[h, r0:, :] for h in range(2)]
            reps = t // LANES
            shift = jnp.concatenate([c[0]] + [c[1]], axis=1) + jnp.concatenate(
                [carry[0]] * reps + [carry[1]] * reps, axis=1)
            w = jnp.exp(tz - shift)
            if mask is not None:
                w = jnp.where(mask, w, 0.0)
            for h in range(2):
                carry_ref[h, r0:, :] = carry[h] + jnp.broadcast_to(c[h][:, 0:1], (n, LANES))
            acc_ref[r0:, :] += jnp.dot(w.astype(BF16), vcat, preferred_element_type=F32)

        for d in range(ns - 1, -1, -1):
            mask = jnp.concatenate([causal] + [true] * (ns - d - 1), axis=0)
            mask = jnp.concatenate([mask, mask], axis=1)
            tz, spb = scores(ns * qi + d, d, mask)
            weights(ns * qi + d, d, mask, tz, spb)

        def stage1(i, slot):
            tz, spb = scores(ns * qi - 1 - i, 0, None)
            z_ref[slot] = tz
            sp_ref[slot] = spb

        def stage2(i, slot):
            weights(ns * qi - 1 - i, 0, None, z_ref[slot], sp_ref[slot])

        @pl.when(qi > 0)
        def _():
            n_groups = (ns * qi) // SB_UNROLL

            def cond(state):
                g, done = state
                return jnp.logical_and(g < n_groups, done == 0)

            def body(state):
                g, _ = state
                i0 = SB_UNROLL * g
                stage1(i0, 0)
                for off in range(SB_UNROLL - 1):
                    stage1(i0 + off + 1, (off + 1) % 2)
                    stage2(i0 + off, off % 2)
                stage2(i0 + SB_UNROLL - 1, (SB_UNROLL - 1) % 2)
                low = jnp.min(jnp.minimum(carry_ref[0], carry_ref[1]), axis=0, keepdims=True)
                return g + 1, (jnp.min(low) >= SB_EXIT).astype(jnp.int32)

            lax.while_loop(cond, body, (jnp.int32(0), jnp.int32(0)))

        o_ref[0, :, lanes] = acc_ref[...].astype(o_ref.dtype)

    for pair in range(SB_PAIRS):
        one_pair(slice(pair * LANES, (pair + 1) * LANES))


def stick_breaking_attention(proj, col0):
    b, s, _ = proj.shape
    width = SB_PAIRS * LANES
    nq = col0 // width
    nk = (col0 + B_W) // width
    nv = (col0 + 2 * B_W) // width
    tq = SB_SUBTILES * SB_TILE
    return pl.pallas_call(
        _stick_kernel,
        out_shape=jax.ShapeDtypeStruct((b, s, B_W), BF16),
        grid=(b, B_W // width, s // tq),
        in_specs=[
            pl.BlockSpec((1, tq, width), lambda bi, p, t: (bi, t, nq + p)),
            pl.BlockSpec((1, s, width), lambda bi, p, t: (bi, 0, nk + p)),
            pl.BlockSpec((1, s, width), lambda bi, p, t: (bi, 0, nv + p)),
        ],
        out_specs=pl.BlockSpec((1, tq, width), lambda bi, p, t: (bi, t, p)),
        scratch_shapes=[
            pltpu.VMEM((tq, LANES), F32),
            pltpu.VMEM((2, tq, LANES), F32),
            pltpu.VMEM((2, tq, 2 * SB_TILE), F32),
            pltpu.VMEM((2, tq, 2 * SB_TILE), BF16),
        ],
        compiler_params=_params(("arbitrary", "arbitrary", "arbitrary")),
        name="stick_breaking",
    )(proj, proj, proj)


def _rglru_kernel(x_ref, g_ref, win_ref, cw_ref, cb_ref, wa_ref, ba_ref, wi_ref, bi_ref, lam_ref,
                  y_ref, xpad_ref, a_ref, u_ref, p_ref, hl_ref, h_ref):
    t = pl.program_id(1)
    n = REC_TILE
    steps = n // 8
    width = y_ref.shape[2]
    n_slabs = width // LANES

    @pl.when(t == 0)
    def _():
        xpad_ref[0:8, :] = jnp.zeros((8, width), F32)
        h_ref[...] = jnp.zeros_like(h_ref)

    hn = _rms(x_ref[0], g_ref[...]).astype(BF16)
    gate = jnp.dot(hn, win_ref[:, :width], preferred_element_type=F32)
    c = 0.7978845608028654
    gate = 0.5 * gate * (1.0 + jnp.tanh(c * (gate + 0.044715 * (gate * gate * gate))))
    xpad_ref[8:8 + n, :] = jnp.dot(hn, win_ref[:, width:], preferred_element_type=F32)
    cw = cw_ref[...]
    xc = (cb_ref[...]
          + cw[3:4] * xpad_ref[8:8 + n, :]
          + cw[2:3] * xpad_ref[7:7 + n, :]
          + cw[1:2] * xpad_ref[6:6 + n, :]
          + cw[0:1] * xpad_ref[5:5 + n, :])
    xpad_ref[0:8, :] = xpad_ref[n:n + 8, :]
    xcb = xc.astype(BF16)
    lam = lam_ref[...]
    sp_lam = jnp.maximum(-lam, 0.0) + jnp.log1p(jnp.exp(-jnp.abs(lam)))
    for blk in range(LRU_BLOCKS):
        sl = slice(blk * LRU_BLOCK_W, (blk + 1) * LRU_BLOCK_W)
        xb = xcb[:, sl]
        pre_r = jnp.dot(xb, wa_ref[blk], preferred_element_type=F32) + ba_ref[:, sl]
        pre_i = jnp.dot(xb, wi_ref[blk], preferred_element_type=F32) + bi_ref[:, sl]
        r = 0.5 * jnp.tanh(0.5 * pre_r) + 0.5
        i = 0.5 * jnp.tanh(0.5 * pre_i) + 0.5
        log_a = (-LRU_C * r) * sp_lam[:, sl]
        a = jnp.exp(log_a)
        m2 = -jnp.tanh(log_a) * (a * a + 1.0)
        mult = m2 * lax.rsqrt(jnp.maximum(m2, 1e-30))
        u = mult * (i * xc[:, sl])
        for k in range(LRU_BLOCK_W // LANES):
            slab = blk * (LRU_BLOCK_W // LANES) + k
            for ch in range(8):
                rows = slice(ch * steps, (ch + 1) * steps)
                dst = slice(ch * REC_PITCH, ch * REC_PITCH + steps)
                a_ref[slab, dst, :] = a[rows, k * LANES:(k + 1) * LANES]
                u_ref[slab, dst, :] = u[rows, k * LANES:(k + 1) * LANES]

    def step(l, state):
        rows = pl.ds(l, 8, stride=REC_PITCH)
        new = []
        for slab in range(n_slabs):
            h, p = state[slab]
            av = a_ref[slab, rows, :]
            h = av * h + u_ref[slab, rows, :]
            p = p * av
            hl_ref[slab, rows, :] = h
            p_ref[slab, rows, :] = p
            new.append((h, p))
        return tuple(new)

    zeros = jnp.zeros((8, LANES), F32)
    ones = jnp.ones((8, LANES), F32)
    ends = lax.fori_loop(0, steps, step, tuple((zeros, ones) for _ in range(n_slabs)), unroll=4)

    rowid = lax.broadcasted_iota(jnp.int32, (8, LANES), 0)
    for slab in range(n_slabs):
        lanes = slice(slab * LANES, (slab + 1) * LANES)
        b, a = ends[slab]
        for d in (1, 2, 4):
            keep = rowid >= d
            a_sh = jnp.where(keep, pltpu.roll(a, d, 0), 1.0)
            b_sh = jnp.where(keep, pltpu.roll(b, d, 0), 0.0)
            b = a * b_sh + b
            a = a * a_sh
        h_in = h_ref[:, lanes]
        after = b + a * h_in
        h_ref[:, lanes] = after[7:8, :]
        start = jnp.where(rowid >= 1, pltpu.roll(after, 1, 0), h_in)
        for ch in range(8):
            src = slice(ch * REC_PITCH, ch * REC_PITCH + steps)
            rows = slice(ch * steps, (ch + 1) * steps)
            hs = hl_ref[slab, src, :] + p_ref[slab, src, :] * start[ch:ch + 1, :]
            y_ref[0, rows, lanes] = (hs * gate[rows, lanes]).astype(y_ref.dtype)


def rglru_mixer(x, g, w_in_stack, conv_w, conv_b, w_a_stack, b_a, w_i_stack, b_i, lam, j):
    b, s, d = x.shape
    width = w_in_stack.shape[2] // 2
    const2 = lambda bi, t: (0, 0)
    scan_rows = 8 * REC_PITCH
    return pl.pallas_call(
        _rglru_kernel,
        out_shape=jax.ShapeDtypeStruct((b, s, width), BF16),
        grid=(b, s // REC_TILE),
        in_specs=[
            pl.BlockSpec((1, REC_TILE, d), lambda bi, t: (bi, t, 0)),
            pl.BlockSpec((1, d), const2),
            _layer_block(w_in_stack, j),
            pl.BlockSpec(conv_w.shape, const2),
            pl.BlockSpec((1, width), const2),
            _layer_block(w_a_stack, j),
            pl.BlockSpec((1, width), const2),
            _layer_block(w_i_stack, j),
            pl.BlockSpec((1, width), const2),
            pl.BlockSpec((1, width), const2),
        ],
        out_specs=pl.BlockSpec((1, REC_TILE, width), lambda bi, t: (bi, t, 0)),
        scratch_shapes=[
            pltpu.VMEM((REC_TILE + 8, width), F32),
            pltpu.VMEM((width // LANES, scan_rows, LANES), F32),
            pltpu.VMEM((width // LANES, scan_rows, LANES), F32),
            pltpu.VMEM((width // LANES, scan_rows, LANES), F32),
            pltpu.VMEM((width // LANES, scan_rows, LANES), F32),
            pltpu.VMEM((1, width), F32),
        ],
        compiler_params=_params(("arbitrary", "arbitrary")),
        name="rglru_mixer",
    )(x, g, w_in_stack, conv_w, conv_b, w_a_stack, b_a, w_i_stack, b_i, lam)


def kernel(x, attn_w_in, attn_rel_bias, attn_w_out, rg_w_in, rg_conv_w, rg_conv_b, rg_w_a, rg_b_a,
           rg_w_i, rg_b_i, rg_lambda, rg_w_out, norm_mix_pre, norm_mix_post, norm_ffn_pre,
           norm_ffn_post, ffn_w_gate, ffn_w_up, ffn_w_down):
    b, s, d = x.shape
    depth = norm_mix_pre.shape[0]
    xf = x.reshape(b * s, d)
    row = lambda v: v.reshape(1, -1)
    attn_w_in, attn_w_out, rg_w_in, rg_w_a, rg_w_i, rg_w_out, ffn_w_gate, ffn_w_up, ffn_w_down = (
        w.astype(BF16) for w in (attn_w_in, attn_w_out, rg_w_in, rg_w_a, rg_w_i, rg_w_out,
                                 ffn_w_gate, ffn_w_up, ffn_w_down))
    for layer in range(depth):
        j = layer // 2
        g_pre = row(norm_mix_pre[layer])
        if layer % 2 == 0:
            proj = norm_linear(xf, g_pre, attn_w_in, j, BF16).reshape(b, s, -1)
            out_a = band_attention(proj, band_bias(attn_rel_bias[j]), 0)
            out_b = stick_breaking_attention(proj, 3 * A_W)
            ins = [out_a.reshape(b * s, A_W), out_b.reshape(b * s, B_W)]
            w_out = attn_w_out
        else:
            width = rg_w_out.shape[1]
            y = rglru_mixer(
                xf.reshape(b, s, d), g_pre, rg_w_in,
                rg_conv_w[j].reshape(-1, width), row(rg_conv_b[j]),
                rg_w_a, row(rg_b_a[j]), rg_w_i, row(rg_b_i[j]), row(rg_lambda[j]), j)
            ins = [y.reshape(b * s, width)]
            w_out = rg_w_out
        xf = mixer_out_ffn(
            xf, row(norm_mix_post[layer]), ins, w_out, j,
            row(norm_ffn_pre[layer]), row(norm_ffn_post[layer]),
            ffn_w_gate, ffn_w_up, ffn_w_down, layer)
    return xf.reshape(b, s, d)
```

```python
import functools

import jax
import jax.numpy as jnp
from jax import lax
from jax.experimental import pallas as pl
from jax.experimental.pallas import tpu as pltpu

F32 = jnp.float32
BF16 = jnp.bfloat16

D_MODEL = 1024
HEAD_DIM = 64
CHUNK = 64
N_LEFT_CHUNKS = 8
REL_CLIP = 256
A_W = 512
B_W = 512
LRU_BLOCKS = 4
LRU_BLOCK_W = 256
LRU_C = 8.0
RMS_EPS = 1e-6

LANES = 128
VMEM_LIMIT = 56 * 1024 * 1024

ROW_TILE = 512
SB_TILE = 256
SB_SUBTILES = 2
SB_UNROLL = 1
SB_PAIRS = 4
SB_EXIT = 128.0
A_TILE = 2 * CHUNK
A_SUBTILES = 32
A_WIN = (N_LEFT_CHUNKS + 2) * CHUNK
A_PAD = N_LEFT_CHUNKS * CHUNK
REC_TILE = 512
REC_PITCH = REC_TILE // 8 + 8
NEG_BIG = -1e30
LOG2E = 1.4426950408889634

_NT = (((1,), (1,)), ((), ()))


def _params(sem):
    return pltpu.CompilerParams(dimension_semantics=sem, vmem_limit_bytes=VMEM_LIMIT)


def _rms(x, g):
    ms = jnp.mean(x * x, axis=-1, keepdims=True)
    return x * lax.rsqrt(ms + RMS_EPS) * g


def _sigmoid(x):
    return 1.0 / (1.0 + jnp.exp(-x))


def _resident(shape):
    return pl.BlockSpec(shape, lambda *_: (0,) * len(shape), pipeline_mode=pl.Buffered(1))


def _layer_block(stack, layer, rows=None, row_block=0):
    shape = stack.shape[1:] if rows is None else (rows,) + stack.shape[2:]
    index = (layer, row_block) + (0,) * (len(shape) - 1)
    return pl.BlockSpec((None,) + shape, lambda *_: index, pipeline_mode=pl.Buffered(1))


def _norm_linear_kernel(x_ref, g_ref, w_ref, o_ref):
    h = _rms(x_ref[...], g_ref[...]).astype(BF16)
    o_ref[...] = jnp.dot(h, w_ref[...], preferred_element_type=F32).astype(o_ref.dtype)


def norm_linear(x, g, w_stack, layer, out_dtype):
    m, d = x.shape
    n = w_stack.shape[2]
    return pl.pallas_call(
        _norm_linear_kernel,
        out_shape=jax.ShapeDtypeStruct((m, n), out_dtype),
        grid=(m // ROW_TILE,),
        in_specs=[
            pl.BlockSpec((ROW_TILE, d), lambda i: (i, 0)),
            pl.BlockSpec((1, d), lambda i: (0, 0)),
            _layer_block(w_stack, layer),
        ],
        out_specs=pl.BlockSpec((ROW_TILE, n), lambda i: (i, 0)),
        compiler_params=_params(("arbitrary",)),
        name="norm_linear",
    )(x, g, w_stack)


def _out_ffn_kernel(n_in, *refs):
    x_ref, gmix_ref, gpre_ref, gpost_ref, wg_ref, wu_ref, wd_ref = refs[:7]
    ins = refs[7:7 + n_in]
    ws = refs[7 + n_in:7 + 2 * n_in]
    o_ref = refs[7 + 2 * n_in]
    m = jnp.dot(ins[0][...], ws[0][...], preferred_element_type=F32)
    for a, w in zip(ins[1:], ws[1:]):
        m = m + jnp.dot(a[...], w[...], preferred_element_type=F32)
    x = x_ref[...] + _rms(m, gmix_ref[...])
    h = _rms(x, gpre_ref[...]).astype(BF16)
    gate = jnp.dot(h, wg_ref[...], preferred_element_type=F32)
    up = jnp.dot(h, wu_ref[...], preferred_element_type=F32)
    act = (gate * _sigmoid(gate) * up).astype(BF16)
    f = jnp.dot(act, wd_ref[...], preferred_element_type=F32)
    o_ref[...] = x + _rms(f, gpost_ref[...])


def mixer_out_ffn(x, gmix, ins, w_out_stack, j, gpre, gpost, wg_stack, wu_stack, wd_stack, layer):
    m, d = x.shape
    n_in = len(ins)
    in_specs = [pl.BlockSpec((ROW_TILE, d), lambda i: (i, 0))]
    in_specs += [_resident(a.shape) for a in (gmix, gpre, gpost)]
    in_specs += [_layer_block(w, layer) for w in (wg_stack, wu_stack, wd_stack)]
    in_specs += [pl.BlockSpec((ROW_TILE, a.shape[1]), lambda i: (i, 0)) for a in ins]
    in_specs += [_layer_block(w_out_stack, j, a.shape[1], k) for k, a in enumerate(ins)]
    return pl.pallas_call(
        functools.partial(_out_ffn_kernel, n_in),
        out_shape=jax.ShapeDtypeStruct((m, d), F32),
        grid=(m // ROW_TILE,),
        in_specs=in_specs,
        out_specs=pl.BlockSpec((ROW_TILE, d), lambda i: (i, 0)),
        compiler_params=_params(("arbitrary",)),
        name="mixer_out_ffn",
    )(x, gmix, gpre, gpost, wg_stack, wu_stack, wd_stack, *ins, *([w_out_stack] * n_in))


def _band_attn_kernel(q_ref, k_ref, v_ref, bias_ref, o_ref, kp_ref, vp_ref, s_ref, p_ref, rinv_ref):
    t = pl.program_id(2)

    @pl.when(t == 0)
    def _():
        kp_ref[0:A_PAD, :] = jnp.zeros((A_PAD, LANES), BF16)
        vp_ref[0:A_PAD, :] = jnp.zeros((A_PAD, LANES), BF16)
        kp_ref[A_PAD:, :] = k_ref[0]
        vp_ref[A_PAD:, :] = v_ref[0]

    lane = lax.broadcasted_iota(jnp.int32, (1, LANES), 1)
    first = lane < HEAD_DIM
    col = lax.broadcasted_iota(jnp.int32, (1, A_WIN), 1)
    bias = jnp.concatenate([bias_ref[0], bias_ref[1]], axis=0)
    zero = jnp.zeros((A_TILE, LANES), BF16)

    def window(u):
        tile = t * A_SUBTILES + u
        return tile, pl.multiple_of(tile * A_TILE, A_TILE)

    def scores(u, slot, head_of_sequence):
        tile, start = window(u)
        kw = kp_ref[pl.ds(start, A_WIN), :]
        q2 = q_ref[0, pl.ds(pl.multiple_of(u * A_TILE, A_TILE), A_TILE), :] * (HEAD_DIM ** -0.5)
        qs = jnp.concatenate([jnp.where(first, q2, zero), jnp.where(first, zero, q2)], axis=0)
        s = lax.dot_general(qs, kw, _NT, preferred_element_type=F32) + bias
        if head_of_sequence:
            s = jnp.where(col >= (A_PAD - tile * A_TILE), s, NEG_BIG)
        s_ref[slot] = s

    def softmax(slot):
        s = s_ref[slot]
        p = jnp.exp(s - jnp.max(s, axis=-1, keepdims=True))
        p_ref[slot] = p.astype(BF16)
        rinv_ref[slot] = jnp.broadcast_to(1.0 / jnp.sum(p, axis=-1, keepdims=True), (2 * A_TILE, LANES))

    def values(u, slot):
        _, start = window(u)
        vw = vp_ref[pl.ds(start, A_WIN), :]
        o = jnp.dot(p_ref[slot], vw, preferred_element_type=F32) * rinv_ref[slot]
        out = jnp.where(first, o[:A_TILE], o[A_TILE:])
        o_ref[0, pl.ds(pl.multiple_of(u * A_TILE, A_TILE), A_TILE), :] = out.astype(o_ref.dtype)

    scores(0, 0, True)
    scores(1, 1, True)
    softmax(0)

    def pair(k, head_of_sequence):
        values(2 * k - 2, 0)
        softmax(1)
        scores(2 * k, 0, head_of_sequence)
        values(2 * k - 1, 1)
        softmax(0)
        scores(2 * k + 1, 1, head_of_sequence)

    n_head = A_PAD // A_TILE // 2
    for k in range(1, n_head):
        pair(k, True)

    def body(k, _):
        pair(k, False)
        return 0

    lax.fori_loop(n_head, A_SUBTILES // 2, body, 0)
    values(A_SUBTILES - 2, 0)
    softmax(1)
    values(A_SUBTILES - 1, 1)


def band_attention(proj, bias, col0):
    b, s, _ = proj.shape
    nq = col0 // LANES
    nk = (col0 + A_W) // LANES
    nv = (col0 + 2 * A_W) // LANES
    tq = A_SUBTILES * A_TILE
    return pl.pallas_call(
        _band_attn_kernel,
        out_shape=jax.ShapeDtypeStruct((b, s, A_W), BF16),
        grid=(b, A_W // LANES, s // tq),
        in_specs=[
            pl.BlockSpec((1, tq, LANES), lambda bi, p, t: (bi, t, nq + p)),
            pl.BlockSpec((1, s, LANES), lambda bi, p, t: (bi, 0, nk + p)),
            pl.BlockSpec((1, s, LANES), lambda bi, p, t: (bi, 0, nv + p)),
            pl.BlockSpec((2, A_TILE, A_WIN), lambda bi, p, t: (p, 0, 0)),
        ],
        out_specs=pl.BlockSpec((1, tq, LANES), lambda bi, p, t: (bi, t, p)),
        scratch_shapes=[
            pltpu.VMEM((s + A_PAD, LANES), BF16),
            pltpu.VMEM((s + A_PAD, LANES), BF16),
            pltpu.VMEM((2, 2 * A_TILE, A_WIN), F32),
            pltpu.VMEM((2, 2 * A_TILE, A_WIN), BF16),
            pltpu.VMEM((2, 2 * A_TILE, LANES), F32),
        ],
        compiler_params=_params(("arbitrary", "arbitrary", "arbitrary")),
        name="band_attention",
    )(proj, proj, proj, bias)


def band_bias(rel_bias):
    h = rel_bias.shape[0]
    n_f = A_WIN + A_TILE - 1
    n_const = A_PAD + A_TILE - 1 - REL_CLIP
    far = jnp.broadcast_to(rel_bias[:, 2 * REL_CLIP:], (h, n_const))
    near = rel_bias[:, 2 * REL_CLIP + 1 - (n_f - n_const):][:, ::-1]
    f = jnp.concatenate([far, near], axis=1)
    period = n_f + 1
    g = jnp.concatenate([f[:, A_TILE - 1:], jnp.zeros((h, 1), F32), f[:, :A_TILE - 1]], axis=1)
    flat = jnp.tile(g, (1, A_TILE))[:, :A_TILE * (period - 1)]
    toep = flat.reshape(h, A_TILE, period - 1)[:, :, :A_WIN]
    r = jnp.arange(A_TILE)[:, None]
    c = jnp.arange(A_WIN)[None, :]
    dchunk = c // CHUNK - r // CHUNK
    ok = (dchunk >= 0) & (dchunk <= N_LEFT_CHUNKS)
    return jnp.where(ok[None], toep, NEG_BIG).astype(F32)


def _stick_kernel(q_ref, k_ref, v_ref, o_ref, acc_ref, carry_ref, z_ref, sp_ref):
    qi = pl.program_id(2)
    t = SB_TILE
    ns = SB_SUBTILES
    lane = lax.broadcasted_iota(jnp.int32, (1, LANES), 1)
    first = lane < HEAD_DIM
    row = lax.broadcasted_iota(jnp.int32, (t, t), 0)
    col = lax.broadcasted_iota(jnp.int32, (t, t), 1)
    causal = col < row
    true = jnp.ones((t, t), jnp.bool_)
    suffix = jnp.where(row >= col, 1.0, 0.0).astype(BF16)
    zero = jnp.zeros((t, LANES), BF16)

    acc_ref[...] = jnp.zeros_like(acc_ref)
    carry_ref[...] = jnp.zeros_like(carry_ref)

    def lanes_of(pair):
        return slice(pair * LANES, (pair + 1) * LANES)

    def split_heads(x):
        return jnp.concatenate([jnp.where(first, x, zero), jnp.where(first, zero, x)], axis=0)

    def scores(pair, j, d, mask):
        start = pl.multiple_of(j * t, t)
        kcat = split_heads(k_ref[0, pl.ds(start, t), lanes_of(pair)])
        q2 = q_ref[0, d * t:, lanes_of(pair)] * (HEAD_DIM ** -0.5)
        z = lax.dot_general(q2, kcat, _NT, preferred_element_type=F32)
        sp = jnp.maximum(z, 0.0) + jnp.log(1.0 + jnp.exp2(jnp.abs(z) * (-LOG2E)))
        if mask is not None:
            sp = jnp.where(mask, sp, 0.0)
        spb = sp.astype(BF16)
        return z - (sp - spb.astype(F32)), spb

    def weights(pair, j, d, mask, tz, spb):
        r0 = d * t
        n = (ns - d) * t
        start = pl.multiple_of(j * t, t)
        vcat = split_heads(v_ref[0, pl.ds(start, t), lanes_of(pair)])
        c = [jnp.dot(spb[:, h * t:(h + 1) * t], suffix, preferred_element_type=F32) for h in range(2)]
        carry = [carry_ref[pair, h, r0:, :] for h in range(2)]
        reps = t // LANES
        shift = jnp.concatenate([c[0]] + [c[1]], axis=1) + jnp.concatenate(
            [carry[0]] * reps + [carry[1]] * reps, axis=1)
        w = jnp.exp(tz - shift)
        if mask is not None:
            w = jnp.where(mask, w, 0.0)
        for h in range(2):
            carry_ref[pair, h, r0:, :] = carry[h] + jnp.broadcast_to(c[h][:, 0:1], (n, LANES))
        acc_ref[pair, r0:, :] += jnp.dot(w.astype(BF16), vcat, preferred_element_type=F32)

    for pair in range(SB_PAIRS):
        for d in range(ns - 1, -1, -1):
            mask = jnp.concatenate([causal] + [true] * (ns - d - 1), axis=0)
            mask = jnp.concatenate([mask, mask], axis=1)
            tz, spb = scores(pair, ns * qi + d, d, mask)
            weights(pair, ns * qi + d, d, mask, tz, spb)

    def stage1(pair, i, slot):
        tz, spb = scores(pair, ns * qi - 1 - i, 0, None)
        z_ref[slot] = tz
        sp_ref[slot] = spb

    def stage2(pair, i, slot):
        weights(pair, ns * qi - 1 - i, 0, None, z_ref[slot], sp_ref[slot])

    @pl.when(qi > 0)
    def _():
        n_groups = (ns * qi) // SB_UNROLL
        for pair in range(SB_PAIRS):
            def cond(state):
                g, done = state
                return jnp.logical_and(g < n_groups, done == 0)

            def body(state, pair=pair):
                g, _ = state
                i0 = SB_UNROLL * g
                stage1(pair, i0, 0)
                for off in range(SB_UNROLL - 1):
                    stage1(pair, i0 + off + 1, (off + 1) % 2)
                    stage2(pair, i0 + off, off % 2)
                stage2(pair, i0 + SB_UNROLL - 1, (SB_UNROLL - 1) % 2)
                low = jnp.min(jnp.minimum(carry_ref[pair, 0], carry_ref[pair, 1]), axis=0, keepdims=True)
                return g + 1, (jnp.min(low) >= SB_EXIT).astype(jnp.int32)

            lax.while_loop(cond, body, (jnp.int32(0), jnp.int32(0)))

    for pair in range(SB_PAIRS):
        o_ref[0, :, lanes_of(pair)] = acc_ref[pair].astype(o_ref.dtype)


def stick_breaking_attention(proj, col0):
    b, s, _ = proj.shape
    width = SB_PAIRS * LANES
    nq = col0 // width
    nk = (col0 + B_W) // width
    nv = (col0 + 2 * B_W) // width
    tq = SB_SUBTILES * SB_TILE
    return pl.pallas_call(
        _stick_kernel,
        out_shape=jax.ShapeDtypeStruct((b, s, B_W), BF16),
        grid=(b, B_W // width, s // tq),
        in_specs=[
            pl.BlockSpec((1, tq, width), lambda bi, p, t: (bi, t, nq + p)),
            pl.BlockSpec((1, s, width), lambda bi, p, t: (bi, 0, nk + p)),
            pl.BlockSpec((1, s, width), lambda bi, p, t: (bi, 0, nv + p)),
        ],
        out_specs=pl.BlockSpec((1, tq, width), lambda bi, p, t: (bi, t, p)),
        scratch_shapes=[
            pltpu.VMEM((SB_PAIRS, tq, LANES), F32),
            pltpu.VMEM((SB_PAIRS, 2, tq, LANES), F32),
            pltpu.VMEM((2, tq, 2 * SB_TILE), F32),
            pltpu.VMEM((2, tq, 2 * SB_TILE), BF16),
        ],
        compiler_params=_params(("arbitrary", "arbitrary", "arbitrary")),
        name="stick_breaking",
    )(proj, proj, proj)


def _rglru_kernel(x_ref, g_ref, win_ref, cw_ref, cb_ref, wa_ref, ba_ref, wi_ref, bi_ref, lam_ref,
                  y_ref, xpad_ref, a_ref, u_ref, p_ref, hl_ref, h_ref):
    t = pl.program_id(1)
    n = REC_TILE
    steps = n // 8
    width = y_ref.shape[2]
    n_slabs = width // LANES

    @pl.when(t == 0)
    def _():
        xpad_ref[0:8, :] = jnp.zeros((8, width), F32)
        h_ref[...] = jnp.zeros_like(h_ref)

    hn = _rms(x_ref[0], g_ref[...]).astype(BF16)
    gate = jnp.dot(hn, win_ref[:, :width], preferred_element_type=F32)
    c = 0.7978845608028654
    gate = 0.5 * gate * (1.0 + jnp.tanh(c * (gate + 0.044715 * (gate * gate * gate))))
    xpad_ref[8:8 + n, :] = jnp.dot(hn, win_ref[:, width:], preferred_element_type=F32)
    cw = cw_ref[...]
    xc = (cb_ref[...]
          + cw[3:4] * xpad_ref[8:8 + n, :]
          + cw[2:3] * xpad_ref[7:7 + n, :]
          + cw[1:2] * xpad_ref[6:6 + n, :]
          + cw[0:1] * xpad_ref[5:5 + n, :])
    xpad_ref[0:8, :] = xpad_ref[n:n + 8, :]
    xcb = xc.astype(BF16)
    lam = lam_ref[...]
    sp_lam = jnp.maximum(-lam, 0.0) + jnp.log1p(jnp.exp(-jnp.abs(lam)))
    for blk in range(LRU_BLOCKS):
        sl = slice(blk * LRU_BLOCK_W, (blk + 1) * LRU_BLOCK_W)
        xb = xcb[:, sl]
        pre_r = jnp.dot(xb, wa_ref[blk], preferred_element_type=F32) + ba_ref[:, sl]
        pre_i = jnp.dot(xb, wi_ref[blk], preferred_element_type=F32) + bi_ref[:, sl]
        r = 0.5 * jnp.tanh(0.5 * pre_r) + 0.5
        i = 0.5 * jnp.tanh(0.5 * pre_i) + 0.5
        log_a = (-LRU_C * r) * sp_lam[:, sl]
        a = jnp.exp(log_a)
        m2 = -jnp.tanh(log_a) * (a * a + 1.0)
        mult = m2 * lax.rsqrt(jnp.maximum(m2, 1e-30))
        u = mult * (i * xc[:, sl])
        for k in range(LRU_BLOCK_W // LANES):
            slab = blk * (LRU_BLOCK_W // LANES) + k
            for ch in range(8):
                rows = slice(ch * steps, (ch + 1) * steps)
                dst = slice(ch * REC_PITCH, ch * REC_PITCH + steps)
                a_ref[slab, dst, :] = a[rows, k * LANES:(k + 1) * LANES]
                u_ref[slab, dst, :] = u[rows, k * LANES:(k + 1) * LANES]

    def step(l, state):
        rows = pl.ds(l, 8, stride=REC_PITCH)
        new = []
        for slab in range(n_slabs):
            h, p = state[slab]
            av = a_ref[slab, rows, :]
            h = av * h + u_ref[slab, rows, :]
            p = p * av
            hl_ref[slab, rows, :] = h
            p_ref[slab, rows, :] = p
            new.append((h, p))
        return tuple(new)

    zeros = jnp.zeros((8, LANES), F32)
    ones = jnp.ones((8, LANES), F32)
    ends = lax.fori_loop(0, steps, step, tuple((zeros, ones) for _ in range(n_slabs)), unroll=4)

    rowid = lax.broadcasted_iota(jnp.int32, (8, LANES), 0)
    for slab in range(n_slabs):
        lanes = slice(slab * LANES, (slab + 1) * LANES)
        b, a = ends[slab]
        for d in (1, 2, 4):
            keep = rowid >= d
            a_sh = jnp.where(keep, pltpu.roll(a, d, 0), 1.0)
            b_sh = jnp.where(keep, pltpu.roll(b, d, 0), 0.0)
            b = a * b_sh + b
            a = a * a_sh
        h_in = h_ref[:, lanes]
        after = b + a * h_in
        h_ref[:, lanes] = after[7:8, :]
        start = jnp.where(rowid >= 1, pltpu.roll(after, 1, 0), h_in)
        for ch in range(8):
            src = slice(ch * REC_PITCH, ch * REC_PITCH + steps)
            rows = slice(ch * steps, (ch + 1) * steps)
            hs = hl_ref[slab, src, :] + p_ref[slab, src, :] * start[ch:ch + 1, :]
            y_ref[0, rows, lanes] = (hs * gate[rows, lanes]).astype(y_ref.dtype)


def rglru_mixer(x, g, w_in_stack, conv_w, conv_b, w_a_stack, b_a, w_i_stack, b_i, lam, j):
    b, s, d = x.shape
    width = w_in_stack.shape[2] // 2
    const2 = lambda bi, t: (0, 0)
    scan_rows = 8 * REC_PITCH
    return pl.pallas_call(
        _rglru_kernel,
        out_shape=jax.ShapeDtypeStruct((b, s, width), BF16),
        grid=(b, s // REC_TILE),
        in_specs=[
            pl.BlockSpec((1, REC_TILE, d), lambda bi, t: (bi, t, 0)),
            pl.BlockSpec((1, d), const2),
            _layer_block(w_in_stack, j),
            pl.BlockSpec(conv_w.shape, const2),
            pl.BlockSpec((1, width), const2),
            _layer_block(w_a_stack, j),
            pl.BlockSpec((1, width), const2),
            _layer_block(w_i_stack, j),
            pl.BlockSpec((1, width), const2),
            pl.BlockSpec((1, width), const2),
        ],
        out_specs=pl.BlockSpec((1, REC_TILE, width), lambda bi, t: (bi, t, 0)),
        scratch_shapes=[
            pltpu.VMEM((REC_TILE + 8, width), F32),
            pltpu.VMEM((width // LANES, scan_rows, LANES), F32),
            pltpu.VMEM((width // LANES, scan_rows, LANES), F32),
            pltpu.VMEM((width // LANES, scan_rows, LANES), F32),
            pltpu.VMEM((width // LANES, scan_rows, LANES), F32),
            pltpu.VMEM((1, width), F32),
        ],
        compiler_params=_params(("arbitrary", "arbitrary")),
        name="rglru_mixer",
    )(x, g, w_in_stack, conv_w, conv_b, w_a_stack, b_a, w_i_stack, b_i, lam)


def kernel(x, attn_w_in, attn_rel_bias, attn_w_out, rg_w_in, rg_conv_w, rg_conv_b, rg_w_a, rg_b_a,
           rg_w_i, rg_b_i, rg_lambda, rg_w_out, norm_mix_pre, norm_mix_post, norm_ffn_pre,
           norm_ffn_post, ffn_w_gate, ffn_w_up, ffn_w_down):
    b, s, d = x.shape
    depth = norm_mix_pre.shape[0]
    xf = x.reshape(b * s, d)
    row = lambda v: v.reshape(1, -1)
    attn_w_in, attn_w_out, rg_w_in, rg_w_a, rg_w_i, rg_w_out, ffn_w_gate, ffn_w_up, ffn_w_down = (
        w.astype(BF16) for w in (attn_w_in, attn_w_out, rg_w_in, rg_w_a, rg_w_i, rg_w_out,
                                 ffn_w_gate, ffn_w_up, ffn_w_down))
    for layer in range(depth):
        j = layer // 2
        g_pre = row(norm_mix_pre[layer])
        if layer % 2 == 0:
            proj = norm_linear(xf, g_pre, attn_w_in, j, BF16).reshape(b, s, -1)
            out_a = band_attention(proj, band_bias(attn_rel_bias[j]), 0)
            out_b = stick_breaking_attention(proj, 3 * A_W)
            ins = [out_a.reshape(b * s, A_W), out_b.reshape(b * s, B_W)]
            w_out = attn_w_out
        else:
            width = rg_w_out.shape[1]
            y = rglru_mixer(
                xf.reshape(b, s, d), g_pre, rg_w_in,
                rg_conv_w[j].reshape(-1, width), row(rg_conv_b[j]),
                rg_w_a, row(rg_b_a[j]), rg_w_i, row(rg_b_i[j]), row(rg_lambda[j]), j)
            ins = [y.reshape(b * s, width)]
            w_out = rg_w_out
        xf = mixer_out_ffn(
            xf, row(norm_mix_post[layer]), ins, w_out, j,
            row(norm_ffn_pre[layer]), row(norm_ffn_post[layer]),
            ffn_w_gate, ffn_w_up, ffn_w_down, layer)
    return xf.reshape(b, s, d)
```

```python
import functools

import jax
import jax.numpy as jnp
from jax import lax
from jax.experimental import pallas as pl
from jax.experimental.pallas import tpu as pltpu

F32 = jnp.float32
BF16 = jnp.bfloat16

HEAD_DIM = 64
CHUNK = 64
N_LEFT_CHUNKS = 8
REL_CLIP = 256
A_W = 512
B_W = 512
LRU_BLOCKS = 4
LRU_BLOCK_W = 256
LRU_C = 8.0
RMS_EPS = 1e-6

LANES = 128
VMEM_LIMIT = 56 * 1024 * 1024

ROW_TILE = 512
SB_TILE = 256
SB_SUBTILES = 2
SB_PAIRS = 4
SB_EXIT = 128.0
A_TILE = 2 * CHUNK
A_SUBTILES = 32
A_WIN = (N_LEFT_CHUNKS + 2) * CHUNK
A_PAD = N_LEFT_CHUNKS * CHUNK
REC_TILE = 512
REC_PITCH = REC_TILE // 8 + 8
NEG_BIG = -1e30
LOG2E = 1.4426950408889634

_NT = (((1,), (1,)), ((), ()))


def _params(sem):
    return pltpu.CompilerParams(dimension_semantics=sem, vmem_limit_bytes=VMEM_LIMIT)


def _rms(x, g):
    ms = jnp.mean(x * x, axis=-1, keepdims=True)
    return x * lax.rsqrt(ms + RMS_EPS) * g


def _sigmoid(x):
    return 1.0 / (1.0 + jnp.exp(-x))


def _resident(shape):
    return pl.BlockSpec(shape, lambda *_: (0,) * len(shape), pipeline_mode=pl.Buffered(1))


def _layer_block(stack, layer, rows=None, row_block=0):
    shape = stack.shape[1:] if rows is None else (rows,) + stack.shape[2:]
    index = (layer, row_block) + (0,) * (len(shape) - 1)
    return pl.BlockSpec((None,) + shape, lambda *_: index, pipeline_mode=pl.Buffered(1))


def _norm_linear_kernel(x_ref, g_ref, w_ref, o_ref):
    h = _rms(x_ref[...], g_ref[...]).astype(BF16)
    o_ref[...] = jnp.dot(h, w_ref[...], preferred_element_type=F32).astype(o_ref.dtype)


def norm_linear(x, g, w_stack, layer, out_dtype):
    m, d = x.shape
    n = w_stack.shape[2]
    return pl.pallas_call(
        _norm_linear_kernel,
        out_shape=jax.ShapeDtypeStruct((m, n), out_dtype),
        grid=(m // ROW_TILE,),
        in_specs=[
            pl.BlockSpec((ROW_TILE, d), lambda i: (i, 0)),
            pl.BlockSpec((1, d), lambda i: (0, 0)),
            _layer_block(w_stack, layer),
        ],
        out_specs=pl.BlockSpec((ROW_TILE, n), lambda i: (i, 0)),
        compiler_params=_params(("arbitrary",)),
        name="norm_linear",
    )(x, g, w_stack)


def _out_ffn_kernel(n_in, *refs):
    x_ref, gmix_ref, gpre_ref, gpost_ref, wg_ref, wu_ref, wd_ref = refs[:7]
    ins = refs[7:7 + n_in]
    ws = refs[7 + n_in:7 + 2 * n_in]
    o_ref = refs[7 + 2 * n_in]
    m = jnp.dot(ins[0][...], ws[0][...], preferred_element_type=F32)
    for a, w in zip(ins[1:], ws[1:]):
        m = m + jnp.dot(a[...], w[...], preferred_element_type=F32)
    x = x_ref[...] + _rms(m, gmix_ref[...])
    h = _rms(x, gpre_ref[...]).astype(BF16)
    gate = jnp.dot(h, wg_ref[...], preferred_element_type=F32)
    up = jnp.dot(h, wu_ref[...], preferred_element_type=F32)
    act = (gate * _sigmoid(gate) * up).astype(BF16)
    f = jnp.dot(act, wd_ref[...], preferred_element_type=F32)
    o_ref[...] = x + _rms(f, gpost_ref[...])


def mixer_out_ffn(x, gmix, ins, w_out_stack, j, gpre, gpost, wg_stack, wu_stack, wd_stack, layer):
    m, d = x.shape
    n_in = len(ins)
    in_specs = [pl.BlockSpec((ROW_TILE, d), lambda i: (i, 0))]
    in_specs += [_resident(a.shape) for a in (gmix, gpre, gpost)]
    in_specs += [_layer_block(w, layer) for w in (wg_stack, wu_stack, wd_stack)]
    in_specs += [pl.BlockSpec((ROW_TILE, a.shape[1]), lambda i: (i, 0)) for a in ins]
    in_specs += [_layer_block(w_out_stack, j, a.shape[1], k) for k, a in enumerate(ins)]
    return pl.pallas_call(
        functools.partial(_out_ffn_kernel, n_in),
        out_shape=jax.ShapeDtypeStruct((m, d), F32),
        grid=(m // ROW_TILE,),
        in_specs=in_specs,
        out_specs=pl.BlockSpec((ROW_TILE, d), lambda i: (i, 0)),
        compiler_params=_params(("arbitrary",)),
        name="mixer_out_ffn",
    )(x, gmix, gpre, gpost, wg_stack, wu_stack, wd_stack, *ins, *([w_out_stack] * n_in))


def _band_attn_kernel(q_ref, k_ref, v_ref, bias_ref, o_ref, kp_ref, vp_ref, s_ref, p_ref, rinv_ref):
    t = pl.program_id(2)

    @pl.when(t == 0)
    def _():
        kp_ref[0:A_PAD, :] = jnp.zeros((A_PAD, LANES), BF16)
        vp_ref[0:A_PAD, :] = jnp.zeros((A_PAD, LANES), BF16)
        kp_ref[A_PAD:, :] = k_ref[0]
        vp_ref[A_PAD:, :] = v_ref[0]

    lane = lax.broadcasted_iota(jnp.int32, (1, LANES), 1)
    first = lane < HEAD_DIM
    col = lax.broadcasted_iota(jnp.int32, (1, A_WIN), 1)
    bias = jnp.concatenate([bias_ref[0], bias_ref[1]], axis=0)
    zero = jnp.zeros((A_TILE, LANES), BF16)

    def window(u):
        tile = t * A_SUBTILES + u
        return tile, pl.multiple_of(tile * A_TILE, A_TILE)

    def scores(u, slot, head_of_sequence):
        tile, start = window(u)
        kw = kp_ref[pl.ds(start, A_WIN), :]
        q2 = q_ref[0, pl.ds(pl.multiple_of(u * A_TILE, A_TILE), A_TILE), :] * (HEAD_DIM ** -0.5)
        qs = jnp.concatenate([jnp.where(first, q2, zero), jnp.where(first, zero, q2)], axis=0)
        s = lax.dot_general(qs, kw, _NT, preferred_element_type=F32) + bias
        if head_of_sequence:
            s = jnp.where(col >= (A_PAD - tile * A_TILE), s, NEG_BIG)
        s_ref[slot] = s

    def softmax(slot):
        s = s_ref[slot]
        p = jnp.exp(s - jnp.max(s, axis=-1, keepdims=True))
        p_ref[slot] = p.astype(BF16)
        rinv_ref[slot] = jnp.broadcast_to(1.0 / jnp.sum(p, axis=-1, keepdims=True), (2 * A_TILE, LANES))

    def values(u, slot):
        _, start = window(u)
        vw = vp_ref[pl.ds(start, A_WIN), :]
        o = jnp.dot(p_ref[slot], vw, preferred_element_type=F32) * rinv_ref[slot]
        out = jnp.where(first, o[:A_TILE], o[A_TILE:])
        o_ref[0, pl.ds(pl.multiple_of(u * A_TILE, A_TILE), A_TILE), :] = out.astype(o_ref.dtype)

    scores(0, 0, True)
    scores(1, 1, True)
    softmax(0)

    def pair(k, head_of_sequence):
        values(2 * k - 2, 0)
        softmax(1)
        scores(2 * k, 0, head_of_sequence)
        values(2 * k - 1, 1)
        softmax(0)
        scores(2 * k + 1, 1, head_of_sequence)

    n_head = A_PAD // A_TILE // 2
    for k in range(1, n_head):
        pair(k, True)

    def body(k, _):
        pair(k, False)
        return 0

    lax.fori_loop(n_head, A_SUBTILES // 2, body, 0)
    values(A_SUBTILES - 2, 0)
    softmax(1)
    values(A_SUBTILES - 1, 1)


def band_attention(proj, bias, col0):
    b, s, _ = proj.shape
    nq = col0 // LANES
    nk = (col0 + A_W) // LANES
    nv = (col0 + 2 * A_W) // LANES
    tq = A_SUBTILES * A_TILE
    return pl.pallas_call(
        _band_attn_kernel,
        out_shape=jax.ShapeDtypeStruct((b, s, A_W), BF16),
        grid=(b, A_W // LANES, s // tq),
        in_specs=[
            pl.BlockSpec((1, tq, LANES), lambda bi, p, t: (bi, t, nq + p)),
            pl.BlockSpec((1, s, LANES), lambda bi, p, t: (bi, 0, nk + p)),
            pl.BlockSpec((1, s, LANES), lambda bi, p, t: (bi, 0, nv + p)),
            pl.BlockSpec((2, A_TILE, A_WIN), lambda bi, p, t: (p, 0, 0)),
        ],
        out_specs=pl.BlockSpec((1, tq, LANES), lambda bi, p, t: (bi, t, p)),
        scratch_shapes=[
            pltpu.VMEM((s + A_PAD, LANES), BF16),
            pltpu.VMEM((s + A_PAD, LANES), BF16),
            pltpu.VMEM((2, 2 * A_TILE, A_WIN), F32),
            pltpu.VMEM((2, 2 * A_TILE, A_WIN), BF16),
            pltpu.VMEM((2, 2 * A_TILE, LANES), F32),
        ],
        compiler_params=_params(("arbitrary", "arbitrary", "arbitrary")),
        name="band_attention",
    )(proj, proj, proj, bias)


def band_bias(rel_bias):
    h = rel_bias.shape[0]
    n_f = A_WIN + A_TILE - 1
    n_const = A_PAD + A_TILE - 1 - REL_CLIP
    far = jnp.broadcast_to(rel_bias[:, 2 * REL_CLIP:], (h, n_const))
    near = rel_bias[:, 2 * REL_CLIP + 1 - (n_f - n_const):][:, ::-1]
    f = jnp.concatenate([far, near], axis=1)
    period = n_f + 1
    g = jnp.concatenate([f[:, A_TILE - 1:], jnp.zeros((h, 1), F32), f[:, :A_TILE - 1]], axis=1)
    flat = jnp.tile(g, (1, A_TILE))[:, :A_TILE * (period - 1)]
    toep = flat.reshape(h, A_TILE, period - 1)[:, :, :A_WIN]
    r = jnp.arange(A_TILE)[:, None]
    c = jnp.arange(A_WIN)[None, :]
    dchunk = c // CHUNK - r // CHUNK
    ok = (dchunk >= 0) & (dchunk <= N_LEFT_CHUNKS)
    return jnp.where(ok[None], toep, NEG_BIG).astype(F32)


def _stick_kernel(q_ref, k_ref, v_ref, o_ref, acc_ref, carry_ref):
    qi = pl.program_id(2)
    t = SB_TILE
    ns = SB_SUBTILES
    lane = lax.broadcasted_iota(jnp.int32, (1, LANES), 1)
    first = lane < HEAD_DIM
    row = lax.broadcasted_iota(jnp.int32, (t, t), 0)
    col = lax.broadcasted_iota(jnp.int32, (t, t), 1)
    causal = col < row
    true = jnp.ones((t, t), jnp.bool_)
    suffix = jnp.where(row >= col, 1.0, 0.0).astype(BF16)
    zero = jnp.zeros((t, LANES), BF16)

    acc_ref[...] = jnp.zeros_like(acc_ref)
    carry_ref[...] = jnp.zeros_like(carry_ref)

    def lanes_of(pair):
        return slice(pair * LANES, (pair + 1) * LANES)

    def split_heads(x):
        return jnp.concatenate([jnp.where(first, x, zero), jnp.where(first, zero, x)], axis=0)

    def scores(pair, j, d, mask):
        start = pl.multiple_of(j * t, t)
        kcat = split_heads(k_ref[0, pl.ds(start, t), lanes_of(pair)])
        q2 = q_ref[0, d * t:, lanes_of(pair)] * (HEAD_DIM ** -0.5)
        z = lax.dot_general(q2, kcat, _NT, preferred_element_type=F32)
        sp = jnp.maximum(z, 0.0) + jnp.log(1.0 + jnp.exp2(jnp.abs(z) * (-LOG2E)))
        if mask is not None:
            sp = jnp.where(mask, sp, 0.0)
        spb = sp.astype(BF16)
        return z - (sp - spb.astype(F32)), spb

    def weights(pair, j, d, mask, tz, spb):
        r0 = d * t
        n = (ns - d) * t
        start = pl.multiple_of(j * t, t)
        vcat = split_heads(v_ref[0, pl.ds(start, t), lanes_of(pair)])
        c = [jnp.dot(spb[:, h * t:(h + 1) * t], suffix, preferred_element_type=F32) for h in range(2)]
        carry = [carry_ref[pair, h, r0:, :] for h in range(2)]
        reps = t // LANES
        shift = jnp.concatenate([c[0]] + [c[1]], axis=1) + jnp.concatenate(
            [carry[0]] * reps + [carry[1]] * reps, axis=1)
        w = jnp.exp(tz - shift)
        if mask is not None:
            w = jnp.where(mask, w, 0.0)
        for h in range(2):
            carry_ref[pair, h, r0:, :] = carry[h] + jnp.broadcast_to(c[h][:, 0:1], (n, LANES))
        acc_ref[pair, r0:, :] += jnp.dot(w.astype(BF16), vcat, preferred_element_type=F32)

    for pair in range(SB_PAIRS):
        for d in range(ns - 1, -1, -1):
            mask = jnp.concatenate([causal] + [true] * (ns - d - 1), axis=0)
            mask = jnp.concatenate([mask, mask], axis=1)
            tz, spb = scores(pair, ns * qi + d, d, mask)
            weights(pair, ns * qi + d, d, mask, tz, spb)

    def saturated(pair):
        low = jnp.min(jnp.minimum(carry_ref[pair, 0], carry_ref[pair, 1]), axis=0, keepdims=True)
        return (jnp.min(low) >= SB_EXIT).astype(jnp.int32)

    @pl.when(qi > 0)
    def _():
        for pair in range(SB_PAIRS):
            tz, spb = scores(pair, ns * qi - 1, 0, None)
            weights(pair, ns * qi - 1, 0, None, tz, spb)
        for pair in range(SB_PAIRS):
            def cond(state):
                i, done = state
                return jnp.logical_and(i < ns * qi, done == 0)

            def body(state, pair=pair):
                i, _ = state
                tz, spb = scores(pair, ns * qi - 1 - i, 0, None)
                weights(pair, ns * qi - 1 - i, 0, None, tz, spb)
                return i + 1, saturated(pair)

            lax.while_loop(cond, body, (jnp.int32(1), saturated(pair)))

    for pair in range(SB_PAIRS):
        o_ref[0, :, lanes_of(pair)] = acc_ref[pair].astype(o_ref.dtype)


def stick_breaking_attention(proj, col0):
    b, s, _ = proj.shape
    width = SB_PAIRS * LANES
    nq = col0 // width
    nk = (col0 + B_W) // width
    nv = (col0 + 2 * B_W) // width
    tq = SB_SUBTILES * SB_TILE
    return pl.pallas_call(
        _stick_kernel,
        out_shape=jax.ShapeDtypeStruct((b, s, B_W), BF16),
        grid=(b, B_W // width, s // tq),
        in_specs=[
            pl.BlockSpec((1, tq, width), lambda bi, p, t: (bi, t, nq + p)),
            pl.BlockSpec((1, s, width), lambda bi, p, t: (bi, 0, nk + p)),
            pl.BlockSpec((1, s, width), lambda bi, p, t: (bi, 0, nv + p)),
        ],
        out_specs=pl.BlockSpec((1, tq, width), lambda bi, p, t: (bi, t, p)),
        scratch_shapes=[
            pltpu.VMEM((SB_PAIRS, tq, LANES), F32),
            pltpu.VMEM((SB_PAIRS, 2, tq, LANES), F32),
        ],
        compiler_params=_params(("arbitrary", "arbitrary", "arbitrary")),
        name="stick_breaking",
    )(proj, proj, proj)


def _rglru_kernel(x_ref, g_ref, win_ref, cw_ref, cb_ref, wa_ref, ba_ref, wi_ref, bi_ref, lam_ref,
                  y_ref, xpad_ref, a_ref, u_ref, p_ref, hl_ref, h_ref):
    t = pl.program_id(1)
    n = REC_TILE
    steps = n // 8
    width = y_ref.shape[2]
    n_slabs = width // LANES

    @pl.when(t == 0)
    def _():
        xpad_ref[0:8, :] = jnp.zeros((8, width), F32)
        h_ref[...] = jnp.zeros_like(h_ref)

    hn = _rms(x_ref[0], g_ref[...]).astype(BF16)
    gate = jnp.dot(hn, win_ref[:, :width], preferred_element_type=F32)
    c = 0.7978845608028654
    gate = 0.5 * gate * (1.0 + jnp.tanh(c * (gate + 0.044715 * (gate * gate * gate))))
    xpad_ref[8:8 + n, :] = jnp.dot(hn, win_ref[:, width:], preferred_element_type=F32)
    cw = cw_ref[...]
    xc = (cb_ref[...]
          + cw[3:4] * xpad_ref[8:8 + n, :]
          + cw[2:3] * xpad_ref[7:7 + n, :]
          + cw[1:2] * xpad_ref[6:6 + n, :]
          + cw[0:1] * xpad_ref[5:5 + n, :])
    xpad_ref[0:8, :] = xpad_ref[n:n + 8, :]
    xcb = xc.astype(BF16)
    lam = lam_ref[...]
    sp_lam = jnp.maximum(-lam, 0.0) + jnp.log1p(jnp.exp(-jnp.abs(lam)))
    for blk in range(LRU_BLOCKS):
        sl = slice(blk * LRU_BLOCK_W, (blk + 1) * LRU_BLOCK_W)
        xb = xcb[:, sl]
        pre_r = jnp.dot(xb, wa_ref[blk], preferred_element_type=F32) + ba_ref[:, sl]
        pre_i = jnp.dot(xb, wi_ref[blk], preferred_element_type=F32) + bi_ref[:, sl]
        r = 0.5 * jnp.tanh(0.5 * pre_r) + 0.5
        i = 0.5 * jnp.tanh(0.5 * pre_i) + 0.5
        log_a = (-LRU_C * r) * sp_lam[:, sl]
        a = jnp.exp(log_a)
        m2 = -jnp.tanh(log_a) * (a * a + 1.0)
        mult = m2 * lax.rsqrt(jnp.maximum(m2, 1e-30))
        u = mult * (i * xc[:, sl])
        for k in range(LRU_BLOCK_W // LANES):
            slab = blk * (LRU_BLOCK_W // LANES) + k
            for ch in range(8):
                rows = slice(ch * steps, (ch + 1) * steps)
                dst = slice(ch * REC_PITCH, ch * REC_PITCH + steps)
                a_ref[slab, dst, :] = a[rows, k * LANES:(k + 1) * LANES]
                u_ref[slab, dst, :] = u[rows, k * LANES:(k + 1) * LANES]

    def step(l, state):
        rows = pl.ds(l, 8, stride=REC_PITCH)
        new = []
        for slab in range(n_slabs):
            h, p = state[slab]
            av = a_ref[slab, rows, :]
            h = av * h + u_ref[slab, rows, :]
            p = p * av
            hl_ref[slab, rows, :] = h
            p_ref[slab, rows, :] = p
            new.append((h, p))
        return tuple(new)

    zeros = jnp.zeros((8, LANES), F32)
    ones = jnp.ones((8, LANES), F32)
    ends = lax.fori_loop(0, steps, step, tuple((zeros, ones) for _ in range(n_slabs)), unroll=4)

    rowid = lax.broadcasted_iota(jnp.int32, (8, LANES), 0)
    for slab in range(n_slabs):
        lanes = slice(slab * LANES, (slab + 1) * LANES)
        b, a = ends[slab]
        for d in (1, 2, 4):
            keep = rowid >= d
            a_sh = jnp.where(keep, pltpu.roll(a, d, 0), 1.0)
            b_sh = jnp.where(keep, pltpu.roll(b, d, 0), 0.0)
            b = a * b_sh + b
            a = a * a_sh
        h_in = h_ref[:, lanes]
        after = b + a * h_in
        h_ref[:, lanes] = after[7:8, :]
        start = jnp.where(rowid >= 1, pltpu.roll(after, 1, 0), h_in)
        for ch in range(8):
            src = slice(ch * REC_PITCH, ch * REC_PITCH + steps)
            rows = slice(ch * steps, (ch + 1) * steps)
            hs = hl_ref[slab, src, :] + p_ref[slab, src, :] * start[ch:ch + 1, :]
            y_ref[0, rows, lanes] = (hs * gate[rows, lanes]).astype(y_ref.dtype)


def rglru_mixer(x, g, w_in_stack, conv_w, conv_b, w_a_stack, b_a, w_i_stack, b_i, lam, j):
    b, s, d = x.shape
    width = w_in_stack.shape[2] // 2
    const2 = lambda bi, t: (0, 0)
    scan_rows = 8 * REC_PITCH
    return pl.pallas_call(
        _rglru_kernel,
        out_shape=jax.ShapeDtypeStruct((b, s, width), BF16),
        grid=(b, s // REC_TILE),
        in_specs=[
            pl.BlockSpec((1, REC_TILE, d), lambda bi, t: (bi, t, 0)),
            pl.BlockSpec((1, d), const2),
            _layer_block(w_in_stack, j),
            pl.BlockSpec(conv_w.shape, const2),
            pl.BlockSpec((1, width), const2),
            _layer_block(w_a_stack, j),
            pl.BlockSpec((1, width), const2),
            _layer_block(w_i_stack, j),
            pl.BlockSpec((1, width), const2),
            pl.BlockSpec((1, width), const2),
        ],
        out_specs=pl.BlockSpec((1, REC_TILE, width), lambda bi, t: (bi, t, 0)),
        scratch_shapes=[
            pltpu.VMEM((REC_TILE + 8, width), F32),
            pltpu.VMEM((width // LANES, scan_rows, LANES), F32),
            pltpu.VMEM((width // LANES, scan_rows, LANES), F32),
            pltpu.VMEM((width // LANES, scan_rows, LANES), F32),
            pltpu.VMEM((width // LANES, scan_rows, LANES), F32),
            pltpu.VMEM((1, width), F32),
        ],
        compiler_params=_params(("arbitrary", "arbitrary")),
        name="rglru_mixer",
    )(x, g, w_in_stack, conv_w, conv_b, w_a_stack, b_a, w_i_stack, b_i, lam)


def kernel(x, attn_w_in, attn_rel_bias, attn_w_out, rg_w_in, rg_conv_w, rg_conv_b, rg_w_a, rg_b_a,
           rg_w_i, rg_b_i, rg_lambda, rg_w_out, norm_mix_pre, norm_mix_post, norm_ffn_pre,
           norm_ffn_post, ffn_w_gate, ffn_w_up, ffn_w_down):
    b, s, d = x.shape
    depth = norm_mix_pre.shape[0]
    xf = x.reshape(b * s, d)
    row = lambda v: v.reshape(1, -1)
    attn_w_in, attn_w_out, rg_w_in, rg_w_a, rg_w_i, rg_w_out, ffn_w_gate, ffn_w_up, ffn_w_down = (
        w.astype(BF16) for w in (attn_w_in, attn_w_out, rg_w_in, rg_w_a, rg_w_i, rg_w_out,
                                 ffn_w_gate, ffn_w_up, ffn_w_down))
    for layer in range(depth):
        j = layer // 2
        g_pre = row(norm_mix_pre[layer])
        if layer % 2 == 0:
            proj = norm_linear(xf, g_pre, attn_w_in, j, BF16).reshape(b, s, -1)
            out_a = band_attention(proj, band_bias(attn_rel_bias[j]), 0)
            out_b = stick_breaking_attention(proj, 3 * A_W)
            ins = [out_a.reshape(b * s, A_W), out_b.reshape(b * s, B_W)]
            w_out = attn_w_out
        else:
            width = rg_w_out.shape[1]
            y = rglru_mixer(
                xf.reshape(b, s, d), g_pre, rg_w_in,
                rg_conv_w[j].reshape(-1, width), row(rg_conv_b[j]),
                rg_w_a, row(rg_b_a[j]), rg_w_i, row(rg_b_i[j]), row(rg_lambda[j]), j)
            ins = [y.reshape(b * s, width)]
            w_out = rg_w_out
        xf = mixer_out_ffn(
            xf, row(norm_mix_post[layer]), ins, w_out, j,
            row(norm_ffn_pre[layer]), row(norm_ffn_post[layer]),
            ffn_w_gate, ffn_w_up, ffn_w_down, layer)
    return xf.reshape(b, s, d)
```

```python
import functools

import jax
import jax.numpy as jnp
from jax import lax
from jax.experimental import pallas as pl
from jax.experimental.pallas import tpu as pltpu

F32 = jnp.float32
BF16 = jnp.bfloat16

HEAD_DIM = 64
CHUNK = 64
N_LEFT_CHUNKS = 8
REL_CLIP = 256
A_W = 512
B_W = 512
LRU_BLOCKS = 4
LRU_BLOCK_W = 256
LRU_C = 8.0
RMS_EPS = 1e-6

LANES = 128
VMEM_LIMIT = 56 * 1024 * 1024

ROW_TILE = 512
SB_TILE = 256
SB_SUBTILES = 2
SB_PAIRS = 4
SB_EXIT = 128.0
A_TILE = 2 * CHUNK
A_SUBTILES = 64
A_WIN = (N_LEFT_CHUNKS + 2) * CHUNK
A_PAD = N_LEFT_CHUNKS * CHUNK
REC_TILE = 512
REC_PITCH = REC_TILE // 8 + 8
NEG_BIG = -1e30
LOG2E = 1.4426950408889634

_NT = (((1,), (1,)), ((), ()))


def _params(sem):
    return pltpu.CompilerParams(dimension_semantics=sem, vmem_limit_bytes=VMEM_LIMIT)


def _rms(x, g):
    ms = jnp.mean(x * x, axis=-1, keepdims=True)
    return x * lax.rsqrt(ms + RMS_EPS) * g


def _sigmoid(x):
    return 1.0 / (1.0 + jnp.exp(-x))


def _resident(shape):
    return pl.BlockSpec(shape, lambda *_: (0,) * len(shape), pipeline_mode=pl.Buffered(1))


def _layer_block(stack, layer, rows=None, row_block=0):
    shape = stack.shape[1:] if rows is None else (rows,) + stack.shape[2:]
    index = (layer, row_block) + (0,) * (len(shape) - 1)
    return pl.BlockSpec((None,) + shape, lambda *_: index, pipeline_mode=pl.Buffered(1))


def _norm_linear_kernel(x_ref, g_ref, w_ref, o_ref):
    h = _rms(x_ref[...], g_ref[...]).astype(BF16)
    o_ref[...] = jnp.dot(h, w_ref[...], preferred_element_type=F32).astype(o_ref.dtype)


def norm_linear(x, g, w_stack, layer, out_dtype):
    m, d = x.shape
    n = w_stack.shape[2]
    return pl.pallas_call(
        _norm_linear_kernel,
        out_shape=jax.ShapeDtypeStruct((m, n), out_dtype),
        grid=(m // ROW_TILE,),
        in_specs=[
            pl.BlockSpec((ROW_TILE, d), lambda i: (i, 0)),
            pl.BlockSpec((1, d), lambda i: (0, 0)),
            _layer_block(w_stack, layer),
        ],
        out_specs=pl.BlockSpec((ROW_TILE, n), lambda i: (i, 0)),
        compiler_params=_params(("arbitrary",)),
        name="norm_linear",
    )(x, g, w_stack)


def _out_ffn_kernel(n_in, *refs):
    x_ref, gmix_ref, gpre_ref, gpost_ref, wg_ref, wu_ref, wd_ref = refs[:7]
    ins = refs[7:7 + n_in]
    ws = refs[7 + n_in:7 + 2 * n_in]
    o_ref = refs[7 + 2 * n_in]
    m = jnp.dot(ins[0][...], ws[0][...], preferred_element_type=F32)
    for a, w in zip(ins[1:], ws[1:]):
        m = m + jnp.dot(a[...], w[...], preferred_element_type=F32)
    x = x_ref[...] + _rms(m, gmix_ref[...])
    h = _rms(x, gpre_ref[...]).astype(BF16)
    gate = jnp.dot(h, wg_ref[...], preferred_element_type=F32)
    up = jnp.dot(h, wu_ref[...], preferred_element_type=F32)
    act = (gate * _sigmoid(gate) * up).astype(BF16)
    f = jnp.dot(act, wd_ref[...], preferred_element_type=F32)
    o_ref[...] = x + _rms(f, gpost_ref[...])


def mixer_out_ffn(x, gmix, ins, w_out_stack, j, gpre, gpost, wg_stack, wu_stack, wd_stack, layer):
    m, d = x.shape
    n_in = len(ins)
    in_specs = [pl.BlockSpec((ROW_TILE, d), lambda i: (i, 0))]
    in_specs += [_resident(a.shape) for a in (gmix, gpre, gpost)]
    in_specs += [_layer_block(w, layer) for w in (wg_stack, wu_stack, wd_stack)]
    in_specs += [pl.BlockSpec((ROW_TILE, a.shape[1]), lambda i: (i, 0)) for a in ins]
    in_specs += [_layer_block(w_out_stack, j, a.shape[1], k) for k, a in enumerate(ins)]
    return pl.pallas_call(
        functools.partial(_out_ffn_kernel, n_in),
        out_shape=jax.ShapeDtypeStruct((m, d), F32),
        grid=(m // ROW_TILE,),
        in_specs=in_specs,
        out_specs=pl.BlockSpec((ROW_TILE, d), lambda i: (i, 0)),
        compiler_params=_params(("arbitrary",)),
        name="mixer_out_ffn",
    )(x, gmix, gpre, gpost, wg_stack, wu_stack, wd_stack, *ins, *([w_out_stack] * n_in))


def _band_attn_kernel(q_ref, k_ref, v_ref, bias_ref, o_ref, kp_ref, vp_ref, s_ref, p_ref, rinv_ref):
    t = pl.program_id(2)

    @pl.when(t == 0)
    def _():
        kp_ref[0:A_PAD, :] = jnp.zeros((A_PAD, LANES), BF16)
        vp_ref[0:A_PAD, :] = jnp.zeros((A_PAD, LANES), BF16)
        kp_ref[A_PAD:, :] = k_ref[0]
        vp_ref[A_PAD:, :] = v_ref[0]

    lane = lax.broadcasted_iota(jnp.int32, (1, LANES), 1)
    first = lane < HEAD_DIM
    col = lax.broadcasted_iota(jnp.int32, (1, A_WIN), 1)
    bias = jnp.concatenate([bias_ref[0], bias_ref[1]], axis=0)
    zero = jnp.zeros((A_TILE, LANES), BF16)

    def window(u):
        tile = t * A_SUBTILES + u
        return tile, pl.multiple_of(tile * A_TILE, A_TILE)

    def scores(u, slot, head_of_sequence):
        tile, start = window(u)
        kw = kp_ref[pl.ds(start, A_WIN), :]
        q2 = q_ref[0, pl.ds(pl.multiple_of(u * A_TILE, A_TILE), A_TILE), :] * (HEAD_DIM ** -0.5)
        qs = jnp.concatenate([jnp.where(first, q2, zero), jnp.where(first, zero, q2)], axis=0)
        s = lax.dot_general(qs, kw, _NT, preferred_element_type=F32) + bias
        if head_of_sequence:
            s = jnp.where(col >= (A_PAD - tile * A_TILE), s, NEG_BIG)
        s_ref[slot] = s

    def softmax(slot):
        s = s_ref[slot]
        p = jnp.exp(s - jnp.max(s, axis=-1, keepdims=True))
        p_ref[slot] = p.astype(BF16)
        rinv_ref[slot] = jnp.broadcast_to(1.0 / jnp.sum(p, axis=-1, keepdims=True), (2 * A_TILE, LANES))

    def values(u, slot):
        _, start = window(u)
        vw = vp_ref[pl.ds(start, A_WIN), :]
        o = jnp.dot(p_ref[slot], vw, preferred_element_type=F32) * rinv_ref[slot]
        out = jnp.where(first, o[:A_TILE], o[A_TILE:])
        o_ref[0, pl.ds(pl.multiple_of(u * A_TILE, A_TILE), A_TILE), :] = out.astype(o_ref.dtype)

    scores(0, 0, True)
    scores(1, 1, True)
    softmax(0)

    def pair(k, head_of_sequence):
        values(2 * k - 2, 0)
        softmax(1)
        scores(2 * k, 0, head_of_sequence)
        values(2 * k - 1, 1)
        softmax(0)
        scores(2 * k + 1, 1, head_of_sequence)

    n_head = A_PAD // A_TILE // 2
    for k in range(1, n_head):
        pair(k, True)

    def body(k, _):
        pair(k, False)
        return 0

    lax.fori_loop(n_head, A_SUBTILES // 2, body, 0)
    values(A_SUBTILES - 2, 0)
    softmax(1)
    values(A_SUBTILES - 1, 1)


def band_attention(proj, bias, col0):
    b, s, _ = proj.shape
    nq = col0 // LANES
    nk = (col0 + A_W) // LANES
    nv = (col0 + 2 * A_W) // LANES
    tq = A_SUBTILES * A_TILE
    return pl.pallas_call(
        _band_attn_kernel,
        out_shape=jax.ShapeDtypeStruct((b, s, A_W), BF16),
        grid=(b, A_W // LANES, s // tq),
        in_specs=[
            pl.BlockSpec((1, tq, LANES), lambda bi, p, t: (bi, t, nq + p)),
            pl.BlockSpec((1, s, LANES), lambda bi, p, t: (bi, 0, nk + p)),
            pl.BlockSpec((1, s, LANES), lambda bi, p, t: (bi, 0, nv + p)),
            pl.BlockSpec((2, A_TILE, A_WIN), lambda bi, p, t: (p, 0, 0)),
        ],
        out_specs=pl.BlockSpec((1, tq, LANES), lambda bi, p, t: (bi, t, p)),
        scratch_shapes=[
            pltpu.VMEM((s + A_PAD, LANES), BF16),
            pltpu.VMEM((s + A_PAD, LANES), BF16),
            pltpu.VMEM((2, 2 * A_TILE, A_WIN), F32),
            pltpu.VMEM((2, 2 * A_TILE, A_WIN), BF16),
            pltpu.VMEM((2, 2 * A_TILE, LANES), F32),
        ],
        compiler_params=_params(("arbitrary", "arbitrary", "arbitrary")),
        name="band_attention",
    )(proj, proj, proj, bias)


def band_bias(rel_bias):
    h = rel_bias.shape[0]
    n_f = A_WIN + A_TILE - 1
    n_const = A_PAD + A_TILE - 1 - REL_CLIP
    far = jnp.broadcast_to(rel_bias[:, 2 * REL_CLIP:], (h, n_const))
    near = rel_bias[:, 2 * REL_CLIP + 1 - (n_f - n_const):][:, ::-1]
    f = jnp.concatenate([far, near], axis=1)
    period = n_f + 1
    g = jnp.concatenate([f[:, A_TILE - 1:], jnp.zeros((h, 1), F32), f[:, :A_TILE - 1]], axis=1)
    flat = jnp.tile(g, (1, A_TILE))[:, :A_TILE * (period - 1)]
    toep = flat.reshape(h, A_TILE, period - 1)[:, :, :A_WIN]
    r = jnp.arange(A_TILE)[:, None]
    c = jnp.arange(A_WIN)[None, :]
    dchunk = c // CHUNK - r // CHUNK
    ok = (dchunk >= 0) & (dchunk <= N_LEFT_CHUNKS)
    return jnp.where(ok[None], toep, NEG_BIG).astype(F32)


def _stick_kernel(q_ref, k_ref, v_ref, o_ref, acc_ref, carry_ref):
    qi = pl.program_id(2)
    t = SB_TILE
    ns = SB_SUBTILES
    lane = lax.broadcasted_iota(jnp.int32, (1, LANES), 1)
    first = lane < HEAD_DIM
    row = lax.broadcasted_iota(jnp.int32, (t, t), 0)
    col = lax.broadcasted_iota(jnp.int32, (t, t), 1)
    causal = col < row
    true = jnp.ones((t, t), jnp.bool_)
    suffix = jnp.where(row >= col, 1.0, 0.0).astype(BF16)
    zero = jnp.zeros((t, LANES), BF16)

    acc_ref[...] = jnp.zeros_like(acc_ref)
    carry_ref[...] = jnp.zeros_like(carry_ref)

    def lanes_of(pair):
        return slice(pair * LANES, (pair + 1) * LANES)

    def split_heads(x):
        return jnp.concatenate([jnp.where(first, x, zero), jnp.where(first, zero, x)], axis=0)

    def scores(pair, j, d, mask):
        start = pl.multiple_of(j * t, t)
        kcat = split_heads(k_ref[0, pl.ds(start, t), lanes_of(pair)])
        q2 = q_ref[0, d * t:, lanes_of(pair)] * (HEAD_DIM ** -0.5)
        z = lax.dot_general(q2, kcat, _NT, preferred_element_type=F32)
        sp = jnp.maximum(z, 0.0) + jnp.log(1.0 + jnp.exp2(jnp.abs(z) * (-LOG2E)))
        if mask is not None:
            sp = jnp.where(mask, sp, 0.0)
        spb = sp.astype(BF16)
        return z - (sp - spb.astype(F32)), spb

    def weights(pair, j, d, mask, tz, spb):
        r0 = d * t
        n = (ns - d) * t
        start = pl.multiple_of(j * t, t)
        vcat = split_heads(v_ref[0, pl.ds(start, t), lanes_of(pair)])
        c = [jnp.dot(spb[:, h * t:(h + 1) * t], suffix, preferred_element_type=F32) for h in range(2)]
        carry = [carry_ref[pair, h, r0:, :] for h in range(2)]
        reps = t // LANES
        shift = jnp.concatenate([c[0]] + [c[1]], axis=1) + jnp.concatenate(
            [carry[0]] * reps + [carry[1]] * reps, axis=1)
        w = jnp.exp(tz - shift)
        if mask is not None:
            w = jnp.where(mask, w, 0.0)
        for h in range(2):
            carry_ref[pair, h, r0:, :] = carry[h] + jnp.broadcast_to(c[h][:, 0:1], (n, LANES))
        acc_ref[pair, r0:, :] += jnp.dot(w.astype(BF16), vcat, preferred_element_type=F32)

    for pair in range(SB_PAIRS):
        for d in range(ns - 1, -1, -1):
            mask = jnp.concatenate([causal] + [true] * (ns - d - 1), axis=0)
            mask = jnp.concatenate([mask, mask], axis=1)
            tz, spb = scores(pair, ns * qi + d, d, mask)
            weights(pair, ns * qi + d, d, mask, tz, spb)

    def saturated(pair):
        low = jnp.min(jnp.minimum(carry_ref[pair, 0], carry_ref[pair, 1]), axis=0, keepdims=True)
        return (jnp.min(low) >= SB_EXIT).astype(jnp.int32)

    @pl.when(qi > 0)
    def _():
        for pair in range(SB_PAIRS):
            tz, spb = scores(pair, ns * qi - 1, 0, None)
            weights(pair, ns * qi - 1, 0, None, tz, spb)
        for pair in range(SB_PAIRS):
            def cond(state):
                i, done = state
                return jnp.logical_and(i < ns * qi, done == 0)

            def body(state, pair=pair):
                i, _ = state
                tz, spb = scores(pair, ns * qi - 1 - i, 0, None)
                weights(pair, ns * qi - 1 - i, 0, None, tz, spb)
                return i + 1, saturated(pair)

            lax.while_loop(cond, body, (jnp.int32(1), saturated(pair)))

    for pair in range(SB_PAIRS):
        o_ref[0, :, lanes_of(pair)] = acc_ref[pair].astype(o_ref.dtype)


def stick_breaking_attention(proj, col0):
    b, s, _ = proj.shape
    width = SB_PAIRS * LANES
    nq = col0 // width
    nk = (col0 + B_W) // width
    nv = (col0 + 2 * B_W) // width
    tq = SB_SUBTILES * SB_TILE
    return pl.pallas_call(
        _stick_kernel,
        out_shape=jax.ShapeDtypeStruct((b, s, B_W), BF16),
        grid=(b, B_W // width, s // tq),
        in_specs=[
            pl.BlockSpec((1, tq, width), lambda bi, p, t: (bi, t, nq + p)),
            pl.BlockSpec((1, s, width), lambda bi, p, t: (bi, 0, nk + p)),
            pl.BlockSpec((1, s, width), lambda bi, p, t: (bi, 0, nv + p)),
        ],
        out_specs=pl.BlockSpec((1, tq, width), lambda bi, p, t: (bi, t, p)),
        scratch_shapes=[
            pltpu.VMEM((SB_PAIRS, tq, LANES), F32),
            pltpu.VMEM((SB_PAIRS, 2, tq, LANES), F32),
        ],
        compiler_params=_params(("arbitrary", "arbitrary", "arbitrary")),
        name="stick_breaking",
    )(proj, proj, proj)


def _rglru_kernel(x_ref, g_ref, win_ref, cw_ref, cb_ref, wa_ref, ba_ref, wi_ref, bi_ref, lam_ref,
                  y_ref, xpad_ref, a_ref, u_ref, p_ref, hl_ref, h_ref):
    t = pl.program_id(1)
    n = REC_TILE
    steps = n // 8
    width = y_ref.shape[2]
    n_slabs = width // LANES

    @pl.when(t == 0)
    def _():
        xpad_ref[0:8, :] = jnp.zeros((8, width), F32)
        h_ref[...] = jnp.zeros_like(h_ref)

    hn = _rms(x_ref[0], g_ref[...]).astype(BF16)
    gate = jnp.dot(hn, win_ref[:, :width], preferred_element_type=F32)
    c = 0.7978845608028654
    gate = 0.5 * gate * (1.0 + jnp.tanh(c * (gate + 0.044715 * (gate * gate * gate))))
    xpad_ref[8:8 + n, :] = jnp.dot(hn, win_ref[:, width:], preferred_element_type=F32)
    cw = cw_ref[...]
    xc = (cb_ref[...]
          + cw[3:4] * xpad_ref[8:8 + n, :]
          + cw[2:3] * xpad_ref[7:7 + n, :]
          + cw[1:2] * xpad_ref[6:6 + n, :]
          + cw[0:1] * xpad_ref[5:5 + n, :])
    xpad_ref[0:8, :] = xpad_ref[n:n + 8, :]
    xcb = xc.astype(BF16)
    lam = lam_ref[...]
    sp_lam = jnp.maximum(-lam, 0.0) + jnp.log1p(jnp.exp(-jnp.abs(lam)))
    for blk in range(LRU_BLOCKS):
        sl = slice(blk * LRU_BLOCK_W, (blk + 1) * LRU_BLOCK_W)
        xb = xcb[:, sl]
        pre_r = jnp.dot(xb, wa_ref[blk], preferred_element_type=F32) + ba_ref[:, sl]
        pre_i = jnp.dot(xb, wi_ref[blk], preferred_element_type=F32) + bi_ref[:, sl]
        r = 0.5 * jnp.tanh(0.5 * pre_r) + 0.5
        i = 0.5 * jnp.tanh(0.5 * pre_i) + 0.5
        log_a = (-LRU_C * r) * sp_lam[:, sl]
        a = jnp.exp(log_a)
        m2 = -jnp.tanh(log_a) * (a * a + 1.0)
        mult = m2 * lax.rsqrt(jnp.maximum(m2, 1e-30))
        u = mult * (i * xc[:, sl])
        for k in range(LRU_BLOCK_W // LANES):
            slab = blk * (LRU_BLOCK_W // LANES) + k
            for ch in range(8):
                rows = slice(ch * steps, (ch + 1) * steps)
                dst = slice(ch * REC_PITCH, ch * REC_PITCH + steps)
                a_ref[slab, dst, :] = a[rows, k * LANES:(k + 1) * LANES]
                u_ref[slab, dst, :] = u[rows, k * LANES:(k + 1) * LANES]

    def step(l, state):
        rows = pl.ds(l, 8, stride=REC_PITCH)
        new = []
        for slab in range(n_slabs):
            h, p = state[slab]
            av = a_ref[slab, rows, :]
            h = av * h + u_ref[slab, rows, :]
            p = p * av
            hl_ref[slab, rows, :] = h
            p_ref[slab, rows, :] = p
            new.append((h, p))
        return tuple(new)

    zeros = jnp.zeros((8, LANES), F32)
    ones = jnp.ones((8, LANES), F32)
    ends = lax.fori_loop(0, steps, step, tuple((zeros, ones) for _ in range(n_slabs)), unroll=4)

    rowid = lax.broadcasted_iota(jnp.int32, (8, LANES), 0)
    for slab in range(n_slabs):
        lanes = slice(slab * LANES, (slab + 1) * LANES)
        b, a = ends[slab]
        for d in (1, 2, 4):
            keep = rowid >= d
            a_sh = jnp.where(keep, pltpu.roll(a, d, 0), 1.0)
            b_sh = jnp.where(keep, pltpu.roll(b, d, 0), 0.0)
            b = a * b_sh + b
            a = a * a_sh
        h_in = h_ref[:, lanes]
        after = b + a * h_in
        h_ref[:, lanes] = after[7:8, :]
        start = jnp.where(rowid >= 1, pltpu.roll(after, 1, 0), h_in)
        for ch in range(8):
            src = slice(ch * REC_PITCH, ch * REC_PITCH + steps)
            rows = slice(ch * steps, (ch + 1) * steps)
            hs = hl_ref[slab, src, :] + p_ref[slab, src, :] * start[ch:ch + 1, :]
            y_ref[0, rows, lanes] = (hs * gate[rows, lanes]).astype(y_ref.dtype)


def rglru_mixer(x, g, w_in_stack, conv_w, conv_b, w_a_stack, b_a, w_i_stack, b_i, lam, j):
    b, s, d = x.shape
    width = w_in_stack.shape[2] // 2
    const2 = lambda bi, t: (0, 0)
    scan_rows = 8 * REC_PITCH
    return pl.pallas_call(
        _rglru_kernel,
        out_shape=jax.ShapeDtypeStruct((b, s, width), BF16),
        grid=(b, s // REC_TILE),
        in_specs=[
            pl.BlockSpec((1, REC_TILE, d), lambda bi, t: (bi, t, 0)),
            pl.BlockSpec((1, d), const2),
            _layer_block(w_in_stack, j),
            pl.BlockSpec(conv_w.shape, const2),
            pl.BlockSpec((1, width), const2),
            _layer_block(w_a_stack, j),
            pl.BlockSpec((1, width), const2),
            _layer_block(w_i_stack, j),
            pl.BlockSpec((1, width), const2),
            pl.BlockSpec((1, width), const2),
        ],
        out_specs=pl.BlockSpec((1, REC_TILE, width), lambda bi, t: (bi, t, 0)),
        scratch_shapes=[
            pltpu.VMEM((REC_TILE + 8, width), F32),
            pltpu.VMEM((width // LANES, scan_rows, LANES), F32),
            pltpu.VMEM((width // LANES, scan_rows, LANES), F32),
            pltpu.VMEM((width // LANES, scan_rows, LANES), F32),
            pltpu.VMEM((width // LANES, scan_rows, LANES), F32),
            pltpu.VMEM((1, width), F32),
        ],
        compiler_params=_params(("arbitrary", "arbitrary")),
        name="rglru_mixer",
    )(x, g, w_in_stack, conv_w, conv_b, w_a_stack, b_a, w_i_stack, b_i, lam)


def kernel(x, attn_w_in, attn_rel_bias, attn_w_out, rg_w_in, rg_conv_w, rg_conv_b, rg_w_a, rg_b_a,
           rg_w_i, rg_b_i, rg_lambda, rg_w_out, norm_mix_pre, norm_mix_post, norm_ffn_pre,
           norm_ffn_post, ffn_w_gate, ffn_w_up, ffn_w_down):
    b, s, d = x.shape
    depth = norm_mix_pre.shape[0]
    xf = x.reshape(b * s, d)
    row = lambda v: v.reshape(1, -1)
    attn_w_in, attn_w_out, rg_w_in, rg_w_a, rg_w_i, rg_w_out, ffn_w_gate, ffn_w_up, ffn_w_down = (
        w.astype(BF16) for w in (attn_w_in, attn_w_out, rg_w_in, rg_w_a, rg_w_i, rg_w_out,
                                 ffn_w_gate, ffn_w_up, ffn_w_down))
    for layer in range(depth):
        j = layer // 2
        g_pre = row(norm_mix_pre[layer])
        if layer % 2 == 0:
            proj = norm_linear(xf, g_pre, attn_w_in, j, BF16).reshape(b, s, -1)
            out_a = band_attention(proj, band_bias(attn_rel_bias[j]), 0)
            out_b = stick_breaking_attention(proj, 3 * A_W)
            ins = [out_a.reshape(b * s, A_W), out_b.reshape(b * s, B_W)]
            w_out = attn_w_out
        else:
            width = rg_w_out.shape[1]
            y = rglru_mixer(
                xf.reshape(b, s, d), g_pre, rg_w_in,
                rg_conv_w[j].reshape(-1, width), row(rg_conv_b[j]),
                rg_w_a, row(rg_b_a[j]), rg_w_i, row(rg_b_i[j]), row(rg_lambda[j]), j)
            ins = [y.reshape(b * s, width)]
            w_out = rg_w_out
        xf = mixer_out_ffn(
            xf, row(norm_mix_post[layer]), ins, w_out, j,
            row(norm_ffn_pre[layer]), row(norm_ffn_post[layer]),
            ffn_w_gate, ffn_w_up, ffn_w_down, layer)
    return xf.reshape(b, s, d)
```

```python
import functools

import jax
import jax.numpy as jnp
from jax import lax
from jax.experimental import pallas as pl
from jax.experimental.pallas import tpu as pltpu

F32 = jnp.float32
BF16 = jnp.bfloat16

HEAD_DIM = 64
CHUNK = 64
N_LEFT_CHUNKS = 8
REL_CLIP = 256
A_W = 512
B_W = 512
LRU_BLOCKS = 4
LRU_BLOCK_W = 256
LRU_C = 8.0
RMS_EPS = 1e-6

LANES = 128
SUBLANES = 8
VMEM_LIMIT = 56 * 1024 * 1024

ROW_TILE = 512
SB_TILE = 256
SB_SUBTILES = 2
SB_PAIRS = 4
SB_EXIT = 128.0
A_TILE = 2 * CHUNK
A_SUBTILES = 64
A_WIN = (N_LEFT_CHUNKS + 2) * CHUNK
A_PAD = N_LEFT_CHUNKS * CHUNK
REC_TILE = 512
REC_PITCH = REC_TILE // SUBLANES + SUBLANES
NEG_BIG = -1e30
LOG2E = 1.4426950408889634

_NT = (((1,), (1,)), ((), ()))


def _params(sem):
    return pltpu.CompilerParams(dimension_semantics=sem, vmem_limit_bytes=VMEM_LIMIT)


def _rms(x, g):
    ms = jnp.mean(x * x, axis=-1, keepdims=True)
    return x * lax.rsqrt(ms + RMS_EPS) * g


def _sigmoid(x):
    return 1.0 / (1.0 + jnp.exp(-x))


def _resident(shape):
    return pl.BlockSpec(shape, lambda *_: (0,) * len(shape), pipeline_mode=pl.Buffered(1))


def _layer_block(stack, layer, rows=None, row_block=0):
    shape = stack.shape[1:] if rows is None else (rows,) + stack.shape[2:]
    index = (layer, row_block) + (0,) * (len(shape) - 1)
    return pl.BlockSpec((None,) + shape, lambda *_: index, pipeline_mode=pl.Buffered(1))


def _norm_linear_kernel(x_ref, g_ref, w_ref, o_ref):
    h = _rms(x_ref[...], g_ref[...]).astype(BF16)
    o_ref[...] = jnp.dot(h, w_ref[...], preferred_element_type=F32).astype(o_ref.dtype)


def norm_linear(x, g, w_stack, layer, out_dtype):
    m, d = x.shape
    n = w_stack.shape[2]
    return pl.pallas_call(
        _norm_linear_kernel,
        out_shape=jax.ShapeDtypeStruct((m, n), out_dtype),
        grid=(m // ROW_TILE,),
        in_specs=[
            pl.BlockSpec((ROW_TILE, d), lambda i: (i, 0)),
            pl.BlockSpec((1, d), lambda i: (0, 0)),
            _layer_block(w_stack, layer),
        ],
        out_specs=pl.BlockSpec((ROW_TILE, n), lambda i: (i, 0)),
        compiler_params=_params(("arbitrary",)),
        name="norm_linear",
    )(x, g, w_stack)


def _out_ffn_kernel(n_in, *refs):
    x_ref, gmix_ref, gpre_ref, gpost_ref, wg_ref, wu_ref, wd_ref = refs[:7]
    ins = refs[7:7 + n_in]
    ws = refs[7 + n_in:7 + 2 * n_in]
    o_ref = refs[7 + 2 * n_in]
    m = jnp.dot(ins[0][...], ws[0][...], preferred_element_type=F32)
    for a, w in zip(ins[1:], ws[1:]):
        m = m + jnp.dot(a[...], w[...], preferred_element_type=F32)
    x = x_ref[...] + _rms(m, gmix_ref[...])
    h = _rms(x, gpre_ref[...]).astype(BF16)
    gate = jnp.dot(h, wg_ref[...], preferred_element_type=F32)
    up = jnp.dot(h, wu_ref[...], preferred_element_type=F32)
    act = (gate * _sigmoid(gate) * up).astype(BF16)
    f = jnp.dot(act, wd_ref[...], preferred_element_type=F32)
    o_ref[...] = x + _rms(f, gpost_ref[...])


def mixer_out_ffn(x, gmix, ins, w_out_stack, j, gpre, gpost, wg_stack, wu_stack, wd_stack, layer):
    m, d = x.shape
    n_in = len(ins)
    in_specs = [pl.BlockSpec((ROW_TILE, d), lambda i: (i, 0))]
    in_specs += [_resident(a.shape) for a in (gmix, gpre, gpost)]
    in_specs += [_layer_block(w, layer) for w in (wg_stack, wu_stack, wd_stack)]
    in_specs += [pl.BlockSpec((ROW_TILE, a.shape[1]), lambda i: (i, 0)) for a in ins]
    in_specs += [_layer_block(w_out_stack, j, a.shape[1], k) for k, a in enumerate(ins)]
    return pl.pallas_call(
        functools.partial(_out_ffn_kernel, n_in),
        out_shape=jax.ShapeDtypeStruct((m, d), F32),
        grid=(m // ROW_TILE,),
        in_specs=in_specs,
        out_specs=pl.BlockSpec((ROW_TILE, d), lambda i: (i, 0)),
        compiler_params=_params(("arbitrary",)),
        name="mixer_out_ffn",
    )(x, gmix, gpre, gpost, wg_stack, wu_stack, wd_stack, *ins, *([w_out_stack] * n_in))


def _band_attn_kernel(q_ref, k_ref, v_ref, bias_ref, o_ref, kp_ref, vp_ref, s_ref, p_ref, rinv_ref):
    t = pl.program_id(2)

    @pl.when(t == 0)
    def _():
        kp_ref[0:A_PAD, :] = jnp.zeros((A_PAD, LANES), BF16)
        vp_ref[0:A_PAD, :] = jnp.zeros((A_PAD, LANES), BF16)
        kp_ref[A_PAD:, :] = k_ref[0]
        vp_ref[A_PAD:, :] = v_ref[0]

    lane = lax.broadcasted_iota(jnp.int32, (1, LANES), 1)
    first = lane < HEAD_DIM
    col = lax.broadcasted_iota(jnp.int32, (1, A_WIN), 1)
    bias = jnp.concatenate([bias_ref[0], bias_ref[1]], axis=0)
    zero = jnp.zeros((A_TILE, LANES), BF16)

    def window(u):
        tile = t * A_SUBTILES + u
        return tile, pl.multiple_of(tile * A_TILE, A_TILE)

    def scores(u, slot, head_of_sequence):
        tile, start = window(u)
        kw = kp_ref[pl.ds(start, A_WIN), :]
        q2 = q_ref[0, pl.ds(pl.multiple_of(u * A_TILE, A_TILE), A_TILE), :] * (HEAD_DIM ** -0.5)
        qs = jnp.concatenate([jnp.where(first, q2, zero), jnp.where(first, zero, q2)], axis=0)
        s = lax.dot_general(qs, kw, _NT, preferred_element_type=F32) + bias
        if head_of_sequence:
            s = jnp.where(col >= (A_PAD - tile * A_TILE), s, NEG_BIG)
        s_ref[slot] = s

    def softmax(slot):
        s = s_ref[slot]
        p = jnp.exp(s - jnp.max(s, axis=-1, keepdims=True))
        p_ref[slot] = p.astype(BF16)
        rinv_ref[slot] = jnp.broadcast_to(1.0 / jnp.sum(p, axis=-1, keepdims=True), (2 * A_TILE, LANES))

    def values(u, slot):
        _, start = window(u)
        vw = vp_ref[pl.ds(start, A_WIN), :]
        o = jnp.dot(p_ref[slot], vw, preferred_element_type=F32) * rinv_ref[slot]
        out = jnp.where(first, o[:A_TILE], o[A_TILE:])
        o_ref[0, pl.ds(pl.multiple_of(u * A_TILE, A_TILE), A_TILE), :] = out.astype(o_ref.dtype)

    scores(0, 0, True)
    scores(1, 1, True)
    softmax(0)

    def pair(k, head_of_sequence):
        values(2 * k - 2, 0)
        softmax(1)
        scores(2 * k, 0, head_of_sequence)
        values(2 * k - 1, 1)
        softmax(0)
        scores(2 * k + 1, 1, head_of_sequence)

    n_head = A_PAD // A_TILE // 2
    for k in range(1, n_head):
        pair(k, True)

    def body(k, _):
        pair(k, False)
        return 0

    lax.fori_loop(n_head, A_SUBTILES // 2, body, 0)
    values(A_SUBTILES - 2, 0)
    softmax(1)
    values(A_SUBTILES - 1, 1)


def band_attention(proj, bias, col0):
    b, s, _ = proj.shape
    nq = col0 // LANES
    nk = (col0 + A_W) // LANES
    nv = (col0 + 2 * A_W) // LANES
    tq = A_SUBTILES * A_TILE
    return pl.pallas_call(
        _band_attn_kernel,
        out_shape=jax.ShapeDtypeStruct((b, s, A_W), BF16),
        grid=(b, A_W // LANES, s // tq),
        in_specs=[
            pl.BlockSpec((1, tq, LANES), lambda bi, p, t: (bi, t, nq + p)),
            pl.BlockSpec((1, s, LANES), lambda bi, p, t: (bi, 0, nk + p)),
            pl.BlockSpec((1, s, LANES), lambda bi, p, t: (bi, 0, nv + p)),
            pl.BlockSpec((2, A_TILE, A_WIN), lambda bi, p, t: (p, 0, 0)),
        ],
        out_specs=pl.BlockSpec((1, tq, LANES), lambda bi, p, t: (bi, t, p)),
        scratch_shapes=[
            pltpu.VMEM((s + A_PAD, LANES), BF16),
            pltpu.VMEM((s + A_PAD, LANES), BF16),
            pltpu.VMEM((2, 2 * A_TILE, A_WIN), F32),
            pltpu.VMEM((2, 2 * A_TILE, A_WIN), BF16),
            pltpu.VMEM((2, 2 * A_TILE, LANES), F32),
        ],
        compiler_params=_params(("arbitrary", "arbitrary", "arbitrary")),
        name="band_attention",
    )(proj, proj, proj, bias)


def band_bias(rel_bias):
    h = rel_bias.shape[0]
    n_f = A_WIN + A_TILE - 1
    n_const = A_PAD + A_TILE - 1 - REL_CLIP
    far = jnp.broadcast_to(rel_bias[:, 2 * REL_CLIP:], (h, n_const))
    near = rel_bias[:, 2 * REL_CLIP + 1 - (n_f - n_const):][:, ::-1]
    f = jnp.concatenate([far, near], axis=1)
    period = n_f + 1
    g = jnp.concatenate([f[:, A_TILE - 1:], jnp.zeros((h, 1), F32), f[:, :A_TILE - 1]], axis=1)
    flat = jnp.tile(g, (1, A_TILE))[:, :A_TILE * (period - 1)]
    toep = flat.reshape(h, A_TILE, period - 1)[:, :, :A_WIN]
    r = jnp.arange(A_TILE)[:, None]
    c = jnp.arange(A_WIN)[None, :]
    dchunk = c // CHUNK - r // CHUNK
    ok = (dchunk >= 0) & (dchunk <= N_LEFT_CHUNKS)
    return jnp.where(ok[None], toep, NEG_BIG).astype(F32)


def _stick_kernel(q_ref, k_ref, v_ref, o_ref, acc_ref, carry_ref):
    qi = pl.program_id(2)
    t = SB_TILE
    ns = SB_SUBTILES
    lane = lax.broadcasted_iota(jnp.int32, (1, LANES), 1)
    first = lane < HEAD_DIM
    row = lax.broadcasted_iota(jnp.int32, (t, t), 0)
    col = lax.broadcasted_iota(jnp.int32, (t, t), 1)
    causal = col < row
    true = jnp.ones((t, t), jnp.bool_)
    suffix = jnp.where(row >= col, 1.0, 0.0).astype(BF16)
    zero = jnp.zeros((t, LANES), BF16)

    acc_ref[...] = jnp.zeros_like(acc_ref)
    carry_ref[...] = jnp.zeros_like(carry_ref)

    def lanes_of(pair):
        return slice(pair * LANES, (pair + 1) * LANES)

    def split_heads(x):
        return jnp.concatenate([jnp.where(first, x, zero), jnp.where(first, zero, x)], axis=0)

    def scores(pair, j, d, mask):
        start = pl.multiple_of(j * t, t)
        kcat = split_heads(k_ref[0, pl.ds(start, t), lanes_of(pair)])
        q2 = q_ref[0, d * t:, lanes_of(pair)] * (HEAD_DIM ** -0.5)
        z = lax.dot_general(q2, kcat, _NT, preferred_element_type=F32)
        sp = jnp.maximum(z, 0.0) + jnp.log(1.0 + jnp.exp2(jnp.abs(z) * (-LOG2E)))
        if mask is not None:
            sp = jnp.where(mask, sp, 0.0)
        spb = sp.astype(BF16)
        return z - (sp - spb.astype(F32)), spb

    def weights(pair, j, d, mask, tz, spb):
        r0 = d * t
        n = (ns - d) * t
        start = pl.multiple_of(j * t, t)
        vcat = split_heads(v_ref[0, pl.ds(start, t), lanes_of(pair)])
        c = [jnp.dot(spb[:, h * t:(h + 1) * t], suffix, preferred_element_type=F32) for h in range(2)]
        carry = [carry_ref[pair, h, r0:, :] for h in range(2)]
        reps = t // LANES
        shift = jnp.concatenate([c[0]] + [c[1]], axis=1) + jnp.concatenate(
            [carry[0]] * reps + [carry[1]] * reps, axis=1)
        w = jnp.exp(tz - shift)
        if mask is not None:
            w = jnp.where(mask, w, 0.0)
        for h in range(2):
            carry_ref[pair, h, r0:, :] = carry[h] + jnp.broadcast_to(c[h][:, 0:1], (n, LANES))
        acc_ref[pair, r0:, :] += jnp.dot(w.astype(BF16), vcat, preferred_element_type=F32)

    for pair in range(SB_PAIRS):
        for d in range(ns - 1, -1, -1):
            mask = jnp.concatenate([causal] + [true] * (ns - d - 1), axis=0)
            mask = jnp.concatenate([mask, mask], axis=1)
            tz, spb = scores(pair, ns * qi + d, d, mask)
            weights(pair, ns * qi + d, d, mask, tz, spb)

    def saturated(pair):
        low = jnp.min(jnp.minimum(carry_ref[pair, 0], carry_ref[pair, 1]), axis=0, keepdims=True)
        return (jnp.min(low) >= SB_EXIT).astype(jnp.int32)

    @pl.when(qi > 0)
    def _():
        for pair in range(SB_PAIRS):
            tz, spb = scores(pair, ns * qi - 1, 0, None)
            weights(pair, ns * qi - 1, 0, None, tz, spb)
        for pair in range(SB_PAIRS):
            def cond(state):
                i, done = state
                return jnp.logical_and(i < ns * qi, done == 0)

            def body(state, pair=pair):
                i, _ = state
                tz, spb = scores(pair, ns * qi - 1 - i, 0, None)
                weights(pair, ns * qi - 1 - i, 0, None, tz, spb)
                return i + 1, saturated(pair)

            lax.while_loop(cond, body, (jnp.int32(1), saturated(pair)))

    for pair in range(SB_PAIRS):
        o_ref[0, :, lanes_of(pair)] = acc_ref[pair].astype(o_ref.dtype)


def stick_breaking_attention(proj, col0):
    b, s, _ = proj.shape
    width = SB_PAIRS * LANES
    nq = col0 // width
    nk = (col0 + B_W) // width
    nv = (col0 + 2 * B_W) // width
    tq = SB_SUBTILES * SB_TILE
    return pl.pallas_call(
        _stick_kernel,
        out_shape=jax.ShapeDtypeStruct((b, s, B_W), BF16),
        grid=(b, B_W // width, s // tq),
        in_specs=[
            pl.BlockSpec((1, tq, width), lambda bi, p, t: (bi, t, nq + p)),
            pl.BlockSpec((1, s, width), lambda bi, p, t: (bi, 0, nk + p)),
            pl.BlockSpec((1, s, width), lambda bi, p, t: (bi, 0, nv + p)),
        ],
        out_specs=pl.BlockSpec((1, tq, width), lambda bi, p, t: (bi, t, p)),
        scratch_shapes=[
            pltpu.VMEM((SB_PAIRS, tq, LANES), F32),
            pltpu.VMEM((SB_PAIRS, 2, tq, LANES), F32),
        ],
        compiler_params=_params(("arbitrary", "arbitrary", "arbitrary")),
        name="stick_breaking",
    )(proj, proj, proj)


def _rglru_kernel(x_ref, g_ref, win_ref, cw_ref, cb_ref, wa_ref, ba_ref, wi_ref, bi_ref, lam_ref,
                  y_ref, xpad_ref, a_ref, u_ref, p_ref, hl_ref, h_ref):
    t = pl.program_id(1)
    n = REC_TILE
    steps = n // SUBLANES
    width = y_ref.shape[2]
    n_slabs = width // LANES

    @pl.when(t == 0)
    def _():
        xpad_ref[0:SUBLANES, :] = jnp.zeros((SUBLANES, width), F32)
        h_ref[...] = jnp.zeros_like(h_ref)

    hn = _rms(x_ref[0], g_ref[...]).astype(BF16)
    gate = jnp.dot(hn, win_ref[:, :width], preferred_element_type=F32)
    c = 0.7978845608028654
    gate = 0.5 * gate * (1.0 + jnp.tanh(c * (gate + 0.044715 * (gate * gate * gate))))
    hist = SUBLANES
    xpad_ref[hist:hist + n, :] = jnp.dot(hn, win_ref[:, width:], preferred_element_type=F32)
    cw = cw_ref[...]
    xc = (cb_ref[...]
          + cw[3:4] * xpad_ref[hist:hist + n, :]
          + cw[2:3] * xpad_ref[hist - 1:hist - 1 + n, :]
          + cw[1:2] * xpad_ref[hist - 2:hist - 2 + n, :]
          + cw[0:1] * xpad_ref[hist - 3:hist - 3 + n, :])
    xpad_ref[0:hist, :] = xpad_ref[n:n + hist, :]
    xcb = xc.astype(BF16)
    lam = lam_ref[...]
    sp_lam = jnp.maximum(-lam, 0.0) + jnp.log1p(jnp.exp(-jnp.abs(lam)))
    for blk in range(LRU_BLOCKS):
        sl = slice(blk * LRU_BLOCK_W, (blk + 1) * LRU_BLOCK_W)
        xb = xcb[:, sl]
        pre_r = jnp.dot(xb, wa_ref[blk], preferred_element_type=F32) + ba_ref[:, sl]
        pre_i = jnp.dot(xb, wi_ref[blk], preferred_element_type=F32) + bi_ref[:, sl]
        r = 0.5 * jnp.tanh(0.5 * pre_r) + 0.5
        i = 0.5 * jnp.tanh(0.5 * pre_i) + 0.5
        log_a = (-LRU_C * r) * sp_lam[:, sl]
        a = jnp.exp(log_a)
        m2 = -jnp.tanh(log_a) * (a * a + 1.0)
        mult = m2 * lax.rsqrt(jnp.maximum(m2, 1e-30))
        u = mult * (i * xc[:, sl])
        for k in range(LRU_BLOCK_W // LANES):
            slab = blk * (LRU_BLOCK_W // LANES) + k
            for ch in range(SUBLANES):
                rows = slice(ch * steps, (ch + 1) * steps)
                dst = slice(ch * REC_PITCH, ch * REC_PITCH + steps)
                a_ref[slab, dst, :] = a[rows, k * LANES:(k + 1) * LANES]
                u_ref[slab, dst, :] = u[rows, k * LANES:(k + 1) * LANES]

    def step(l, state):
        rows = pl.ds(l, SUBLANES, stride=REC_PITCH)
        new = []
        for slab in range(n_slabs):
            h, p = state[slab]
            av = a_ref[slab, rows, :]
            h = av * h + u_ref[slab, rows, :]
            p = p * av
            hl_ref[slab, rows, :] = h
            p_ref[slab, rows, :] = p
            new.append((h, p))
        return tuple(new)

    zeros = jnp.zeros((SUBLANES, LANES), F32)
    ones = jnp.ones((SUBLANES, LANES), F32)
    ends = lax.fori_loop(0, steps, step, tuple((zeros, ones) for _ in range(n_slabs)), unroll=4)

    rowid = lax.broadcasted_iota(jnp.int32, (SUBLANES, LANES), 0)
    for slab in range(n_slabs):
        lanes = slice(slab * LANES, (slab + 1) * LANES)
        b, a = ends[slab]
        for d in (1, 2, 4):
            keep = rowid >= d
            a_sh = jnp.where(keep, pltpu.roll(a, d, 0), 1.0)
            b_sh = jnp.where(keep, pltpu.roll(b, d, 0), 0.0)
            b = a * b_sh + b
            a = a * a_sh
        h_in = h_ref[:, lanes]
        after = b + a * h_in
        h_ref[:, lanes] = after[SUBLANES - 1:, :]
        start = jnp.where(rowid >= 1, pltpu.roll(after, 1, 0), h_in)
        for ch in range(SUBLANES):
            src = slice(ch * REC_PITCH, ch * REC_PITCH + steps)
            rows = slice(ch * steps, (ch + 1) * steps)
            hs = hl_ref[slab, src, :] + p_ref[slab, src, :] * start[ch:ch + 1, :]
            y_ref[0, rows, lanes] = (hs * gate[rows, lanes]).astype(y_ref.dtype)


def rglru_mixer(x, g, w_in_stack, conv_w, conv_b, w_a_stack, b_a, w_i_stack, b_i, lam, j):
    b, s, d = x.shape
    width = w_in_stack.shape[2] // 2
    const2 = lambda bi, t: (0, 0)
    scan_rows = SUBLANES * REC_PITCH
    return pl.pallas_call(
        _rglru_kernel,
        out_shape=jax.ShapeDtypeStruct((b, s, width), BF16),
        grid=(b, s // REC_TILE),
        in_specs=[
            pl.BlockSpec((1, REC_TILE, d), lambda bi, t: (bi, t, 0)),
            pl.BlockSpec((1, d), const2),
            _layer_block(w_in_stack, j),
            pl.BlockSpec(conv_w.shape, const2),
            pl.BlockSpec((1, width), const2),
            _layer_block(w_a_stack, j),
            pl.BlockSpec((1, width), const2),
            _layer_block(w_i_stack, j),
            pl.BlockSpec((1, width), const2),
            pl.BlockSpec((1, width), const2),
        ],
        out_specs=pl.BlockSpec((1, REC_TILE, width), lambda bi, t: (bi, t, 0)),
        scratch_shapes=[
            pltpu.VMEM((REC_TILE + SUBLANES, width), F32),
            pltpu.VMEM((width // LANES, scan_rows, LANES), F32),
            pltpu.VMEM((width // LANES, scan_rows, LANES), F32),
            pltpu.VMEM((width // LANES, scan_rows, LANES), F32),
            pltpu.VMEM((width // LANES, scan_rows, LANES), F32),
            pltpu.VMEM((1, width), F32),
        ],
        compiler_params=_params(("arbitrary", "arbitrary")),
        name="rglru_mixer",
    )(x, g, w_in_stack, conv_w, conv_b, w_a_stack, b_a, w_i_stack, b_i, lam)


def kernel(x, attn_w_in, attn_rel_bias, attn_w_out, rg_w_in, rg_conv_w, rg_conv_b, rg_w_a, rg_b_a,
           rg_w_i, rg_b_i, rg_lambda, rg_w_out, norm_mix_pre, norm_mix_post, norm_ffn_pre,
           norm_ffn_post, ffn_w_gate, ffn_w_up, ffn_w_down):
    b, s, d = x.shape
    depth = norm_mix_pre.shape[0]
    xf = x.reshape(b * s, d)
    row = lambda v: v.reshape(1, -1)
    attn_w_in, attn_w_out, rg_w_in, rg_w_a, rg_w_i, rg_w_out, ffn_w_gate, ffn_w_up, ffn_w_down = (
        w.astype(BF16) for w in (attn_w_in, attn_w_out, rg_w_in, rg_w_a, rg_w_i, rg_w_out,
                                 ffn_w_gate, ffn_w_up, ffn_w_down))
    for layer in range(depth):
        j = layer // 2
        g_pre = row(norm_mix_pre[layer])
        if layer % 2 == 0:
            proj = norm_linear(xf, g_pre, attn_w_in, j, BF16).reshape(b, s, -1)
            out_a = band_attention(proj, band_bias(attn_rel_bias[j]), 0)
            out_b = stick_breaking_attention(proj, 3 * A_W)
            ins = [out_a.reshape(b * s, A_W), out_b.reshape(b * s, B_W)]
            w_out = attn_w_out
        else:
            width = rg_w_out.shape[1]
            y = rglru_mixer(
                xf.reshape(b, s, d), g_pre, rg_w_in,
                rg_conv_w[j].reshape(-1, width), row(rg_conv_b[j]),
                rg_w_a, row(rg_b_a[j]), rg_w_i, row(rg_b_i[j]), row(rg_lambda[j]), j)
            ins = [y.reshape(b * s, width)]
            w_out = rg_w_out
        xf = mixer_out_ffn(
            xf, row(norm_mix_post[layer]), ins, w_out, j,
            row(norm_ffn_pre[layer]), row(norm_ffn_post[layer]),
            ffn_w_gate, ffn_w_up, ffn_w_down, layer)
    return xf.reshape(b, s, d)
```

```python
import functools

import jax
import jax.numpy as jnp
from jax import lax
from jax.experimental import pallas as pl
from jax.experimental.pallas import tpu as pltpu

F32 = jnp.float32
BF16 = jnp.bfloat16

HEAD_DIM = 64
CHUNK = 64
N_LEFT_CHUNKS = 8
REL_CLIP = 256
A_W = 512
B_W = 512
LRU_BLOCKS = 4
LRU_BLOCK_W = 256
LRU_C = 8.0
RMS_EPS = 1e-6

LANES = 128
SUBLANES = 8
VMEM_LIMIT = 56 * 1024 * 1024

ROW_TILE = 512
SB_TILE = 256
SB_SUBTILES = 2
SB_PAIRS = 4
SB_EXIT = 128.0
A_TILE = 2 * CHUNK
A_SUBTILES = 64
A_WIN = (N_LEFT_CHUNKS + 2) * CHUNK
A_PAD = N_LEFT_CHUNKS * CHUNK
REC_TILE = 512
REC_PITCH = REC_TILE // SUBLANES + SUBLANES
NEG_BIG = -1e30
LOG2E = 1.4426950408889634

_NT = (((1,), (1,)), ((), ()))


def _params(sem):
    return pltpu.CompilerParams(dimension_semantics=sem, vmem_limit_bytes=VMEM_LIMIT)


def _rms(x, g):
    ms = jnp.mean(x * x, axis=-1, keepdims=True)
    return x * lax.rsqrt(ms + RMS_EPS) * g


def _sigmoid(x):
    return 1.0 / (1.0 + jnp.exp(-x))


def _resident(shape):
    return pl.BlockSpec(shape, lambda *_: (0,) * len(shape), pipeline_mode=pl.Buffered(1))


def _layer_block(stack, layer, rows=None, row_block=0):
    shape = stack.shape[1:] if rows is None else (rows,) + stack.shape[2:]
    index = (layer, row_block) + (0,) * (len(shape) - 1)
    return pl.BlockSpec((None,) + shape, lambda *_: index, pipeline_mode=pl.Buffered(1))


def _norm_linear_kernel(x_ref, g_ref, w_ref, o_ref):
    h = _rms(x_ref[...], g_ref[...]).astype(BF16)
    o_ref[...] = jnp.dot(h, w_ref[...], preferred_element_type=F32).astype(o_ref.dtype)


def norm_linear(x, g, w_stack, layer, out_dtype):
    m, d = x.shape
    n = w_stack.shape[2]
    return pl.pallas_call(
        _norm_linear_kernel,
        out_shape=jax.ShapeDtypeStruct((m, n), out_dtype),
        grid=(m // ROW_TILE,),
        in_specs=[
            pl.BlockSpec((ROW_TILE, d), lambda i: (i, 0)),
            pl.BlockSpec((1, d), lambda i: (0, 0)),
            _layer_block(w_stack, layer),
        ],
        out_specs=pl.BlockSpec((ROW_TILE, n), lambda i: (i, 0)),
        compiler_params=_params(("arbitrary",)),
        name="norm_linear",
    )(x, g, w_stack)


def _out_ffn_kernel(n_in, *refs):
    x_ref, gmix_ref, gpre_ref, gpost_ref, wg_ref, wu_ref, wd_ref = refs[:7]
    ins = refs[7:7 + n_in]
    ws = refs[7 + n_in:7 + 2 * n_in]
    o_ref = refs[7 + 2 * n_in]
    m = jnp.dot(ins[0][...], ws[0][...], preferred_element_type=F32)
    for a, w in zip(ins[1:], ws[1:]):
        m = m + jnp.dot(a[...], w[...], preferred_element_type=F32)
    x = x_ref[...] + _rms(m, gmix_ref[...])
    h = _rms(x, gpre_ref[...]).astype(BF16)
    gate = jnp.dot(h, wg_ref[...], preferred_element_type=F32)
    up = jnp.dot(h, wu_ref[...], preferred_element_type=F32)
    act = (gate * _sigmoid(gate) * up).astype(BF16)
    f = jnp.dot(act, wd_ref[...], preferred_element_type=F32)
    o_ref[...] = x + _rms(f, gpost_ref[...])


def mixer_out_ffn(x, gmix, ins, w_out_stack, j, gpre, gpost, wg_stack, wu_stack, wd_stack, layer):
    m, d = x.shape
    n_in = len(ins)
    in_specs = [pl.BlockSpec((ROW_TILE, d), lambda i: (i, 0))]
    in_specs += [_resident(a.shape) for a in (gmix, gpre, gpost)]
    in_specs += [_layer_block(w, layer) for w in (wg_stack, wu_stack, wd_stack)]
    in_specs += [pl.BlockSpec((ROW_TILE, a.shape[1]), lambda i: (i, 0)) for a in ins]
    in_specs += [_layer_block(w_out_stack, j, a.shape[1], k) for k, a in enumerate(ins)]
    return pl.pallas_call(
        functools.partial(_out_ffn_kernel, n_in),
        out_shape=jax.ShapeDtypeStruct((m, d), F32),
        grid=(m // ROW_TILE,),
        in_specs=in_specs,
        out_specs=pl.BlockSpec((ROW_TILE, d), lambda i: (i, 0)),
        compiler_params=_params(("arbitrary",)),
        name="mixer_out_ffn",
    )(x, gmix, gpre, gpost, wg_stack, wu_stack, wd_stack, *ins, *([w_out_stack] * n_in))


def _band_attn_kernel(q_ref, k_ref, v_ref, bias_ref, o_ref, kp_ref, vp_ref, s_ref, p_ref, rinv_ref):
    t = pl.program_id(2)

    @pl.when(t == 0)
    def _():
        kp_ref[0:A_PAD, :] = jnp.zeros((A_PAD, LANES), BF16)
        vp_ref[0:A_PAD, :] = jnp.zeros((A_PAD, LANES), BF16)
        kp_ref[A_PAD:, :] = k_ref[0]
        vp_ref[A_PAD:, :] = v_ref[0]

    lane = lax.broadcasted_iota(jnp.int32, (1, LANES), 1)
    first = lane < HEAD_DIM
    col = lax.broadcasted_iota(jnp.int32, (1, A_WIN), 1)
    bias = jnp.concatenate([bias_ref[0], bias_ref[1]], axis=0)
    zero = jnp.zeros((A_TILE, LANES), BF16)

    def window(u):
        tile = t * A_SUBTILES + u
        return tile, pl.multiple_of(tile * A_TILE, A_TILE)

    def scores(u, slot, head_of_sequence):
        tile, start = window(u)
        kw = kp_ref[pl.ds(start, A_WIN), :]
        q2 = q_ref[0, pl.ds(pl.multiple_of(u * A_TILE, A_TILE), A_TILE), :] * (HEAD_DIM ** -0.5)
        qs = jnp.concatenate([jnp.where(first, q2, zero), jnp.where(first, zero, q2)], axis=0)
        s = lax.dot_general(qs, kw, _NT, preferred_element_type=F32) + bias
        if head_of_sequence:
            s = jnp.where(col >= (A_PAD - tile * A_TILE), s, NEG_BIG)
        s_ref[slot] = s

    def softmax(slot):
        s = s_ref[slot]
        p = jnp.exp(s - jnp.max(s, axis=-1, keepdims=True))
        p_ref[slot] = p.astype(BF16)
        rinv_ref[slot] = jnp.broadcast_to(1.0 / jnp.sum(p, axis=-1, keepdims=True), (2 * A_TILE, LANES))

    def values(u, slot):
        _, start = window(u)
        vw = vp_ref[pl.ds(start, A_WIN), :]
        o = jnp.dot(p_ref[slot], vw, preferred_element_type=F32) * rinv_ref[slot]
        out = jnp.where(first, o[:A_TILE], o[A_TILE:])
        o_ref[0, pl.ds(pl.multiple_of(u * A_TILE, A_TILE), A_TILE), :] = out.astype(o_ref.dtype)

    scores(0, 0, True)
    scores(1, 1, True)
    softmax(0)

    def pair(k, head_of_sequence):
        values(2 * k - 2, 0)
        softmax(1)
        scores(2 * k, 0, head_of_sequence)
        values(2 * k - 1, 1)
        softmax(0)
        scores(2 * k + 1, 1, head_of_sequence)

    n_head = A_PAD // A_TILE // 2
    for k in range(1, n_head):
        pair(k, True)

    def body(k, _):
        pair(k, False)
        return 0

    lax.fori_loop(n_head, A_SUBTILES // 2, body, 0)
    values(A_SUBTILES - 2, 0)
    softmax(1)
    values(A_SUBTILES - 1, 1)


def band_attention(proj, bias, col0):
    b, s, _ = proj.shape
    nq = col0 // LANES
    nk = (col0 + A_W) // LANES
    nv = (col0 + 2 * A_W) // LANES
    tq = A_SUBTILES * A_TILE
    return pl.pallas_call(
        _band_attn_kernel,
        out_shape=jax.ShapeDtypeStruct((b, s, A_W), BF16),
        grid=(b, A_W // LANES, s // tq),
        in_specs=[
            pl.BlockSpec((1, tq, LANES), lambda bi, p, t: (bi, t, nq + p)),
            pl.BlockSpec((1, s, LANES), lambda bi, p, t: (bi, 0, nk + p)),
            pl.BlockSpec((1, s, LANES), lambda bi, p, t: (bi, 0, nv + p)),
            pl.BlockSpec((2, A_TILE, A_WIN), lambda bi, p, t: (p, 0, 0)),
        ],
        out_specs=pl.BlockSpec((1, tq, LANES), lambda bi, p, t: (bi, t, p)),
        scratch_shapes=[
            pltpu.VMEM((s + A_PAD, LANES), BF16),
            pltpu.VMEM((s + A_PAD, LANES), BF16),
            pltpu.VMEM((2, 2 * A_TILE, A_WIN), F32),
            pltpu.VMEM((2, 2 * A_TILE, A_WIN), BF16),
            pltpu.VMEM((2, 2 * A_TILE, LANES), F32),
        ],
        compiler_params=_params(("arbitrary", "arbitrary", "arbitrary")),
        name="band_attention",
    )(proj, proj, proj, bias)


def band_bias(rel_bias):
    h = rel_bias.shape[0]
    n_f = A_WIN + A_TILE - 1
    n_const = A_PAD + A_TILE - 1 - REL_CLIP
    far = jnp.broadcast_to(rel_bias[:, 2 * REL_CLIP:], (h, n_const))
    near = rel_bias[:, 2 * REL_CLIP + 1 - (n_f - n_const):][:, ::-1]
    f = jnp.concatenate([far, near], axis=1)
    period = n_f + 1
    g = jnp.concatenate([f[:, A_TILE - 1:], jnp.zeros((h, 1), F32), f[:, :A_TILE - 1]], axis=1)
    flat = jnp.tile(g, (1, A_TILE))[:, :A_TILE * (period - 1)]
    toep = flat.reshape(h, A_TILE, period - 1)[:, :, :A_WIN]
    r = jnp.arange(A_TILE)[:, None]
    c = jnp.arange(A_WIN)[None, :]
    dchunk = c // CHUNK - r // CHUNK
    ok = (dchunk >= 0) & (dchunk <= N_LEFT_CHUNKS)
    return jnp.where(ok[None], toep, NEG_BIG).astype(F32)


def _stick_kernel(q_ref, k_ref, v_ref, o_ref, acc_ref, carry_ref):
    qi = pl.program_id(2)
    t = SB_TILE
    ns = SB_SUBTILES
    lane = lax.broadcasted_iota(jnp.int32, (1, LANES), 1)
    first = lane < HEAD_DIM
    row = lax.broadcasted_iota(jnp.int32, (t, t), 0)
    col = lax.broadcasted_iota(jnp.int32, (t, t), 1)
    causal = col < row
    true = jnp.ones((t, t), jnp.bool_)
    suffix = jnp.where(row >= col, 1.0, 0.0).astype(BF16)
    zero = jnp.zeros((t, LANES), BF16)

    acc_ref[...] = jnp.zeros_like(acc_ref)
    carry_ref[...] = jnp.zeros_like(carry_ref)

    def lanes_of(pair):
        return slice(pair * LANES, (pair + 1) * LANES)

    def split_heads(x):
        return jnp.concatenate([jnp.where(first, x, zero), jnp.where(first, zero, x)], axis=0)

    def scores(pair, j, d, mask):
        start = pl.multiple_of(j * t, t)
        kcat = split_heads(k_ref[0, pl.ds(start, t), lanes_of(pair)])
        q2 = q_ref[0, d * t:, lanes_of(pair)] * (HEAD_DIM ** -0.5)
        z = lax.dot_general(q2, kcat, _NT, preferred_element_type=F32)
        sp = jnp.maximum(z, 0.0) + jnp.log(1.0 + jnp.exp2(jnp.abs(z) * (-LOG2E)))
        if mask is not None:
            sp = jnp.where(mask, sp, 0.0)
        spb = sp.astype(BF16)
        return z - (sp - spb.astype(F32)), spb

    def weights(pair, j, d, mask, tz, spb):
        r0 = d * t
        n = (ns - d) * t
        start = pl.multiple_of(j * t, t)
        vcat = split_heads(v_ref[0, pl.ds(start, t), lanes_of(pair)])
        c = [jnp.dot(spb[:, h * t:(h + 1) * t], suffix, preferred_element_type=F32) for h in range(2)]
        carry = [carry_ref[pair, h, r0:, :] for h in range(2)]
        reps = t // LANES
        shift = jnp.concatenate([c[0]] + [c[1]], axis=1) + jnp.concatenate(
            [carry[0]] * reps + [carry[1]] * reps, axis=1)
        w = jnp.exp(tz - shift)
        if mask is not None:
            w = jnp.where(mask, w, 0.0)
        for h in range(2):
            carry_ref[pair, h, r0:, :] = carry[h] + jnp.broadcast_to(c[h][:, 0:1], (n, LANES))
        acc_ref[pair, r0:, :] += jnp.dot(w.astype(BF16), vcat, preferred_element_type=F32)

    for pair in range(SB_PAIRS):
        for d in range(ns - 1, -1, -1):
            mask = jnp.concatenate([causal] + [true] * (ns - d - 1), axis=0)
            mask = jnp.concatenate([mask, mask], axis=1)
            tz, spb = scores(pair, ns * qi + d, d, mask)
            weights(pair, ns * qi + d, d, mask, tz, spb)

    def saturated(pair):
        low = jnp.min(jnp.minimum(carry_ref[pair, 0], carry_ref[pair, 1]), axis=0, keepdims=True)
        return (jnp.min(low) >= SB_EXIT).astype(jnp.int32)

    @pl.when(qi > 0)
    def _():
        for pair in range(SB_PAIRS):
            tz, spb = scores(pair, ns * qi - 1, 0, None)
            weights(pair, ns * qi - 1, 0, None, tz, spb)
        done_after_first = [saturated(pair) for pair in range(SB_PAIRS)]
        for pair in range(SB_PAIRS):
            def cond(state):
                i, done = state
                return jnp.logical_and(i < ns * qi, done == 0)

            def body(state, pair=pair):
                i, _ = state
                tz, spb = scores(pair, ns * qi - 1 - i, 0, None)
                weights(pair, ns * qi - 1 - i, 0, None, tz, spb)
                return i + 1, saturated(pair)

            lax.while_loop(cond, body, (jnp.int32(1), done_after_first[pair]))

    for pair in range(SB_PAIRS):
        o_ref[0, :, lanes_of(pair)] = acc_ref[pair].astype(o_ref.dtype)


def stick_breaking_attention(proj, col0):
    b, s, _ = proj.shape
    width = SB_PAIRS * LANES
    nq = col0 // width
    nk = (col0 + B_W) // width
    nv = (col0 + 2 * B_W) // width
    tq = SB_SUBTILES * SB_TILE
    return pl.pallas_call(
        _stick_kernel,
        out_shape=jax.ShapeDtypeStruct((b, s, B_W), BF16),
        grid=(b, B_W // width, s // tq),
        in_specs=[
            pl.BlockSpec((1, tq, width), lambda bi, p, t: (bi, t, nq + p)),
            pl.BlockSpec((1, s, width), lambda bi, p, t: (bi, 0, nk + p)),
            pl.BlockSpec((1, s, width), lambda bi, p, t: (bi, 0, nv + p)),
        ],
        out_specs=pl.BlockSpec((1, tq, width), lambda bi, p, t: (bi, t, p)),
        scratch_shapes=[
            pltpu.VMEM((SB_PAIRS, tq, LANES), F32),
            pltpu.VMEM((SB_PAIRS, 2, tq, LANES), F32),
        ],
        compiler_params=_params(("arbitrary", "arbitrary", "arbitrary")),
        name="stick_breaking",
    )(proj, proj, proj)


def _rglru_kernel(x_ref, g_ref, win_ref, cw_ref, cb_ref, wa_ref, ba_ref, wi_ref, bi_ref, lam_ref,
                  y_ref, xpad_ref, a_ref, u_ref, p_ref, hl_ref, h_ref):
    t = pl.program_id(1)
    n = REC_TILE
    steps = n // SUBLANES
    width = y_ref.shape[2]
    n_slabs = width // LANES

    @pl.when(t == 0)
    def _():
        xpad_ref[0:SUBLANES, :] = jnp.zeros((SUBLANES, width), F32)
        h_ref[...] = jnp.zeros_like(h_ref)

    hn = _rms(x_ref[0], g_ref[...]).astype(BF16)
    gate = jnp.dot(hn, win_ref[:, :width], preferred_element_type=F32)
    c = 0.7978845608028654
    gate = 0.5 * gate * (1.0 + jnp.tanh(c * (gate + 0.044715 * (gate * gate * gate))))
    hist = SUBLANES
    xpad_ref[hist:hist + n, :] = jnp.dot(hn, win_ref[:, width:], preferred_element_type=F32)
    cw = cw_ref[...]
    xc = (cb_ref[...]
          + cw[3:4] * xpad_ref[hist:hist + n, :]
          + cw[2:3] * xpad_ref[hist - 1:hist - 1 + n, :]
          + cw[1:2] * xpad_ref[hist - 2:hist - 2 + n, :]
          + cw[0:1] * xpad_ref[hist - 3:hist - 3 + n, :])
    xpad_ref[0:hist, :] = xpad_ref[n:n + hist, :]
    xcb = xc.astype(BF16)
    lam = lam_ref[...]
    sp_lam = jnp.maximum(-lam, 0.0) + jnp.log1p(jnp.exp(-jnp.abs(lam)))
    for blk in range(LRU_BLOCKS):
        sl = slice(blk * LRU_BLOCK_W, (blk + 1) * LRU_BLOCK_W)
        xb = xcb[:, sl]
        pre_r = jnp.dot(xb, wa_ref[blk], preferred_element_type=F32) + ba_ref[:, sl]
        pre_i = jnp.dot(xb, wi_ref[blk], preferred_element_type=F32) + bi_ref[:, sl]
        r = 0.5 * jnp.tanh(0.5 * pre_r) + 0.5
        i = 0.5 * jnp.tanh(0.5 * pre_i) + 0.5
        log_a = (-LRU_C * r) * sp_lam[:, sl]
        a = jnp.exp(log_a)
        m2 = -jnp.tanh(log_a) * (a * a + 1.0)
        mult = m2 * lax.rsqrt(jnp.maximum(m2, 1e-30))
        u = mult * (i * xc[:, sl])
        for k in range(LRU_BLOCK_W // LANES):
            slab = blk * (LRU_BLOCK_W // LANES) + k
            for ch in range(SUBLANES):
                rows = slice(ch * steps, (ch + 1) * steps)
                dst = slice(ch * REC_PITCH, ch * REC_PITCH + steps)
                a_ref[slab, dst, :] = a[rows, k * LANES:(k + 1) * LANES]
                u_ref[slab, dst, :] = u[rows, k * LANES:(k + 1) * LANES]

    def step(l, state):
        rows = pl.ds(l, SUBLANES, stride=REC_PITCH)
        new = []
        for slab in range(n_slabs):
            h, p = state[slab]
            av = a_ref[slab, rows, :]
            h = av * h + u_ref[slab, rows, :]
            p = p * av
            hl_ref[slab, rows, :] = h
            p_ref[slab, rows, :] = p
            new.append((h, p))
        return tuple(new)

    zeros = jnp.zeros((SUBLANES, LANES), F32)
    ones = jnp.ones((SUBLANES, LANES), F32)
    ends = lax.fori_loop(0, steps, step, tuple((zeros, ones) for _ in range(n_slabs)), unroll=4)

    rowid = lax.broadcasted_iota(jnp.int32, (SUBLANES, LANES), 0)
    for slab in range(n_slabs):
        lanes = slice(slab * LANES, (slab + 1) * LANES)
        b, a = ends[slab]
        for d in (1, 2, 4):
            keep = rowid >= d
            a_sh = jnp.where(keep, pltpu.roll(a, d, 0), 1.0)
            b_sh = jnp.where(keep, pltpu.roll(b, d, 0), 0.0)
            b = a * b_sh + b
            a = a * a_sh
        h_in = h_ref[:, lanes]
        after = b + a * h_in
        h_ref[:, lanes] = after[SUBLANES - 1:, :]
        start = jnp.where(rowid >= 1, pltpu.roll(after, 1, 0), h_in)
        for ch in range(SUBLANES):
            src = slice(ch * REC_PITCH, ch * REC_PITCH + steps)
            rows = slice(ch * steps, (ch + 1) * steps)
            hs = hl_ref[slab, src, :] + p_ref[slab, src, :] * start[ch:ch + 1, :]
            y_ref[0, rows, lanes] = (hs * gate[rows, lanes]).astype(y_ref.dtype)


def rglru_mixer(x, g, w_in_stack, conv_w, conv_b, w_a_stack, b_a, w_i_stack, b_i, lam, j):
    b, s, d = x.shape
    width = w_in_stack.shape[2] // 2
    const2 = lambda bi, t: (0, 0)
    scan_rows = SUBLANES * REC_PITCH
    return pl.pallas_call(
        _rglru_kernel,
        out_shape=jax.ShapeDtypeStruct((b, s, width), BF16),
        grid=(b, s // REC_TILE),
        in_specs=[
            pl.BlockSpec((1, REC_TILE, d), lambda bi, t: (bi, t, 0)),
            pl.BlockSpec((1, d), const2),
            _layer_block(w_in_stack, j),
            pl.BlockSpec(conv_w.shape, const2),
            pl.BlockSpec((1, width), const2),
            _layer_block(w_a_stack, j),
            pl.BlockSpec((1, width), const2),
            _layer_block(w_i_stack, j),
            pl.BlockSpec((1, width), const2),
            pl.BlockSpec((1, width), const2),
        ],
        out_specs=pl.BlockSpec((1, REC_TILE, width), lambda bi, t: (bi, t, 0)),
        scratch_shapes=[
            pltpu.VMEM((REC_TILE + SUBLANES, width), F32),
            pltpu.VMEM((width // LANES, scan_rows, LANES), F32),
            pltpu.VMEM((width // LANES, scan_rows, LANES), F32),
            pltpu.VMEM((width // LANES, scan_rows, LANES), F32),
            pltpu.VMEM((width // LANES, scan_rows, LANES), F32),
            pltpu.VMEM((1, width), F32),
        ],
        compiler_params=_params(("arbitrary", "arbitrary")),
        name="rglru_mixer",
    )(x, g, w_in_stack, conv_w, conv_b, w_a_stack, b_a, w_i_stack, b_i, lam)


def kernel(x, attn_w_in, attn_rel_bias, attn_w_out, rg_w_in, rg_conv_w, rg_conv_b, rg_w_a, rg_b_a,
           rg_w_i, rg_b_i, rg_lambda, rg_w_out, norm_mix_pre, norm_mix_post, norm_ffn_pre,
           norm_ffn_post, ffn_w_gate, ffn_w_up, ffn_w_down):
    b, s, d = x.shape
    depth = norm_mix_pre.shape[0]
    xf = x.reshape(b * s, d)
    row = lambda v: v.reshape(1, -1)
    attn_w_in, attn_w_out, rg_w_in, rg_w_a, rg_w_i, rg_w_out, ffn_w_gate, ffn_w_up, ffn_w_down = (
        w.astype(BF16) for w in (attn_w_in, attn_w_out, rg_w_in, rg_w_a, rg_w_i, rg_w_out,
                                 ffn_w_gate, ffn_w_up, ffn_w_down))
    for layer in range(depth):
        j = layer // 2
        g_pre = row(norm_mix_pre[layer])
        if layer % 2 == 0:
            proj = norm_linear(xf, g_pre, attn_w_in, j, BF16).reshape(b, s, -1)
            out_a = band_attention(proj, band_bias(attn_rel_bias[j]), 0)
            out_b = stick_breaking_attention(proj, 3 * A_W)
            ins = [out_a.reshape(b * s, A_W), out_b.reshape(b * s, B_W)]
            w_out = attn_w_out
        else:
            width = rg_w_out.shape[1]
            y = rglru_mixer(
                xf.reshape(b, s, d), g_pre, rg_w_in,
                rg_conv_w[j].reshape(-1, width), row(rg_conv_b[j]),
                rg_w_a, row(rg_b_a[j]), rg_w_i, row(rg_b_i[j]), row(rg_lambda[j]), j)
            ins = [y.reshape(b * s, width)]
            w_out = rg_w_out
        xf = mixer_out_ffn(
            xf, row(norm_mix_post[layer]), ins, w_out, j,
            row(norm_ffn_pre[layer]), row(norm_ffn_post[layer]),
            ffn_w_gate, ffn_w_up, ffn_w_down, layer)
    return xf.reshape(b, s, d)
```

```python
import functools

import jax
import jax.numpy as jnp
from jax import lax
from jax.experimental import pallas as pl
from jax.experimental.pallas import tpu as pltpu

F32 = jnp.float32
BF16 = jnp.bfloat16

HEAD_DIM = 64
CHUNK = 64
N_LEFT_CHUNKS = 8
REL_CLIP = 256
A_W = 512
B_W = 512
LRU_BLOCKS = 4
LRU_BLOCK_W = 256
LRU_C = 8.0
RMS_EPS = 1e-6

LANES = 128
SUBLANES = 8
MXU_TILE = 256
VMEM_LIMIT = 56 * 1024 * 1024

ROW_TILE = 512
SB_TILE = 256
SB_SUBTILES = 2
SB_PAIRS = 4
SB_EXIT = 128.0
A_TILE = 2 * CHUNK
A_SUBTILES = 64
A_WIN = (N_LEFT_CHUNKS + 2) * CHUNK
A_PAD = N_LEFT_CHUNKS * CHUNK
REC_TILE = 512
REC_PITCH = REC_TILE // SUBLANES + SUBLANES
NEG_BIG = -1e30
LOG2E = 1.4426950408889634

_NT = (((1,), (1,)), ((), ()))


def _params(sem):
    return pltpu.CompilerParams(dimension_semantics=sem, vmem_limit_bytes=VMEM_LIMIT)


def _rms(x, g):
    ms = jnp.mean(x * x, axis=-1, keepdims=True)
    return x * lax.rsqrt(ms + RMS_EPS) * g


def _sigmoid(x):
    return 1.0 / (1.0 + jnp.exp(-x))


def _resident(shape):
    return pl.BlockSpec(shape, lambda *_: (0,) * len(shape), pipeline_mode=pl.Buffered(1))


def _layer_block(stack, layer, rows=None, row_block=0):
    shape = stack.shape[1:] if rows is None else (rows,) + stack.shape[2:]
    index = (layer, row_block) + (0,) * (len(shape) - 1)
    return pl.BlockSpec((None,) + shape, lambda *_: index, pipeline_mode=pl.Buffered(1))


def _norm_linear_kernel(x_ref, g_ref, w_ref, o_ref):
    h = _rms(x_ref[...], g_ref[...]).astype(BF16)
    o_ref[...] = jnp.dot(h, w_ref[...], preferred_element_type=F32).astype(o_ref.dtype)


def norm_linear(x, g, w_stack, layer, out_dtype):
    m, d = x.shape
    n = w_stack.shape[2]
    return pl.pallas_call(
        _norm_linear_kernel,
        out_shape=jax.ShapeDtypeStruct((m, n), out_dtype),
        grid=(m // ROW_TILE,),
        in_specs=[
            pl.BlockSpec((ROW_TILE, d), lambda i: (i, 0)),
            pl.BlockSpec((1, d), lambda i: (0, 0)),
            _layer_block(w_stack, layer),
        ],
        out_specs=pl.BlockSpec((ROW_TILE, n), lambda i: (i, 0)),
        compiler_params=_params(("arbitrary",)),
        name="norm_linear",
    )(x, g, w_stack)


def _out_ffn_kernel(n_in, *refs):
    x_ref, gmix_ref, gpre_ref, gpost_ref, wg_ref, wu_ref, wd_ref = refs[:7]
    ins = refs[7:7 + n_in]
    ws = refs[7 + n_in:7 + 2 * n_in]
    o_ref = refs[7 + 2 * n_in]
    m = jnp.dot(ins[0][...], ws[0][...], preferred_element_type=F32)
    for a, w in zip(ins[1:], ws[1:]):
        m = m + jnp.dot(a[...], w[...], preferred_element_type=F32)
    x = x_ref[...] + _rms(m, gmix_ref[...])
    h = _rms(x, gpre_ref[...]).astype(BF16)
    d_ff = wg_ref.shape[1]
    split = (d_ff // MXU_TILE + 1) // 2 * MXU_TILE
    f = None
    for cols in (slice(0, split), slice(split, d_ff)):
        gate = jnp.dot(h, wg_ref[:, cols], preferred_element_type=F32)
        up = jnp.dot(h, wu_ref[:, cols], preferred_element_type=F32)
        act = (gate * _sigmoid(gate) * up).astype(BF16)
        part = jnp.dot(act, wd_ref[cols, :], preferred_element_type=F32)
        f = part if f is None else f + part
    o_ref[...] = x + _rms(f, gpost_ref[...])


def mixer_out_ffn(x, gmix, ins, w_out_stack, j, gpre, gpost, wg_stack, wu_stack, wd_stack, layer):
    m, d = x.shape
    n_in = len(ins)
    in_specs = [pl.BlockSpec((ROW_TILE, d), lambda i: (i, 0))]
    in_specs += [_resident(a.shape) for a in (gmix, gpre, gpost)]
    in_specs += [_layer_block(w, layer) for w in (wg_stack, wu_stack, wd_stack)]
    in_specs += [pl.BlockSpec((ROW_TILE, a.shape[1]), lambda i: (i, 0)) for a in ins]
    in_specs += [_layer_block(w_out_stack, j, a.shape[1], k) for k, a in enumerate(ins)]
    return pl.pallas_call(
        functools.partial(_out_ffn_kernel, n_in),
        out_shape=jax.ShapeDtypeStruct((m, d), F32),
        grid=(m // ROW_TILE,),
        in_specs=in_specs,
        out_specs=pl.BlockSpec((ROW_TILE, d), lambda i: (i, 0)),
        compiler_params=_params(("arbitrary",)),
        name="mixer_out_ffn",
    )(x, gmix, gpre, gpost, wg_stack, wu_stack, wd_stack, *ins, *([w_out_stack] * n_in))


def _band_attn_kernel(q_ref, k_ref, v_ref, bias_ref, o_ref, kp_ref, vp_ref, s_ref, p_ref, rinv_ref):
    t = pl.program_id(2)

    @pl.when(t == 0)
    def _():
        kp_ref[0:A_PAD, :] = jnp.zeros((A_PAD, LANES), BF16)
        vp_ref[0:A_PAD, :] = jnp.zeros((A_PAD, LANES), BF16)
        kp_ref[A_PAD:, :] = k_ref[0]
        vp_ref[A_PAD:, :] = v_ref[0]

    lane = lax.broadcasted_iota(jnp.int32, (1, LANES), 1)
    first = lane < HEAD_DIM
    col = lax.broadcasted_iota(jnp.int32, (1, A_WIN), 1)
    bias = jnp.concatenate([bias_ref[0], bias_ref[1]], axis=0)
    zero = jnp.zeros((A_TILE, LANES), BF16)

    def window(u):
        tile = t * A_SUBTILES + u
        return tile, pl.multiple_of(tile * A_TILE, A_TILE)

    def scores(u, slot, head_of_sequence):
        tile, start = window(u)
        kw = kp_ref[pl.ds(start, A_WIN), :]
        q2 = q_ref[0, pl.ds(pl.multiple_of(u * A_TILE, A_TILE), A_TILE), :] * (HEAD_DIM ** -0.5)
        qs = jnp.concatenate([jnp.where(first, q2, zero), jnp.where(first, zero, q2)], axis=0)
        s = lax.dot_general(qs, kw, _NT, preferred_element_type=F32) + bias
        if head_of_sequence:
            s = jnp.where(col >= (A_PAD - tile * A_TILE), s, NEG_BIG)
        s_ref[slot] = s

    def softmax(slot):
        s = s_ref[slot]
        p = jnp.exp(s - jnp.max(s, axis=-1, keepdims=True))
        p_ref[slot] = p.astype(BF16)
        rinv_ref[slot] = jnp.broadcast_to(1.0 / jnp.sum(p, axis=-1, keepdims=True), (2 * A_TILE, LANES))

    def values(u, slot):
        _, start = window(u)
        vw = vp_ref[pl.ds(start, A_WIN), :]
        o = jnp.dot(p_ref[slot], vw, preferred_element_type=F32) * rinv_ref[slot]
        out = jnp.where(first, o[:A_TILE], o[A_TILE:])
        o_ref[0, pl.ds(pl.multiple_of(u * A_TILE, A_TILE), A_TILE), :] = out.astype(o_ref.dtype)

    scores(0, 0, True)
    scores(1, 1, True)
    softmax(0)

    def pair(k, head_of_sequence):
        values(2 * k - 2, 0)
        softmax(1)
        scores(2 * k, 0, head_of_sequence)
        values(2 * k - 1, 1)
        softmax(0)
        scores(2 * k + 1, 1, head_of_sequence)

    n_head = A_PAD // A_TILE // 2
    for k in range(1, n_head):
        pair(k, True)

    def body(k, _):
        pair(k, False)
        return 0

    lax.fori_loop(n_head, A_SUBTILES // 2, body, 0)
    values(A_SUBTILES - 2, 0)
    softmax(1)
    values(A_SUBTILES - 1, 1)


def band_attention(proj, bias, col0):
    b, s, _ = proj.shape
    nq = col0 // LANES
    nk = (col0 + A_W) // LANES
    nv = (col0 + 2 * A_W) // LANES
    tq = A_SUBTILES * A_TILE
    return pl.pallas_call(
        _band_attn_kernel,
        out_shape=jax.ShapeDtypeStruct((b, s, A_W), BF16),
        grid=(b, A_W // LANES, s // tq),
        in_specs=[
            pl.BlockSpec((1, tq, LANES), lambda bi, p, t: (bi, t, nq + p)),
            pl.BlockSpec((1, s, LANES), lambda bi, p, t: (bi, 0, nk + p)),
            pl.BlockSpec((1, s, LANES), lambda bi, p, t: (bi, 0, nv + p)),
            pl.BlockSpec((2, A_TILE, A_WIN), lambda bi, p, t: (p, 0, 0)),
        ],
        out_specs=pl.BlockSpec((1, tq, LANES), lambda bi, p, t: (bi, t, p)),
        scratch_shapes=[
            pltpu.VMEM((s + A_PAD, LANES), BF16),
            pltpu.VMEM((s + A_PAD, LANES), BF16),
            pltpu.VMEM((2, 2 * A_TILE, A_WIN), F32),
            pltpu.VMEM((2, 2 * A_TILE, A_WIN), BF16),
            pltpu.VMEM((2, 2 * A_TILE, LANES), F32),
        ],
        compiler_params=_params(("arbitrary", "arbitrary", "arbitrary")),
        name="band_attention",
    )(proj, proj, proj, bias)


def band_bias(rel_bias):
    h = rel_bias.shape[0]
    n_f = A_WIN + A_TILE - 1
    n_const = A_PAD + A_TILE - 1 - REL_CLIP
    far = jnp.broadcast_to(rel_bias[:, 2 * REL_CLIP:], (h, n_const))
    near = rel_bias[:, 2 * REL_CLIP + 1 - (n_f - n_const):][:, ::-1]
    f = jnp.concatenate([far, near], axis=1)
    period = n_f + 1
    g = jnp.concatenate([f[:, A_TILE - 1:], jnp.zeros((h, 1), F32), f[:, :A_TILE - 1]], axis=1)
    flat = jnp.tile(g, (1, A_TILE))[:, :A_TILE * (period - 1)]
    toep = flat.reshape(h, A_TILE, period - 1)[:, :, :A_WIN]
    r = jnp.arange(A_TILE)[:, None]
    c = jnp.arange(A_WIN)[None, :]
    dchunk = c // CHUNK - r // CHUNK
    ok = (dchunk >= 0) & (dchunk <= N_LEFT_CHUNKS)
    return jnp.where(ok[None], toep, NEG_BIG).astype(F32)


def _stick_kernel(q_ref, k_ref, v_ref, o_ref, acc_ref, carry_ref):
    qi = pl.program_id(2)
    t = SB_TILE
    ns = SB_SUBTILES
    lane = lax.broadcasted_iota(jnp.int32, (1, LANES), 1)
    first = lane < HEAD_DIM
    row = lax.broadcasted_iota(jnp.int32, (t, t), 0)
    col = lax.broadcasted_iota(jnp.int32, (t, t), 1)
    causal = col < row
    true = jnp.ones((t, t), jnp.bool_)
    suffix = jnp.where(row >= col, 1.0, 0.0).astype(BF16)
    zero = jnp.zeros((t, LANES), BF16)

    acc_ref[...] = jnp.zeros_like(acc_ref)
    carry_ref[...] = jnp.zeros_like(carry_ref)

    def lanes_of(pair):
        return slice(pair * LANES, (pair + 1) * LANES)

    def split_heads(x):
        return jnp.concatenate([jnp.where(first, x, zero), jnp.where(first, zero, x)], axis=0)

    def scores(pair, j, d, mask):
        start = pl.multiple_of(j * t, t)
        kcat = split_heads(k_ref[0, pl.ds(start, t), lanes_of(pair)])
        q2 = q_ref[0, d * t:, lanes_of(pair)] * (HEAD_DIM ** -0.5)
        z = lax.dot_general(q2, kcat, _NT, preferred_element_type=F32)
        sp = jnp.maximum(z, 0.0) + jnp.log(1.0 + jnp.exp2(jnp.abs(z) * (-LOG2E)))
        if mask is not None:
            sp = jnp.where(mask, sp, 0.0)
        spb = sp.astype(BF16)
        return z - (sp - spb.astype(F32)), spb

    def weights(pair, j, d, mask, tz, spb):
        r0 = d * t
        n = (ns - d) * t
        start = pl.multiple_of(j * t, t)
        vcat = split_heads(v_ref[0, pl.ds(start, t), lanes_of(pair)])
        c = [jnp.dot(spb[:, h * t:(h + 1) * t], suffix, preferred_element_type=F32) for h in range(2)]
        carry = [carry_ref[pair, h, r0:, :] for h in range(2)]
        reps = t // LANES
        shift = jnp.concatenate([c[0]] + [c[1]], axis=1) + jnp.concatenate(
            [carry[0]] * reps + [carry[1]] * reps, axis=1)
        w = jnp.exp(tz - shift)
        if mask is not None:
            w = jnp.where(mask, w, 0.0)
        for h in range(2):
            carry_ref[pair, h, r0:, :] = carry[h] + jnp.broadcast_to(c[h][:, 0:1], (n, LANES))
        acc_ref[pair, r0:, :] += jnp.dot(w.astype(BF16), vcat, preferred_element_type=F32)

    for pair in range(SB_PAIRS):
        for d in range(ns - 1, -1, -1):
            mask = jnp.concatenate([causal] + [true] * (ns - d - 1), axis=0)
            mask = jnp.concatenate([mask, mask], axis=1)
            tz, spb = scores(pair, ns * qi + d, d, mask)
            weights(pair, ns * qi + d, d, mask, tz, spb)

    def saturated(pair):
        low = jnp.min(jnp.minimum(carry_ref[pair, 0], carry_ref[pair, 1]), axis=0, keepdims=True)
        return (jnp.min(low) >= SB_EXIT).astype(jnp.int32)

    @pl.when(qi > 0)
    def _():
        for pair in range(SB_PAIRS):
            tz, spb = scores(pair, ns * qi - 1, 0, None)
            weights(pair, ns * qi - 1, 0, None, tz, spb)
        done_after_first = [saturated(pair) for pair in range(SB_PAIRS)]
        for pair in range(SB_PAIRS):
            def cond(state):
                i, done = state
                return jnp.logical_and(i < ns * qi, done == 0)

            def body(state, pair=pair):
                i, _ = state
                tz, spb = scores(pair, ns * qi - 1 - i, 0, None)
                weights(pair, ns * qi - 1 - i, 0, None, tz, spb)
                return i + 1, saturated(pair)

            lax.while_loop(cond, body, (jnp.int32(1), done_after_first[pair]))

    for pair in range(SB_PAIRS):
        o_ref[0, :, lanes_of(pair)] = acc_ref[pair].astype(o_ref.dtype)


def stick_breaking_attention(proj, col0):
    b, s, _ = proj.shape
    width = SB_PAIRS * LANES
    nq = col0 // width
    nk = (col0 + B_W) // width
    nv = (col0 + 2 * B_W) // width
    tq = SB_SUBTILES * SB_TILE
    return pl.pallas_call(
        _stick_kernel,
        out_shape=jax.ShapeDtypeStruct((b, s, B_W), BF16),
        grid=(b, B_W // width, s // tq),
        in_specs=[
            pl.BlockSpec((1, tq, width), lambda bi, p, t: (bi, t, nq + p)),
            pl.BlockSpec((1, s, width), lambda bi, p, t: (bi, 0, nk + p)),
            pl.BlockSpec((1, s, width), lambda bi, p, t: (bi, 0, nv + p)),
        ],
        out_specs=pl.BlockSpec((1, tq, width), lambda bi, p, t: (bi, t, p)),
        scratch_shapes=[
            pltpu.VMEM((SB_PAIRS, tq, LANES), F32),
            pltpu.VMEM((SB_PAIRS, 2, tq, LANES), F32),
        ],
        compiler_params=_params(("arbitrary", "arbitrary", "arbitrary")),
        name="stick_breaking",
    )(proj, proj, proj)


def _rglru_kernel(x_ref, g_ref, win_ref, cw_ref, cb_ref, wa_ref, ba_ref, wi_ref, bi_ref, lam_ref,
                  y_ref, xpad_ref, a_ref, u_ref, p_ref, hl_ref, h_ref):
    t = pl.program_id(1)
    n = REC_TILE
    steps = n // SUBLANES
    width = y_ref.shape[2]
    n_slabs = width // LANES

    @pl.when(t == 0)
    def _():
        xpad_ref[0:SUBLANES, :] = jnp.zeros((SUBLANES, width), F32)
        h_ref[...] = jnp.zeros_like(h_ref)

    hn = _rms(x_ref[0], g_ref[...]).astype(BF16)
    gate = jnp.dot(hn, win_ref[:, :width], preferred_element_type=F32)
    c = 0.7978845608028654
    gate = 0.5 * gate * (1.0 + jnp.tanh(c * (gate + 0.044715 * (gate * gate * gate))))
    hist = SUBLANES
    xpad_ref[hist:hist + n, :] = jnp.dot(hn, win_ref[:, width:], preferred_element_type=F32)
    cw = cw_ref[...]
    xc = (cb_ref[...]
          + cw[3:4] * xpad_ref[hist:hist + n, :]
          + cw[2:3] * xpad_ref[hist - 1:hist - 1 + n, :]
          + cw[1:2] * xpad_ref[hist - 2:hist - 2 + n, :]
          + cw[0:1] * xpad_ref[hist - 3:hist - 3 + n, :])
    xpad_ref[0:hist, :] = xpad_ref[n:n + hist, :]
    xcb = xc.astype(BF16)
    lam = lam_ref[...]
    sp_lam = jnp.maximum(-lam, 0.0) + jnp.log1p(jnp.exp(-jnp.abs(lam)))
    for blk in range(LRU_BLOCKS):
        sl = slice(blk * LRU_BLOCK_W, (blk + 1) * LRU_BLOCK_W)
        xb = xcb[:, sl]
        pre_r = jnp.dot(xb, wa_ref[blk], preferred_element_type=F32) + ba_ref[:, sl]
        pre_i = jnp.dot(xb, wi_ref[blk], preferred_element_type=F32) + bi_ref[:, sl]
        r = 0.5 * jnp.tanh(0.5 * pre_r) + 0.5
        i = 0.5 * jnp.tanh(0.5 * pre_i) + 0.5
        log_a = (-LRU_C * r) * sp_lam[:, sl]
        a = jnp.exp(log_a)
        m2 = -jnp.tanh(log_a) * (a * a + 1.0)
        mult = m2 * lax.rsqrt(jnp.maximum(m2, 1e-30))
        u = mult * (i * xc[:, sl])
        for k in range(LRU_BLOCK_W // LANES):
            slab = blk * (LRU_BLOCK_W // LANES) + k
            for ch in range(SUBLANES):
                rows = slice(ch * steps, (ch + 1) * steps)
                dst = slice(ch * REC_PITCH, ch * REC_PITCH + steps)
                a_ref[slab, dst, :] = a[rows, k * LANES:(k + 1) * LANES]
                u_ref[slab, dst, :] = u[rows, k * LANES:(k + 1) * LANES]

    def step(l, state):
        rows = pl.ds(l, SUBLANES, stride=REC_PITCH)
        new = []
        for slab in range(n_slabs):
            h, p = state[slab]
            av = a_ref[slab, rows, :]
            h = av * h + u_ref[slab, rows, :]
            p = p * av
            hl_ref[slab, rows, :] = h
            p_ref[slab, rows, :] = p
            new.append((h, p))
        return tuple(new)

    zeros = jnp.zeros((SUBLANES, LANES), F32)
    ones = jnp.ones((SUBLANES, LANES), F32)
    ends = lax.fori_loop(0, steps, step, tuple((zeros, ones) for _ in range(n_slabs)), unroll=4)

    rowid = lax.broadcasted_iota(jnp.int32, (SUBLANES, LANES), 0)
    for slab in range(n_slabs):
        lanes = slice(slab * LANES, (slab + 1) * LANES)
        b, a = ends[slab]
        for d in (1, 2, 4):
            keep = rowid >= d
            a_sh = jnp.where(keep, pltpu.roll(a, d, 0), 1.0)
            b_sh = jnp.where(keep, pltpu.roll(b, d, 0), 0.0)
            b = a * b_sh + b
            a = a * a_sh
        h_in = h_ref[:, lanes]
        after = b + a * h_in
        h_ref[:, lanes] = after[SUBLANES - 1:, :]
        start = jnp.where(rowid >= 1, pltpu.roll(after, 1, 0), h_in)
        for ch in range(SUBLANES):
            src = slice(ch * REC_PITCH, ch * REC_PITCH + steps)
            rows = slice(ch * steps, (ch + 1) * steps)
            hs = hl_ref[slab, src, :] + p_ref[slab, src, :] * start[ch:ch + 1, :]
            y_ref[0, rows, lanes] = (hs * gate[rows, lanes]).astype(y_ref.dtype)


def rglru_mixer(x, g, w_in_stack, conv_w, conv_b, w_a_stack, b_a, w_i_stack, b_i, lam, j):
    b, s, d = x.shape
    width = w_in_stack.shape[2] // 2
    const2 = lambda bi, t: (0, 0)
    scan_rows = SUBLANES * REC_PITCH
    return pl.pallas_call(
        _rglru_kernel,
        out_shape=jax.ShapeDtypeStruct((b, s, width), BF16),
        grid=(b, s // REC_TILE),
        in_specs=[
            pl.BlockSpec((1, REC_TILE, d), lambda bi, t: (bi, t, 0)),
            pl.BlockSpec((1, d), const2),
            _layer_block(w_in_stack, j),
            pl.BlockSpec(conv_w.shape, const2),
            pl.BlockSpec((1, width), const2),
            _layer_block(w_a_stack, j),
            pl.BlockSpec((1, width), const2),
            _layer_block(w_i_stack, j),
            pl.BlockSpec((1, width), const2),
            pl.BlockSpec((1, width), const2),
        ],
        out_specs=pl.BlockSpec((1, REC_TILE, width), lambda bi, t: (bi, t, 0)),
        scratch_shapes=[
            pltpu.VMEM((REC_TILE + SUBLANES, width), F32),
            pltpu.VMEM((width // LANES, scan_rows, LANES), F32),
            pltpu.VMEM((width // LANES, scan_rows, LANES), F32),
            pltpu.VMEM((width // LANES, scan_rows, LANES), F32),
            pltpu.VMEM((width // LANES, scan_rows, LANES), F32),
            pltpu.VMEM((1, width), F32),
        ],
        compiler_params=_params(("arbitrary", "arbitrary")),
        name="rglru_mixer",
    )(x, g, w_in_stack, conv_w, conv_b, w_a_stack, b_a, w_i_stack, b_i, lam)


def kernel(x, attn_w_in, attn_rel_bias, attn_w_out, rg_w_in, rg_conv_w, rg_conv_b, rg_w_a, rg_b_a,
           rg_w_i, rg_b_i, rg_lambda, rg_w_out, norm_mix_pre, norm_mix_post, norm_ffn_pre,
           norm_ffn_post, ffn_w_gate, ffn_w_up, ffn_w_down):
    b, s, d = x.shape
    depth = norm_mix_pre.shape[0]
    xf = x.reshape(b * s, d)
    row = lambda v: v.reshape(1, -1)
    attn_w_in, attn_w_out, rg_w_in, rg_w_a, rg_w_i, rg_w_out, ffn_w_gate, ffn_w_up, ffn_w_down = (
        w.astype(BF16) for w in (attn_w_in, attn_w_out, rg_w_in, rg_w_a, rg_w_i, rg_w_out,
                                 ffn_w_gate, ffn_w_up, ffn_w_down))
    for layer in range(depth):
        j = layer // 2
        g_pre = row(norm_mix_pre[layer])
        if layer % 2 == 0:
            proj = norm_linear(xf, g_pre, attn_w_in, j, BF16).reshape(b, s, -1)
            out_a = band_attention(proj, band_bias(attn_rel_bias[j]), 0)
            out_b = stick_breaking_attention(proj, 3 * A_W)
            ins = [out_a.reshape(b * s, A_W), out_b.reshape(b * s, B_W)]
            w_out = attn_w_out
        else:
            width = rg_w_out.shape[1]
            y = rglru_mixer(
                xf.reshape(b, s, d), g_pre, rg_w_in,
                rg_conv_w[j].reshape(-1, width), row(rg_conv_b[j]),
                rg_w_a, row(rg_b_a[j]), rg_w_i, row(rg_b_i[j]), row(rg_lambda[j]), j)
            ins = [y.reshape(b * s, width)]
            w_out = rg_w_out
        xf = mixer_out_ffn(
            xf, row(norm_mix_post[layer]), ins, w_out, j,
            row(norm_ffn_pre[layer]), row(norm_ffn_post[layer]),
            ffn_w_gate, ffn_w_up, ffn_w_down, layer)
    return xf.reshape(b, s, d)
```

```python
import functools

import jax
import jax.numpy as jnp
from jax import lax
from jax.experimental import pallas as pl
from jax.experimental.pallas import tpu as pltpu

F32 = jnp.float32
BF16 = jnp.bfloat16

HEAD_DIM = 64
CHUNK = 64
N_LEFT_CHUNKS = 8
REL_CLIP = 256
A_W = 512
B_W = 512
LRU_BLOCKS = 4
LRU_BLOCK_W = 256
LRU_C = 8.0
RMS_EPS = 1e-6

LANES = 128
SUBLANES = 8
MXU_TILE = 256
VMEM_LIMIT = 56 * 1024 * 1024

ROW_TILE = 512
SB_TILE = 256
SB_SUBTILES = 2
SB_PAIRS = 4
SB_EXIT = 128.0
A_TILE = 2 * CHUNK
A_SUBTILES = 64
A_WIN = (N_LEFT_CHUNKS + 2) * CHUNK
A_PAD = N_LEFT_CHUNKS * CHUNK
REC_TILE = 512
REC_PITCH = REC_TILE // SUBLANES + SUBLANES
NEG_BIG = -1e30
LOG2E = 1.4426950408889634

_NT = (((1,), (1,)), ((), ()))


def _params(sem):
    return pltpu.CompilerParams(dimension_semantics=sem, vmem_limit_bytes=VMEM_LIMIT)


def _rms(x, g):
    ms = jnp.mean(x * x, axis=-1, keepdims=True)
    return x * lax.rsqrt(ms + RMS_EPS) * g


def _sigmoid(x):
    return 1.0 / (1.0 + jnp.exp(-x))


def _resident(shape):
    return pl.BlockSpec(shape, lambda *_: (0,) * len(shape), pipeline_mode=pl.Buffered(1))


def _layer_block(stack, layer, rows=None, row_block=0):
    shape = stack.shape[1:] if rows is None else (rows,) + stack.shape[2:]
    index = (layer, row_block) + (0,) * (len(shape) - 1)
    return pl.BlockSpec((None,) + shape, lambda *_: index, pipeline_mode=pl.Buffered(1))


def _norm_linear_kernel(x_ref, g_ref, w_ref, o_ref):
    h = _rms(x_ref[...], g_ref[...]).astype(BF16)
    o_ref[...] = jnp.dot(h, w_ref[...], preferred_element_type=F32).astype(o_ref.dtype)


def norm_linear(x, g, w_stack, layer, out_dtype):
    m, d = x.shape
    n = w_stack.shape[2]
    return pl.pallas_call(
        _norm_linear_kernel,
        out_shape=jax.ShapeDtypeStruct((m, n), out_dtype),
        grid=(m // ROW_TILE,),
        in_specs=[
            pl.BlockSpec((ROW_TILE, d), lambda i: (i, 0)),
            pl.BlockSpec((1, d), lambda i: (0, 0)),
            _layer_block(w_stack, layer),
        ],
        out_specs=pl.BlockSpec((ROW_TILE, n), lambda i: (i, 0)),
        compiler_params=_params(("arbitrary",)),
        name="norm_linear",
    )(x, g, w_stack)


def _out_ffn_kernel(n_in, *refs):
    x_ref, gmix_ref, gpre_ref, gpost_ref, wg_ref, wu_ref, wd_ref = refs[:7]
    ins = refs[7:7 + n_in]
    ws = refs[7 + n_in:7 + 2 * n_in]
    o_ref = refs[7 + 2 * n_in]
    m = jnp.dot(ins[0][...], ws[0][...], preferred_element_type=F32)
    for a, w in zip(ins[1:], ws[1:]):
        m = m + jnp.dot(a[...], w[...], preferred_element_type=F32)
    x = x_ref[...] + _rms(m, gmix_ref[...])
    h = _rms(x, gpre_ref[...]).astype(BF16)
    d_ff = wg_ref.shape[1]
    split = (d_ff // MXU_TILE + 1) // 2 * MXU_TILE
    f = None
    for cols in (slice(0, split), slice(split, d_ff)):
        gate = jnp.dot(h, wg_ref[:, cols], preferred_element_type=F32)
        up = jnp.dot(h, wu_ref[:, cols], preferred_element_type=F32)
        act = (gate * _sigmoid(gate) * up).astype(BF16)
        part = jnp.dot(act, wd_ref[cols, :], preferred_element_type=F32)
        f = part if f is None else f + part
    o_ref[...] = x + _rms(f, gpost_ref[...])


def mixer_out_ffn(x, gmix, ins, w_out_stack, j, gpre, gpost, wg_stack, wu_stack, wd_stack, layer):
    m, d = x.shape
    n_in = len(ins)
    in_specs = [pl.BlockSpec((ROW_TILE, d), lambda i: (i, 0))]
    in_specs += [_resident(a.shape) for a in (gmix, gpre, gpost)]
    in_specs += [_layer_block(w, layer) for w in (wg_stack, wu_stack, wd_stack)]
    in_specs += [pl.BlockSpec((ROW_TILE, a.shape[1]), lambda i: (i, 0)) for a in ins]
    in_specs += [_layer_block(w_out_stack, j, a.shape[1], k) for k, a in enumerate(ins)]
    return pl.pallas_call(
        functools.partial(_out_ffn_kernel, n_in),
        out_shape=jax.ShapeDtypeStruct((m, d), F32),
        grid=(m // ROW_TILE,),
        in_specs=in_specs,
        out_specs=pl.BlockSpec((ROW_TILE, d), lambda i: (i, 0)),
        compiler_params=_params(("arbitrary",)),
        name="mixer_out_ffn",
    )(x, gmix, gpre, gpost, wg_stack, wu_stack, wd_stack, *ins, *([w_out_stack] * n_in))


def _band_attn_kernel(q_ref, k_ref, v_ref, bias_ref, o_ref, kp_ref, vp_ref, s_ref, p_ref, rinv_ref):
    t = pl.program_id(2)

    n_head_rows = A_PAD + A_WIN - A_TILE

    @pl.when(t == 0)
    def _():
        kp_ref[0:A_PAD, :] = jnp.zeros((A_PAD, LANES), BF16)
        vp_ref[0:A_PAD, :] = jnp.zeros((A_PAD, LANES), BF16)
        kp_ref[A_PAD:, :] = k_ref[0, 0:n_head_rows - A_PAD, :]
        vp_ref[A_PAD:, :] = v_ref[0, 0:n_head_rows - A_PAD, :]

    lane = lax.broadcasted_iota(jnp.int32, (1, LANES), 1)
    first = lane < HEAD_DIM
    col = lax.broadcasted_iota(jnp.int32, (1, A_WIN), 1)
    bias = jnp.concatenate([bias_ref[0], bias_ref[1]], axis=0)
    zero = jnp.zeros((A_TILE, LANES), BF16)

    def window(ref, pad_ref, u, head_of_sequence):
        tile = t * A_SUBTILES + u
        if head_of_sequence:
            return tile, pad_ref[pl.ds(pl.multiple_of(tile * A_TILE, A_TILE), A_WIN), :]
        return tile, ref[0, pl.ds(pl.multiple_of(tile * A_TILE - A_PAD, A_TILE), A_WIN), :]

    def scores(u, slot, head_of_sequence):
        tile, kw = window(k_ref, kp_ref, u, head_of_sequence)
        q2 = q_ref[0, pl.ds(pl.multiple_of(u * A_TILE, A_TILE), A_TILE), :] * (HEAD_DIM ** -0.5)
        qs = jnp.concatenate([jnp.where(first, q2, zero), jnp.where(first, zero, q2)], axis=0)
        s = lax.dot_general(qs, kw, _NT, preferred_element_type=F32) + bias
        if head_of_sequence:
            s = jnp.where(col >= (A_PAD - tile * A_TILE), s, NEG_BIG)
        s_ref[slot] = s

    def softmax(slot):
        s = s_ref[slot]
        p = jnp.exp(s - jnp.max(s, axis=-1, keepdims=True))
        p_ref[slot] = p.astype(BF16)
        rinv_ref[slot] = jnp.broadcast_to(1.0 / jnp.sum(p, axis=-1, keepdims=True), (2 * A_TILE, LANES))

    def values(u, slot, head_of_sequence):
        _, vw = window(v_ref, vp_ref, u, head_of_sequence)
        o = jnp.dot(p_ref[slot], vw, preferred_element_type=F32) * rinv_ref[slot]
        out = jnp.where(first, o[:A_TILE], o[A_TILE:])
        o_ref[0, pl.ds(pl.multiple_of(u * A_TILE, A_TILE), A_TILE), :] = out.astype(o_ref.dtype)

    scores(0, 0, True)
    scores(1, 1, True)
    softmax(0)

    def pair(k, scores_head, values_head):
        values(2 * k - 2, 0, values_head)
        softmax(1)
        scores(2 * k, 0, scores_head)
        values(2 * k - 1, 1, values_head)
        softmax(0)
        scores(2 * k + 1, 1, scores_head)

    n_head = A_PAD // A_TILE // 2
    for k in range(1, n_head + 1):
        pair(k, k < n_head, True)

    def body(k, _):
        pair(k, False, False)
        return 0

    lax.fori_loop(n_head + 1, A_SUBTILES // 2, body, 0)
    values(A_SUBTILES - 2, 0, False)
    softmax(1)
    values(A_SUBTILES - 1, 1, False)


def band_attention(proj, bias, col0):
    b, s, _ = proj.shape
    nq = col0 // LANES
    nk = (col0 + A_W) // LANES
    nv = (col0 + 2 * A_W) // LANES
    tq = A_SUBTILES * A_TILE
    assert s == tq
    return pl.pallas_call(
        _band_attn_kernel,
        out_shape=jax.ShapeDtypeStruct((b, s, A_W), BF16),
        grid=(b, A_W // LANES, s // tq),
        in_specs=[
            pl.BlockSpec((1, tq, LANES), lambda bi, p, t: (bi, t, nq + p)),
            pl.BlockSpec((1, s, LANES), lambda bi, p, t: (bi, 0, nk + p)),
            pl.BlockSpec((1, s, LANES), lambda bi, p, t: (bi, 0, nv + p)),
            pl.BlockSpec((2, A_TILE, A_WIN), lambda bi, p, t: (p, 0, 0)),
        ],
        out_specs=pl.BlockSpec((1, tq, LANES), lambda bi, p, t: (bi, t, p)),
        scratch_shapes=[
            pltpu.VMEM((A_PAD + A_WIN - A_TILE, LANES), BF16),
            pltpu.VMEM((A_PAD + A_WIN - A_TILE, LANES), BF16),
            pltpu.VMEM((2, 2 * A_TILE, A_WIN), F32),
            pltpu.VMEM((2, 2 * A_TILE, A_WIN), BF16),
            pltpu.VMEM((2, 2 * A_TILE, LANES), F32),
        ],
        compiler_params=_params(("arbitrary", "arbitrary", "arbitrary")),
        name="band_attention",
    )(proj, proj, proj, bias)


def band_bias(rel_bias):
    h = rel_bias.shape[0]
    n_f = A_WIN + A_TILE - 1
    n_const = A_PAD + A_TILE - 1 - REL_CLIP
    far = jnp.broadcast_to(rel_bias[:, 2 * REL_CLIP:], (h, n_const))
    near = rel_bias[:, 2 * REL_CLIP + 1 - (n_f - n_const):][:, ::-1]
    f = jnp.concatenate([far, near], axis=1)
    period = n_f + 1
    g = jnp.concatenate([f[:, A_TILE - 1:], jnp.zeros((h, 1), F32), f[:, :A_TILE - 1]], axis=1)
    flat = jnp.tile(g, (1, A_TILE))[:, :A_TILE * (period - 1)]
    toep = flat.reshape(h, A_TILE, period - 1)[:, :, :A_WIN]
    r = jnp.arange(A_TILE)[:, None]
    c = jnp.arange(A_WIN)[None, :]
    dchunk = c // CHUNK - r // CHUNK
    ok = (dchunk >= 0) & (dchunk <= N_LEFT_CHUNKS)
    return jnp.where(ok[None], toep, NEG_BIG).astype(F32)


def _stick_kernel(q_ref, k_ref, v_ref, o_ref, acc_ref, carry_ref):
    qi = pl.program_id(2)
    t = SB_TILE
    ns = SB_SUBTILES
    lane = lax.broadcasted_iota(jnp.int32, (1, LANES), 1)
    first = lane < HEAD_DIM
    row = lax.broadcasted_iota(jnp.int32, (t, t), 0)
    col = lax.broadcasted_iota(jnp.int32, (t, t), 1)
    causal = col < row
    true = jnp.ones((t, t), jnp.bool_)
    suffix = jnp.where(row >= col, 1.0, 0.0).astype(BF16)
    zero = jnp.zeros((t, LANES), BF16)

    acc_ref[...] = jnp.zeros_like(acc_ref)
    carry_ref[...] = jnp.zeros_like(carry_ref)

    def lanes_of(pair):
        return slice(pair * LANES, (pair + 1) * LANES)

    def split_heads(x):
        return jnp.concatenate([jnp.where(first, x, zero), jnp.where(first, zero, x)], axis=0)

    def scores(pair, j, d, mask):
        start = pl.multiple_of(j * t, t)
        kcat = split_heads(k_ref[0, pl.ds(start, t), lanes_of(pair)])
        q2 = q_ref[0, d * t:, lanes_of(pair)] * (HEAD_DIM ** -0.5)
        z = lax.dot_general(q2, kcat, _NT, preferred_element_type=F32)
        sp = jnp.maximum(z, 0.0) + jnp.log(1.0 + jnp.exp2(jnp.abs(z) * (-LOG2E)))
        if mask is not None:
            sp = jnp.where(mask, sp, 0.0)
        spb = sp.astype(BF16)
        return z - (sp - spb.astype(F32)), spb

    def weights(pair, j, d, mask, tz, spb):
        r0 = d * t
        n = (ns - d) * t
        start = pl.multiple_of(j * t, t)
        vcat = split_heads(v_ref[0, pl.ds(start, t), lanes_of(pair)])
        c = [jnp.dot(spb[:, h * t:(h + 1) * t], suffix, preferred_element_type=F32) for h in range(2)]
        carry = [carry_ref[pair, h, r0:, :] for h in range(2)]
        reps = t // LANES
        shift = jnp.concatenate([c[0]] + [c[1]], axis=1) + jnp.concatenate(
            [carry[0]] * reps + [carry[1]] * reps, axis=1)
        w = jnp.exp(tz - shift)
        if mask is not None:
            w = jnp.where(mask, w, 0.0)
        for h in range(2):
            carry_ref[pair, h, r0:, :] = carry[h] + jnp.broadcast_to(c[h][:, 0:1], (n, LANES))
        acc_ref[pair, r0:, :] += jnp.dot(w.astype(BF16), vcat, preferred_element_type=F32)

    for pair in range(SB_PAIRS):
        for d in range(ns - 1, -1, -1):
            mask = jnp.concatenate([causal] + [true] * (ns - d - 1), axis=0)
            mask = jnp.concatenate([mask, mask], axis=1)
            tz, spb = scores(pair, ns * qi + d, d, mask)
            weights(pair, ns * qi + d, d, mask, tz, spb)

    def saturated(pair):
        low = jnp.min(jnp.minimum(carry_ref[pair, 0], carry_ref[pair, 1]), axis=0, keepdims=True)
        return (jnp.min(low) >= SB_EXIT).astype(jnp.int32)

    @pl.when(qi > 0)
    def _():
        for pair in range(SB_PAIRS):
            tz, spb = scores(pair, ns * qi - 1, 0, None)
            weights(pair, ns * qi - 1, 0, None, tz, spb)
        done_after_first = [saturated(pair) for pair in range(SB_PAIRS)]
        for pair in range(SB_PAIRS):
            def cond(state):
                i, done = state
                return jnp.logical_and(i < ns * qi, done == 0)

            def body(state, pair=pair):
                i, _ = state
                tz, spb = scores(pair, ns * qi - 1 - i, 0, None)
                weights(pair, ns * qi - 1 - i, 0, None, tz, spb)
                return i + 1, saturated(pair)

            lax.while_loop(cond, body, (jnp.int32(1), done_after_first[pair]))

    for pair in range(SB_PAIRS):
        o_ref[0, :, lanes_of(pair)] = acc_ref[pair].astype(o_ref.dtype)


def stick_breaking_attention(proj, col0):
    b, s, _ = proj.shape
    width = SB_PAIRS * LANES
    nq = col0 // width
    nk = (col0 + B_W) // width
    nv = (col0 + 2 * B_W) // width
    tq = SB_SUBTILES * SB_TILE
    return pl.pallas_call(
        _stick_kernel,
        out_shape=jax.ShapeDtypeStruct((b, s, B_W), BF16),
        grid=(b, B_W // width, s // tq),
        in_specs=[
            pl.BlockSpec((1, tq, width), lambda bi, p, t: (bi, t, nq + p)),
            pl.BlockSpec((1, s, width), lambda bi, p, t: (bi, 0, nk + p)),
            pl.BlockSpec((1, s, width), lambda bi, p, t: (bi, 0, nv + p)),
        ],
        out_specs=pl.BlockSpec((1, tq, width), lambda bi, p, t: (bi, t, p)),
        scratch_shapes=[
            pltpu.VMEM((SB_PAIRS, tq, LANES), F32),
            pltpu.VMEM((SB_PAIRS, 2, tq, LANES), F32),
        ],
        compiler_params=_params(("arbitrary", "arbitrary", "arbitrary")),
        name="stick_breaking",
    )(proj, proj, proj)


def _rglru_kernel(x_ref, g_ref, win_ref, cw_ref, cb_ref, wa_ref, ba_ref, wi_ref, bi_ref, lam_ref,
                  y_ref, xpad_ref, a_ref, u_ref, p_ref, hl_ref, h_ref):
    t = pl.program_id(1)
    n = REC_TILE
    steps = n // SUBLANES
    width = y_ref.shape[2]
    n_slabs = width // LANES

    @pl.when(t == 0)
    def _():
        xpad_ref[0:SUBLANES, :] = jnp.zeros((SUBLANES, width), F32)
        h_ref[...] = jnp.zeros_like(h_ref)

    hn = _rms(x_ref[0], g_ref[...]).astype(BF16)
    gate = jnp.dot(hn, win_ref[:, :width], preferred_element_type=F32)
    c = 0.7978845608028654
    gate = 0.5 * gate * (1.0 + jnp.tanh(c * (gate + 0.044715 * (gate * gate * gate))))
    hist = SUBLANES
    xpad_ref[hist:hist + n, :] = jnp.dot(hn, win_ref[:, width:], preferred_element_type=F32)
    cw = cw_ref[...]
    xc = (cb_ref[...]
          + cw[3:4] * xpad_ref[hist:hist + n, :]
          + cw[2:3] * xpad_ref[hist - 1:hist - 1 + n, :]
          + cw[1:2] * xpad_ref[hist - 2:hist - 2 + n, :]
          + cw[0:1] * xpad_ref[hist - 3:hist - 3 + n, :])
    xpad_ref[0:hist, :] = xpad_ref[n:n + hist, :]
    xcb = xc.astype(BF16)
    lam = lam_ref[...]
    sp_lam = jnp.maximum(-lam, 0.0) + jnp.log1p(jnp.exp(-jnp.abs(lam)))
    for blk in range(LRU_BLOCKS):
        sl = slice(blk * LRU_BLOCK_W, (blk + 1) * LRU_BLOCK_W)
        xb = xcb[:, sl]
        pre_r = jnp.dot(xb, wa_ref[blk], preferred_element_type=F32) + ba_ref[:, sl]
        pre_i = jnp.dot(xb, wi_ref[blk], preferred_element_type=F32) + bi_ref[:, sl]
        r = 0.5 * jnp.tanh(0.5 * pre_r) + 0.5
        i = 0.5 * jnp.tanh(0.5 * pre_i) + 0.5
        log_a = (-LRU_C * r) * sp_lam[:, sl]
        a = jnp.exp(log_a)
        m2 = -jnp.tanh(log_a) * (a * a + 1.0)
        mult = m2 * lax.rsqrt(jnp.maximum(m2, 1e-30))
        u = mult * (i * xc[:, sl])
        for k in range(LRU_BLOCK_W // LANES):
            slab = blk * (LRU_BLOCK_W // LANES) + k
            for ch in range(SUBLANES):
                rows = slice(ch * steps, (ch + 1) * steps)
                dst = slice(ch * REC_PITCH, ch * REC_PITCH + steps)
                a_ref[slab, dst, :] = a[rows, k * LANES:(k + 1) * LANES]
                u_ref[slab, dst, :] = u[rows, k * LANES:(k + 1) * LANES]

    def step(l, state):
        rows = pl.ds(l, SUBLANES, stride=REC_PITCH)
        new = []
        for slab in range(n_slabs):
            h, p = state[slab]
            av = a_ref[slab, rows, :]
            h = av * h + u_ref[slab, rows, :]
            p = p * av
            hl_ref[slab, rows, :] = h
            p_ref[slab, rows, :] = p
            new.append((h, p))
        return tuple(new)

    zeros = jnp.zeros((SUBLANES, LANES), F32)
    ones = jnp.ones((SUBLANES, LANES), F32)
    ends = lax.fori_loop(0, steps, step, tuple((zeros, ones) for _ in range(n_slabs)), unroll=4)

    rowid = lax.broadcasted_iota(jnp.int32, (SUBLANES, LANES), 0)
    for slab in range(n_slabs):
        lanes = slice(slab * LANES, (slab + 1) * LANES)
        b, a = ends[slab]
        for d in (1, 2, 4):
            keep = rowid >= d
            a_sh = jnp.where(keep, pltpu.roll(a, d, 0), 1.0)
            b_sh = jnp.where(keep, pltpu.roll(b, d, 0), 0.0)
            b = a * b_sh + b
            a = a * a_sh
        h_in = h_ref[:, lanes]
        after = b + a * h_in
        h_ref[:, lanes] = after[SUBLANES - 1:, :]
        start = jnp.where(rowid >= 1, pltpu.roll(after, 1, 0), h_in)
        for ch in range(SUBLANES):
            src = slice(ch * REC_PITCH, ch * REC_PITCH + steps)
            rows = slice(ch * steps, (ch + 1) * steps)
            hs = hl_ref[slab, src, :] + p_ref[slab, src, :] * start[ch:ch + 1, :]
            y_ref[0, rows, lanes] = (hs * gate[rows, lanes]).astype(y_ref.dtype)


def rglru_mixer(x, g, w_in_stack, conv_w, conv_b, w_a_stack, b_a, w_i_stack, b_i, lam, j):
    b, s, d = x.shape
    width = w_in_stack.shape[2] // 2
    const2 = lambda bi, t: (0, 0)
    scan_rows = SUBLANES * REC_PITCH
    return pl.pallas_call(
        _rglru_kernel,
        out_shape=jax.ShapeDtypeStruct((b, s, width), BF16),
        grid=(b, s // REC_TILE),
        in_specs=[
            pl.BlockSpec((1, REC_TILE, d), lambda bi, t: (bi, t, 0)),
            pl.BlockSpec((1, d), const2),
            _layer_block(w_in_stack, j),
            pl.BlockSpec(conv_w.shape, const2),
            pl.BlockSpec((1, width), const2),
            _layer_block(w_a_stack, j),
            pl.BlockSpec((1, width), const2),
            _layer_block(w_i_stack, j),
            pl.BlockSpec((1, width), const2),
            pl.BlockSpec((1, width), const2),
        ],
        out_specs=pl.BlockSpec((1, REC_TILE, width), lambda bi, t: (bi, t, 0)),
        scratch_shapes=[
            pltpu.VMEM((REC_TILE + SUBLANES, width), F32),
            pltpu.VMEM((width // LANES, scan_rows, LANES), F32),
            pltpu.VMEM((width // LANES, scan_rows, LANES), F32),
            pltpu.VMEM((width // LANES, scan_rows, LANES), F32),
            pltpu.VMEM((width // LANES, scan_rows, LANES), F32),
            pltpu.VMEM((1, width), F32),
        ],
        compiler_params=_params(("arbitrary", "arbitrary")),
        name="rglru_mixer",
    )(x, g, w_in_stack, conv_w, conv_b, w_a_stack, b_a, w_i_stack, b_i, lam)


def kernel(x, attn_w_in, attn_rel_bias, attn_w_out, rg_w_in, rg_conv_w, rg_conv_b, rg_w_a, rg_b_a,
           rg_w_i, rg_b_i, rg_lambda, rg_w_out, norm_mix_pre, norm_mix_post, norm_ffn_pre,
           norm_ffn_post, ffn_w_gate, ffn_w_up, ffn_w_down):
    b, s, d = x.shape
    depth = norm_mix_pre.shape[0]
    xf = x.reshape(b * s, d)
    row = lambda v: v.reshape(1, -1)
    attn_w_in, attn_w_out, rg_w_in, rg_w_a, rg_w_i, rg_w_out, ffn_w_gate, ffn_w_up, ffn_w_down = (
        w.astype(BF16) for w in (attn_w_in, attn_w_out, rg_w_in, rg_w_a, rg_w_i, rg_w_out,
                                 ffn_w_gate, ffn_w_up, ffn_w_down))
    for layer in range(depth):
        j = layer // 2
        g_pre = row(norm_mix_pre[layer])
        if layer % 2 == 0:
            proj = norm_linear(xf, g_pre, attn_w_in, j, BF16).reshape(b, s, -1)
            out_a = band_attention(proj, band_bias(attn_rel_bias[j]), 0)
            out_b = stick_breaking_attention(proj, 3 * A_W)
            ins = [out_a.reshape(b * s, A_W), out_b.reshape(b * s, B_W)]
            w_out = attn_w_out
        else:
            width = rg_w_out.shape[1]
            y = rglru_mixer(
                xf.reshape(b, s, d), g_pre, rg_w_in,
                rg_conv_w[j].reshape(-1, width), row(rg_conv_b[j]),
                rg_w_a, row(rg_b_a[j]), rg_w_i, row(rg_b_i[j]), row(rg_lambda[j]), j)
            ins = [y.reshape(b * s, width)]
            w_out = rg_w_out
        xf = mixer_out_ffn(
            xf, row(norm_mix_post[layer]), ins, w_out, j,
            row(norm_ffn_pre[layer]), row(norm_ffn_post[layer]),
            ffn_w_gate, ffn_w_up, ffn_w_down, layer)
    return xf.reshape(b, s, d)
```
